```python
import jax
import jax.numpy as jnp
from jax import lax
import numpy as np

D_MODEL = 2048
BATCH = 8
SEQ = 2048
DEPTH = 2
DEC_BATCH = 128
DEC_SEQ = 4
PAST_LEN = 8192
PAGE_SIZE = 128

N_META = 16
POOL_WINDOWS = (2, 4, 8, 16)
N_POOL_GROUPS = len(POOL_WINDOWS)
POOL_GROUP_DIM = D_MODEL // N_POOL_GROUPS
POOL_STATE = max(POOL_WINDOWS) - 1
HEAD_DIM = 64
N_HEADS = D_MODEL // HEAD_DIM
N_KV_HEADS = 4
GROUP = N_HEADS // N_KV_HEADS
WINDOW = 128
BLOCK = 128
D_FF = 5632
N_EXPERTS = 8
TOP_K = 2
D_FF_EXPERT = 7168
EPS = 1e-5

kernel_name = "yoco_pool_swa_sink_moe_step"


def rmsnorm(x, g):
    xf = x.astype(jnp.float32)
    y = xf * lax.rsqrt(jnp.mean(xf * xf, axis=-1, keepdims=True) + EPS)
    return (y * g.astype(jnp.float32)).astype(x.dtype)


def pool_mixer(u, prefix, pos0, w_pool, ls_pool):
    B, T, D = u.shape
    P = POOL_STATE
    u_ext = jnp.concatenate([prefix.astype(u.dtype), u], axis=1)
    cs = jnp.cumsum(u_ext.astype(jnp.float32), axis=1)
    cs = jnp.concatenate([jnp.zeros((B, 1, D), jnp.float32), cs], axis=1)
    pos = pos0 + jnp.arange(T)
    means = []
    for g, w in enumerate(POOL_WINDOWS):
        c0, c1 = g * POOL_GROUP_DIM, (g + 1) * POOL_GROUP_DIM
        win_sum = cs[:, P + 1:, c0:c1] - cs[:, P + 1 - w:P + 1 - w + T, c0:c1]
        cnt = jnp.minimum(w, pos + 1).astype(jnp.float32)[None, :, None]
        means.append(win_sum / cnt)
    pooled = jnp.stack(means, axis=2)
    mix = pooled - u.astype(jnp.float32).reshape(B, T, N_POOL_GROUPS, POOL_GROUP_DIM)
    out = jnp.einsum('btgc,gcd->btgd', mix.astype(u.dtype), w_pool).reshape(B, T, D) * ls_pool
    return out, u_ext[:, -P:]


def swiglu(u, w1, w3, w2):
    return (jax.nn.silu(u @ w1) * (u @ w3)) @ w2


def moe_ffn(u, w_router, w_e1, w_e3, w_e2):
    logits = jnp.einsum('btd,de->bte', u, w_router, preferred_element_type=jnp.float32)
    top_v, top_i = lax.top_k(logits, TOP_K)
    gates = jax.nn.softmax(top_v, axis=-1)
    combine = jnp.sum(jax.nn.one_hot(top_i, N_EXPERTS, dtype=jnp.float32) * gates[..., None], axis=-2)
    y = jnp.zeros(u.shape, jnp.float32)
    for e in range(N_EXPERTS):
        y = y + combine[..., e:e + 1] * swiglu(u, w_e1[e], w_e3[e], w_e2[e]).astype(jnp.float32)
    return y.astype(u.dtype)


def swa_sink(q, k_new, v_new, k_past, v_past, pos0, sinks):
    B, T = q.shape[:2]
    W = k_past.shape[1]
    Tp = -(-T // BLOCK) * BLOCK
    nb = Tp // BLOCK
    pad_front = jnp.zeros((B, BLOCK - W, N_KV_HEADS, HEAD_DIM), k_new.dtype)
    tail = ((0, 0), (0, Tp - T), (0, 0), (0, 0))
    k_ext = jnp.concatenate([pad_front, k_past.astype(k_new.dtype), jnp.pad(k_new, tail)], axis=1)
    v_ext = jnp.concatenate([pad_front, v_past.astype(v_new.dtype), jnp.pad(v_new, tail)], axis=1)
    kb = k_ext.reshape(B, nb + 1, BLOCK, N_KV_HEADS, HEAD_DIM)
    vb = v_ext.reshape(B, nb + 1, BLOCK, N_KV_HEADS, HEAD_DIM)
    k_band = jnp.concatenate([kb[:, :-1], kb[:, 1:]], axis=2)
    v_band = jnp.concatenate([vb[:, :-1], vb[:, 1:]], axis=2)
    qb = jnp.pad(q, tail).reshape(B, nb, BLOCK, N_KV_HEADS, GROUP, HEAD_DIM)
    s = jnp.einsum('bnqkgd,bnskd->bnkgqs', qb, k_band,
                   preferred_element_type=jnp.float32) * (HEAD_DIM ** -0.5)
    r = jnp.arange(BLOCK)[:, None]
    c = jnp.arange(2 * BLOCK)[None, :]
    diff = r + BLOCK - c
    key_pos = pos0 + jnp.arange(nb)[:, None] * BLOCK + c - BLOCK
    valid = ((diff >= 0) & (diff < WINDOW))[None] & (key_pos >= 0)[:, None, :]
    s = jnp.where(valid[None, :, None, None], s, -1e30)
    sink = sinks.astype(jnp.float32).reshape(1, 1, N_KV_HEADS, GROUP, 1, 1)
    m = jnp.maximum(jnp.max(s, axis=-1, keepdims=True), sink)
    p = jnp.exp(s - m)
    denom = jnp.sum(p, axis=-1, keepdims=True) + jnp.exp(sink - m)
    o = jnp.einsum('bnkgqs,bnskd->bnqkgd', (p / denom).astype(v_band.dtype), v_band)
    o = o.reshape(B, Tp, N_HEADS * HEAD_DIM)[:, :T]
    new_k = jnp.concatenate([k_past.astype(k_new.dtype), k_new], axis=1)[:, -W:]
    new_v = jnp.concatenate([v_past.astype(v_new.dtype), v_new], axis=1)[:, -W:]
    return o, new_k, new_v


def trunk(x, pool_prefix, k_past, v_past, pos0,
          g_pool, w_pool, ls_pool, g_ffn0, w_ff1, w_ff3, w_ff2,
          g_kv, w_kv, b_kv, g_attn, w_q, b_q, sinks, w_o, b_o,
          g_ffn1, w_router, w_e1, w_e3, w_e2, g_final):
    B, T, _ = x.shape
    h = x
    k_sh = v_sh = new_pool = new_k = new_v = None
    for layer in range(DEPTH):
        if layer == 0:
            mix, new_pool = pool_mixer(rmsnorm(h, g_pool), pool_prefix, pos0, w_pool, ls_pool)
            h = h + mix
            h = h + swiglu(rmsnorm(h, g_ffn0), w_ff1, w_ff3, w_ff2)
            kv = rmsnorm(h, g_kv) @ w_kv + b_kv
            k_sh = kv[..., :N_KV_HEADS * HEAD_DIM].reshape(B, T, N_KV_HEADS, HEAD_DIM)
            v_sh = kv[..., N_KV_HEADS * HEAD_DIM:].reshape(B, T, N_KV_HEADS, HEAD_DIM)
        else:
            q = (rmsnorm(h, g_attn) @ w_q + b_q).reshape(B, T, N_HEADS, HEAD_DIM)
            o, new_k, new_v = swa_sink(q, k_sh, v_sh, k_past, v_past, pos0, sinks)
            h = h + o @ w_o + b_o
            h = h + moe_ffn(rmsnorm(h, g_ffn1), w_router, w_e1, w_e3, w_e2)
    return rmsnorm(h, g_final), new_pool, new_k, new_v


def setup_inputs(seed: int = 0) -> dict:
    key = jax.random.key(seed)
    ks = jax.random.split(key, 32)
    f32 = jnp.float32
    W = min(WINDOW, PAST_LEN)
    HQ = N_HEADS * HEAD_DIM
    HKV = N_KV_HEADS * HEAD_DIM

    def nrm(k, shape, scale):
        return jax.random.normal(k, shape, f32) * scale

    def gain(k, n):
        return 1.0 + 0.05 * jax.random.normal(k, (n,), f32)

    return {
        "x_prompt": nrm(ks[0], (BATCH, SEQ, D_MODEL), 1.0),
        "x_sample": nrm(ks[1], (DEC_BATCH, DEC_SEQ, D_MODEL), 1.0),
        "state_pool": nrm(ks[2], (DEC_BATCH, POOL_STATE, D_MODEL), 1.0),
        "cache_k": nrm(ks[3], (DEC_BATCH, W, N_KV_HEADS, HEAD_DIM), 1.0),
        "cache_v": nrm(ks[4], (DEC_BATCH, W, N_KV_HEADS, HEAD_DIM), 1.0),
        "meta_tokens": nrm(ks[5], (N_META, D_MODEL), 1.0),
        "g_pool": gain(ks[6], D_MODEL),
        "w_pool": nrm(ks[7], (N_POOL_GROUPS, POOL_GROUP_DIM, POOL_GROUP_DIM), POOL_GROUP_DIM ** -0.5),
        "ls_pool": 1.0 + 0.1 * jax.random.normal(ks[8], (D_MODEL,), f32),
        "g_ffn0": gain(ks[9], D_MODEL),
        "w_ff1": nrm(ks[10], (D_MODEL, D_FF), D_MODEL ** -0.5),
        "w_ff3": nrm(ks[11], (D_MODEL, D_FF), D_MODEL ** -0.5),
        "w_ff2": nrm(ks[12], (D_FF, D_MODEL), D_FF ** -0.5),
        "g_kv": gain(ks[13], D_MODEL),
        "w_kv": nrm(ks[14], (D_MODEL, 2 * HKV), D_MODEL ** -0.5),
        "b_kv": nrm(ks[15], (2 * HKV,), 0.02),
        "g_attn": gain(ks[16], D_MODEL),
        "w_q": nrm(ks[17], (D_MODEL, HQ), D_MODEL ** -0.5),
        "b_q": nrm(ks[18], (HQ,), 0.02),
        "sinks": nrm(ks[19], (N_HEADS,), 0.5),
        "w_o": nrm(ks[20], (HQ, D_MODEL), HQ ** -0.5),
        "b_o": nrm(ks[21], (D_MODEL,), 0.02),
        "g_ffn1": gain(ks[22], D_MODEL),
        "w_router": nrm(ks[23], (D_MODEL, N_EXPERTS), D_MODEL ** -0.5),
        "w_e1": nrm(ks[24], (N_EXPERTS, D_MODEL, D_FF_EXPERT), D_MODEL ** -0.5),
        "w_e3": nrm(ks[25], (N_EXPERTS, D_MODEL, D_FF_EXPERT), D_MODEL ** -0.5),
        "w_e2": nrm(ks[26], (N_EXPERTS, D_FF_EXPERT, D_MODEL), D_FF_EXPERT ** -0.5),
        "g_final": gain(ks[27], D_MODEL),
    }


def reference(x_prompt, x_sample, state_pool, cache_k, cache_v, meta_tokens,
              g_pool, w_pool, ls_pool, g_ffn0, w_ff1, w_ff3, w_ff2,
              g_kv, w_kv, b_kv, g_attn, w_q, b_q, sinks, w_o, b_o,
              g_ffn1, w_router, w_e1, w_e3, w_e2, g_final):
    weights = (g_pool, w_pool, ls_pool, g_ffn0, w_ff1, w_ff3, w_ff2,
               g_kv, w_kv, b_kv, g_attn, w_q, b_q, sinks, w_o, b_o,
               g_ffn1, w_router, w_e1, w_e3, w_e2, g_final)
    B = x_prompt.shape[0]
    W = cache_k.shape[1]
    meta = jnp.broadcast_to(meta_tokens.astype(x_prompt.dtype)[None], (B, N_META, D_MODEL))
    xp = jnp.concatenate([meta, x_prompt], axis=1)
    zero_pool = jnp.zeros((B, POOL_STATE, D_MODEL), x_prompt.dtype)
    zero_kv = jnp.zeros((B, W, N_KV_HEADS, HEAD_DIM), cache_k.dtype)
    yp, pool_p, k_p, v_p = trunk(xp, zero_pool, zero_kv, zero_kv, 0, *weights)
    y_prompt = yp[:, N_META:]
    y_sample, pool_s, k_s, v_s = trunk(x_sample, state_pool, cache_k, cache_v, PAST_LEN, *weights)
    return (y_prompt, y_sample, pool_p, pool_s, k_p, v_p, k_s, v_s)
```

```python
import functools

import jax
import jax.numpy as jnp
from jax import lax
from jax.experimental import pallas as pl
from jax.experimental.pallas import tpu as pltpu

F32 = jnp.float32
BF16 = jnp.bfloat16

D_MODEL = 2048
N_META = 16
POOL_WINDOWS = (2, 4, 8, 16)
POOL_GROUP_DIM = D_MODEL // len(POOL_WINDOWS)
POOL_STATE = max(POOL_WINDOWS) - 1
HEAD_DIM = 64
N_HEADS = D_MODEL // HEAD_DIM
N_KV_HEADS = 4
GROUP = N_HEADS // N_KV_HEADS
HKV = N_KV_HEADS * HEAD_DIM
WINDOW = 128
N_EXPERTS = 8
EPS = 1e-5
PAST_LEN = 8192
MASKED = -1e30

LANES = 128
V7X_VMEM_BYTES = 64 * 2 ** 20
VMEM_LIMIT = V7X_VMEM_BYTES - 8 * 2 ** 20

POOL_TILE = 128
ATTN_TILE = WINDOW
MOE_TILE = 512
MOE_FF_TILE = 512
FFN_FF_TILE = 512
FFN_TILE = 512
PROJ_TILE = 256
ROW_DMA_TILE = 256


def _dot(a, b):
    return jnp.dot(a, b, preferred_element_type=F32)


def _dot_nt(a, b):
    return lax.dot_general(a, b, (((1,), (1,)), ((), ())), preferred_element_type=F32)


def _rmsnorm(x, g):
    return x * lax.rsqrt(jnp.mean(x * x, axis=-1, keepdims=True) + EPS) * g


def _params(n_axes):
    return pltpu.CompilerParams(dimension_semantics=("arbitrary",) * n_axes,
                                vmem_limit_bytes=VMEM_LIMIT)


def _pool_seq_kernel(x_ref, meta_ref, g_ref, w_ref, ls_ref, h_ref, st_ref, e_ref, *, pos0):
    t = pl.program_id(1)
    tt = POOL_TILE
    g = g_ref[...]

    @pl.when(t == 0)
    def _():
        e_ref[0:tt - N_META, :] = jnp.zeros((tt - N_META, D_MODEL), F32)
        e_ref[tt - N_META:tt, :] = _rmsnorm(meta_ref[...], g)

    @pl.when(t > 0)
    def _():
        e_ref[0:tt, :] = e_ref[tt:2 * tt, :]

    x = x_ref[0]
    u = _rmsnorm(x, g)
    e_ref[tt:2 * tt, :] = u

    e = e_ref[...]
    hi = e.astype(BF16)
    lo = (e - hi.astype(F32)).astype(BF16)
    r = lax.broadcasted_iota(jnp.int32, (tt, 2 * tt), 0)
    c = lax.broadcasted_iota(jnp.int32, (tt, 2 * tt), 1)
    pos = pos0 + t * tt + lax.broadcasted_iota(jnp.int32, (tt, 1), 0)
    for gi, w in enumerate(POOL_WINDOWS):
        sl = slice(gi * POOL_GROUP_DIM, (gi + 1) * POOL_GROUP_DIM)
        band = jnp.where((c > r + tt - w) & (c <= r + tt), 1.0, 0.0).astype(BF16)
        win_sum = _dot(band, hi[:, sl]) + _dot(band, lo[:, sl])
        cnt = jnp.minimum(w, pos + 1).astype(F32)
        mix = win_sum / cnt - u[:, sl]
        o = _dot(mix.astype(BF16), w_ref[gi]) * ls_ref[:, sl]
        h_ref[0, :, sl] = x[:, sl] + o

    @pl.when(t == pl.num_programs(1) - 1)
    def _():
        st_ref[0] = e_ref[2 * tt - POOL_STATE:2 * tt, :]


def _pool_seq(x, meta, g, w, ls, pos0):
    b, t, _ = x.shape
    tt = POOL_TILE
    return pl.pallas_call(
        functools.partial(_pool_seq_kernel, pos0=pos0),
        grid=(b, t // tt),
        in_specs=[
            pl.BlockSpec((1, tt, D_MODEL), lambda i, j: (i, j, 0)),
            pl.BlockSpec((N_META, D_MODEL), lambda i, j: (0, 0)),
            pl.BlockSpec((1, D_MODEL), lambda i, j: (0, 0)),
            pl.BlockSpec((len(POOL_WINDOWS), POOL_GROUP_DIM, POOL_GROUP_DIM), lambda i, j: (0, 0, 0)),
            pl.BlockSpec((1, D_MODEL), lambda i, j: (0, 0)),
        ],
        out_specs=[
            pl.BlockSpec((1, tt, D_MODEL), lambda i, j: (i, j, 0)),
            pl.BlockSpec((1, POOL_STATE, D_MODEL), lambda i, j: (i, 0, 0)),
        ],
        out_shape=[
            jax.ShapeDtypeStruct((b, t, D_MODEL), F32),
            jax.ShapeDtypeStruct((b, POOL_STATE, D_MODEL), F32),
        ],
        scratch_shapes=[pltpu.VMEM((2 * tt, D_MODEL), F32)],
        compiler_params=_params(2),
        name="pool_seq",
    )(x, meta, g, w, ls)


def _pool_step_kernel(x_ref, st_ref, g_ref, w_ref, ls_ref, h_ref, nst_ref, e_ref, *, pos0):
    n_t, bb, _ = x_ref.shape
    g = g_ref[...]
    for j in range(POOL_STATE):
        e_ref[j] = st_ref[j]
    for t in range(n_t):
        e_ref[POOL_STATE + t] = _rmsnorm(x_ref[t], g)
    for j in range(POOL_STATE):
        nst_ref[j] = e_ref[j + n_t]
    for gi, w in enumerate(POOL_WINDOWS):
        sl = slice(gi * POOL_GROUP_DIM, (gi + 1) * POOL_GROUP_DIM)
        mixes = []
        for t in range(n_t):
            win_sum = e_ref[POOL_STATE + t, :, sl]
            for i in range(1, w):
                win_sum = win_sum + e_ref[POOL_STATE + t - i, :, sl]
            cnt = float(min(w, pos0 + t + 1))
            mixes.append(win_sum / cnt - e_ref[POOL_STATE + t, :, sl])
        mix = jnp.concatenate(mixes, axis=0)
        o = _dot(mix.astype(BF16), w_ref[gi]) * ls_ref[:, sl]
        for t in range(n_t):
            h_ref[t, :, sl] = x_ref[t, :, sl] + o[t * bb:(t + 1) * bb]


def _pool_step(x_t, st_t, g, w, ls, pos0, bb=32):
    n_t, b, _ = x_t.shape
    return pl.pallas_call(
        functools.partial(_pool_step_kernel, pos0=pos0),
        grid=(b // bb,),
        in_specs=[
            pl.BlockSpec((n_t, bb, D_MODEL), lambda i: (0, i, 0)),
            pl.BlockSpec((POOL_STATE, bb, D_MODEL), lambda i: (0, i, 0)),
            pl.BlockSpec((1, D_MODEL), lambda i: (0, 0)),
            pl.BlockSpec((len(POOL_WINDOWS), POOL_GROUP_DIM, POOL_GROUP_DIM), lambda i: (0, 0, 0)),
            pl.BlockSpec((1, D_MODEL), lambda i: (0, 0)),
        ],
        out_specs=[
            pl.BlockSpec((n_t, bb, D_MODEL), lambda i: (0, i, 0)),
            pl.BlockSpec((POOL_STATE, bb, D_MODEL), lambda i: (0, i, 0)),
        ],
        out_shape=[
            jax.ShapeDtypeStruct((n_t, b, D_MODEL), F32),
            jax.ShapeDtypeStruct((POOL_STATE, b, D_MODEL), F32),
        ],
        scratch_shapes=[pltpu.VMEM((POOL_STATE + n_t, bb, D_MODEL), F32)],
        compiler_params=_params(1),
        name="pool_step",
    )(x_t, st_t, g, w, ls)


def _ffn_kernel(x_ref, g_ref, w1_ref, w3_ref, w2_ref, o_ref, u_ref):
    k = pl.program_id(1)

    @pl.when(k == 0)
    def _():
        u_ref[...] = _rmsnorm(x_ref[...], g_ref[...]).astype(BF16)

    u = u_ref[...]
    a = _dot(u, w1_ref[...])
    b = _dot(u, w3_ref[...])
    part = _dot((jax.nn.silu(a) * b).astype(BF16), w2_ref[...])

    @pl.when(k == 0)
    def _():
        o_ref[...] = x_ref[...] + part

    @pl.when(k > 0)
    def _():
        o_ref[...] += part


def _ffn(x, g, w1, w3, w2, tm):
    m = x.shape[0]
    f = w1.shape[1]
    tf = FFN_FF_TILE
    return pl.pallas_call(
        _ffn_kernel,
        grid=(m // tm, f // tf),
        in_specs=[
            pl.BlockSpec((tm, D_MODEL), lambda i, k: (i, 0)),
            pl.BlockSpec((1, D_MODEL), lambda i, k: (0, 0)),
            pl.BlockSpec((D_MODEL, tf), lambda i, k: (0, k)),
            pl.BlockSpec((D_MODEL, tf), lambda i, k: (0, k)),
            pl.BlockSpec((tf, D_MODEL), lambda i, k: (k, 0)),
        ],
        out_specs=pl.BlockSpec((tm, D_MODEL), lambda i, k: (i, 0)),
        out_shape=jax.ShapeDtypeStruct((m, D_MODEL), F32),
        scratch_shapes=[pltpu.VMEM((tm, D_MODEL), BF16)],
        compiler_params=_params(2),
        name="ffn0",
    )(x, g, w1, w3, w2)


def _qkv_kernel(x_ref, gkv_ref, wkv_ref, bkv_ref, gq_ref, wq_ref, bq_ref, kv_ref, q_ref):
    x = x_ref[...]
    xn = x * lax.rsqrt(jnp.mean(x * x, axis=-1, keepdims=True) + EPS)
    kv_ref[...] = _dot((xn * gkv_ref[...]).astype(BF16), wkv_ref[...]) + bkv_ref[...]
    q = _dot((xn * gq_ref[...]).astype(BF16), wq_ref[...]) + bq_ref[...]
    q_ref[...] = (q * HEAD_DIM ** -0.5).astype(BF16)


def _qkv(x, gkv, wkv, bkv, gq, wq, bq, tm):
    m = x.shape[0]
    row = lambda i: (i, 0)
    fixed = lambda i: (0, 0)
    return pl.pallas_call(
        _qkv_kernel,
        grid=(m // tm,),
        in_specs=[
            pl.BlockSpec((tm, D_MODEL), row),
            pl.BlockSpec((1, D_MODEL), fixed),
            pl.BlockSpec((D_MODEL, 2 * HKV), fixed),
            pl.BlockSpec((1, 2 * HKV), fixed),
            pl.BlockSpec((1, D_MODEL), fixed),
            pl.BlockSpec((D_MODEL, D_MODEL), fixed),
            pl.BlockSpec((1, D_MODEL), fixed),
        ],
        out_specs=[pl.BlockSpec((tm, 2 * HKV), row), pl.BlockSpec((tm, D_MODEL), row)],
        out_shape=[jax.ShapeDtypeStruct((m, 2 * HKV), F32), jax.ShapeDtypeStruct((m, D_MODEL), BF16)],
        compiler_params=_params(1),
        name="qkv",
    )(x, gkv, wkv, bkv, gq, wq, bq)


def _attn_seq_kernel(sinks_ref, q_ref, cur_ref, prev_ref, past_ref, o_ref, *, pos0):
    j = pl.program_id(1)
    tq = ATTN_TILE
    prev = jnp.where(j == 0, past_ref[...], prev_ref[...])
    band = jnp.concatenate([prev, cur_ref[...]], axis=0)
    kb = band[:, :HKV].astype(BF16)
    vb = band[:, HKV:].astype(BF16)
    r = lax.broadcasted_iota(jnp.int32, (tq, 2 * tq), 0)
    c = lax.broadcasted_iota(jnp.int32, (tq, 2 * tq), 1)
    key_pos = pos0 + (j - 1) * tq + c
    valid = (c > r) & (c <= r + tq) & (key_pos >= 0)
    q = q_ref[...]
    for kvh in range(N_KV_HEADS):
        k_g = kb[:, kvh * HEAD_DIM:(kvh + 1) * HEAD_DIM]
        v_g = vb[:, kvh * HEAD_DIM:(kvh + 1) * HEAD_DIM]
        for hg in range(GROUP):
            h = kvh * GROUP + hg
            hs = slice(h * HEAD_DIM, (h + 1) * HEAD_DIM)
            s = jnp.where(valid, _dot_nt(q[:, hs], k_g), MASKED)
            sink = sinks_ref[h]
            m = jnp.maximum(jnp.max(s, axis=-1, keepdims=True), sink)
            p = jnp.exp(s - m)
            denom = jnp.sum(p, axis=-1, keepdims=True) + jnp.exp(sink - m)
            o_ref[:, hs] = (_dot(p.astype(BF16), v_g) / denom).astype(BF16)


def _attn_seq(q, kv, past_kv, sinks, n_seq, pos0):
    m = q.shape[0]
    tq = ATTN_TILE
    nb = m // n_seq // tq
    return pl.pallas_call(
        functools.partial(_attn_seq_kernel, pos0=pos0),
        grid=(n_seq, nb),
        in_specs=[
            pl.BlockSpec(memory_space=pltpu.SMEM),
            pl.BlockSpec((tq, D_MODEL), lambda b, j: (b * nb + j, 0)),
            pl.BlockSpec((tq, 2 * HKV), lambda b, j: (b * nb + j, 0)),
            pl.BlockSpec((tq, 2 * HKV), lambda b, j: (b * nb + jnp.maximum(j - 1, 0), 0)),
            pl.BlockSpec((tq, 2 * HKV), lambda b, j: (0, 0)),
        ],
        out_specs=pl.BlockSpec((tq, D_MODEL), lambda b, j: (b * nb + j, 0)),
        out_shape=jax.ShapeDtypeStruct((m, D_MODEL), BF16),
        compiler_params=_params(2),
        name="attn_seq",
    )(sinks, q, kv, kv, past_kv)


def _attn_step_kernel(q_ref, new_ref, ck_ref, cv_ref, sink_ref, o_ref, nk_ref, nv_ref, *, pos0, n_t):
    bb = q_ref.shape[0]
    rows = N_KV_HEADS * n_t * GROUP
    pad = WINDOW - new_ref.shape[1]
    rho = lax.broadcasted_iota(jnp.int32, (rows, 2 * WINDOW), 0)
    c = lax.broadcasted_iota(jnp.int32, (rows, 2 * WINDOW), 1)
    t = (rho // GROUP) % n_t
    valid = (c > t) & (c <= t + WINDOW) & (pos0 - WINDOW + c >= 0)
    sink = sink_ref[...]
    for i in range(bb):
        new = jnp.concatenate([new_ref[i], jnp.zeros((pad, 2 * HKV), F32)], axis=0)
        kcat = jnp.concatenate([ck_ref[i], new[:, :HKV]], axis=0).astype(BF16)
        vcat = jnp.concatenate([cv_ref[i], new[:, HKV:]], axis=0).astype(BF16)
        s = jnp.where(valid, _dot_nt(q_ref[i], kcat), MASKED)
        m = jnp.maximum(jnp.max(s, axis=-1, keepdims=True), sink)
        p = jnp.exp(s - m)
        denom = jnp.sum(p, axis=-1, keepdims=True) + jnp.exp(sink - m)
        o = (_dot(p.astype(BF16), vcat) / denom).astype(BF16)
        for kvh in range(N_KV_HEADS):
            rs = n_t * GROUP
            o_ref[i, kvh] = o[kvh * rs:(kvh + 1) * rs, kvh * HEAD_DIM:(kvh + 1) * HEAD_DIM]
        nk_ref[i, 0:WINDOW - n_t, :] = ck_ref[i, n_t:WINDOW, :]
        nk_ref[i, WINDOW - n_t:WINDOW, :] = new_ref[i, 0:n_t, 0:HKV]
        nv_ref[i, 0:WINDOW - n_t, :] = cv_ref[i, n_t:WINDOW, :]
        nv_ref[i, WINDOW - n_t:WINDOW, :] = new_ref[i, 0:n_t, HKV:2 * HKV]


def _attn_step(q_bd, new_kv, cache_k, cache_v, sink_rows, pos0, n_t, bb=8):
    b = q_bd.shape[0]
    rows = q_bd.shape[1]
    n_new = new_kv.shape[1]
    blk = lambda *s: pl.BlockSpec((bb,) + s, lambda i: (i,) + (0,) * len(s))
    return pl.pallas_call(
        functools.partial(_attn_step_kernel, pos0=pos0, n_t=n_t),
        grid=(b // bb,),
        in_specs=[
            blk(rows, HKV),
            blk(n_new, 2 * HKV),
            blk(WINDOW, HKV),
            blk(WINDOW, HKV),
            pl.BlockSpec((rows, 1), lambda i: (0, 0)),
        ],
        out_specs=[blk(N_KV_HEADS, n_t * GROUP, HEAD_DIM), blk(WINDOW, HKV), blk(WINDOW, HKV)],
        out_shape=[
            jax.ShapeDtypeStruct((b, N_KV_HEADS, n_t * GROUP, HEAD_DIM), BF16),
            jax.ShapeDtypeStruct((b, WINDOW, HKV), F32),
            jax.ShapeDtypeStruct((b, WINDOW, HKV), F32),
        ],
        compiler_params=_params(1),
        name="attn_step",
    )(q_bd, new_kv, cache_k, cache_v, sink_rows)


def _oproj_kernel(o_ref, h_ref, wo_ref, bo_ref, g_ref, wr_ref, h3_ref, u_ref, comb_ref, sel_ref):
    h3 = h_ref[...] + (_dot(o_ref[...], wo_ref[...]) + bo_ref[...])
    h3_ref[...] = h3
    u = _rmsnorm(h3, g_ref[...])
    u_ref[...] = u
    logits = _dot(u.astype(BF16), wr_ref[...])
    lane = lax.broadcasted_iota(jnp.int32, logits.shape, 1)
    lg = jnp.where(lane < N_EXPERTS, logits, -jnp.inf)
    m1 = jnp.max(lg, axis=-1, keepdims=True)
    i1 = jnp.min(jnp.where(lg == m1, lane, LANES), axis=-1, keepdims=True)
    top1 = lane == i1
    lg2 = jnp.where(top1, -jnp.inf, lg)
    m2 = jnp.max(lg2, axis=-1, keepdims=True)
    i2 = jnp.min(jnp.where(lg2 == m2, lane, LANES), axis=-1, keepdims=True)
    top2 = lane == i2
    e2 = jnp.exp(m2 - m1)
    denom = 1.0 + e2
    comb = jnp.where(top1, 1.0 / denom, 0.0) + jnp.where(top2, e2 / denom, 0.0)
    comb_ref[...] = comb[:, :N_EXPERTS]
    sel_ref[...] = jnp.where(top1 | top2, 1, 0).astype(jnp.int32)[:, :N_EXPERTS]


def _oproj(o, h, wo, bo, g, wr, tm):
    m = o.shape[0]
    row = lambda i: (i, 0)
    fixed = lambda i: (0, 0)
    return pl.pallas_call(
        _oproj_kernel,
        grid=(m // tm,),
        in_specs=[
            pl.BlockSpec((tm, D_MODEL), row),
            pl.BlockSpec((tm, D_MODEL), row),
            pl.BlockSpec((D_MODEL, D_MODEL), fixed),
            pl.BlockSpec((1, D_MODEL), fixed),
            pl.BlockSpec((1, D_MODEL), fixed),
            pl.BlockSpec((D_MODEL, LANES), fixed),
        ],
        out_specs=[
            pl.BlockSpec((tm, D_MODEL), row),
            pl.BlockSpec((tm, D_MODEL), row),
            pl.BlockSpec((tm, N_EXPERTS), row),
            pl.BlockSpec((tm, N_EXPERTS), row),
        ],
        out_shape=[
            jax.ShapeDtypeStruct((m, D_MODEL), F32),
            jax.ShapeDtypeStruct((m, D_MODEL), F32),
            jax.ShapeDtypeStruct((m, N_EXPERTS), F32),
            jax.ShapeDtypeStruct((m, N_EXPERTS), jnp.int32),
        ],
        compiler_params=_params(1),
        name="oproj_router",
    )(o, h, wo, bo, g, wr)


def _row_copy(src, r, dst, p, sem):
    return pltpu.make_async_copy(src.at[pl.ds(r, 1)], dst.at[pl.ds(p, 1)], sem)


def _zero_fill_copies(zeros_ref, xs_ref, last_tile_ref, has_rows_ref, nvalid_ref, sem, n_tiles, min_tiles):
    tm = MOE_TILE
    pairs = []
    for e in range(N_EXPERTS):
        start = pl.multiple_of(last_tile_ref[e], tm)
        cp = pltpu.make_async_copy(zeros_ref, xs_ref.at[pl.ds(start, tm)], sem)
        pairs.append((has_rows_ref[e] != 0, cp))
    for tile in range(min_tiles, n_tiles):
        cp = pltpu.make_async_copy(zeros_ref, xs_ref.at[pl.ds(tile * tm, tm)], sem)
        pairs.append((tile >= nvalid_ref[0], cp))
    return pairs


def _dispatch_kernel(pos_a_ref, pos_b_ref, last_tile_ref, has_rows_ref, nvalid_ref,
                     up_ref, us_ref, xs_ref, zeros_ref, sem, zsem, *, n_blocks_p, n_tiles, min_tiles):
    i = pl.program_id(0)
    rt = ROW_DMA_TILE

    @pl.when(i == 0)
    def _():
        zeros_ref[...] = jnp.zeros(zeros_ref.shape, F32)
        pairs = _zero_fill_copies(zeros_ref, xs_ref, last_tile_ref, has_rows_ref, nvalid_ref,
                                  zsem, n_tiles, min_tiles)
        for cond, cp in pairs:
            @pl.when(cond)
            def _():
                cp.start()
        for cond, cp in pairs:
            @pl.when(cond)
            def _():
                cp.wait()

    def scatter(src_ref):
        def body(r, carry):
            tok = i * rt + r
            _row_copy(src_ref, r, xs_ref, pos_a_ref[tok], sem).start()
            _row_copy(src_ref, r, xs_ref, pos_b_ref[tok], sem).start()
            return carry
        lax.fori_loop(0, rt, body, 0)
        for _ in range(2):
            pltpu.make_async_copy(src_ref, xs_ref.at[pl.ds(0, rt)], sem).wait()

    @pl.when(i < n_blocks_p)
    def _():
        scatter(up_ref)

    @pl.when(i >= n_blocks_p)
    def _():
        scatter(us_ref)


def _dispatch(pos_a, pos_b, last_tile, has_rows, nvalid, u_p, u_s, n_tiles, min_tiles):
    rt = ROW_DMA_TILE
    nbp = u_p.shape[0] // rt
    nbs = u_s.shape[0] // rt
    grid_spec = pltpu.PrefetchScalarGridSpec(
        num_scalar_prefetch=5,
        grid=(nbp + nbs,),
        in_specs=[
            pl.BlockSpec((rt, D_MODEL), lambda i, *_: (jnp.minimum(i, nbp - 1), 0)),
            pl.BlockSpec((rt, D_MODEL), lambda i, *_: (jnp.maximum(i - nbp, 0), 0)),
        ],
        out_specs=pl.BlockSpec(memory_space=pl.ANY),
        scratch_shapes=[
            pltpu.VMEM((MOE_TILE, D_MODEL), F32),
            pltpu.SemaphoreType.DMA,
            pltpu.SemaphoreType.DMA,
        ],
    )
    return pl.pallas_call(
        functools.partial(_dispatch_kernel, n_blocks_p=nbp, n_tiles=n_tiles, min_tiles=min_tiles),
        grid_spec=grid_spec,
        out_shape=jax.ShapeDtypeStruct((n_tiles * MOE_TILE, D_MODEL), F32),
        compiler_params=_params(1),
        name="moe_dispatch",
    )(pos_a, pos_b, last_tile, has_rows, nvalid, u_p, u_s)


def _moe_kernel(te_ref, nv_ref, x_ref, w1_ref, w3_ref, w2_ref, y_ref, xb_ref):
    i = pl.program_id(0)
    k = pl.program_id(1)

    @pl.when(i < nv_ref[0])
    def _():
        @pl.when(k == 0)
        def _():
            xb_ref[...] = x_ref[...].astype(BF16)

        xb = xb_ref[...]
        a = _dot(xb, w1_ref[0])
        b = _dot(xb, w3_ref[0])
        part = _dot((jax.nn.silu(a) * b).astype(BF16), w2_ref[0])

        @pl.when(k == 0)
        def _():
            y_ref[...] = part

        @pl.when(k > 0)
        def _():
            y_ref[...] += part

    @pl.when((i >= nv_ref[0]) & (k == 0))
    def _():
        y_ref[...] = jnp.zeros(y_ref.shape, F32)


def _moe(tile_expert, nvalid, xs, w1, w3, w2):
    tm, tf = MOE_TILE, MOE_FF_TILE
    n_tiles = xs.shape[0] // tm
    kf = w1.shape[2] // tf

    def row(i, k, te, nv):
        return (jnp.minimum(i, nv[0] - 1), 0)

    def out_row(i, k, te, nv):
        return (i, 0)

    def up(i, k, te, nv):
        return (te[jnp.minimum(i, nv[0] - 1)], 0, jnp.where(i < nv[0], k, kf - 1))

    def down(i, k, te, nv):
        return (te[jnp.minimum(i, nv[0] - 1)], jnp.where(i < nv[0], k, kf - 1), 0)

    grid_spec = pltpu.PrefetchScalarGridSpec(
        num_scalar_prefetch=2,
        grid=(n_tiles, kf),
        in_specs=[
            pl.BlockSpec((tm, D_MODEL), row),
            pl.BlockSpec((1, D_MODEL, tf), up),
            pl.BlockSpec((1, D_MODEL, tf), up),
            pl.BlockSpec((1, tf, D_MODEL), down),
        ],
        out_specs=pl.BlockSpec((tm, D_MODEL), out_row),
        scratch_shapes=[pltpu.VMEM((tm, D_MODEL), BF16)],
    )
    return pl.pallas_call(
        _moe_kernel,
        grid_spec=grid_spec,
        out_shape=jax.ShapeDtypeStruct(xs.shape, F32),
        compiler_params=_params(2),
        name="moe_ffn",
    )(tile_expert, nvalid, xs, w1, w3, w2)


def _combine_kernel(pos_a_ref, pos_b_ref, h_ref, ga_ref, gb_ref, g_ref, y_ref, o_ref, ya_ref, yb_ref, sem, *, base):
    i = pl.program_id(0)
    rt = ROW_DMA_TILE

    def body(r, carry):
        tok = base + i * rt + r
        _row_copy(y_ref, pos_a_ref[tok], ya_ref, r, sem).start()
        _row_copy(y_ref, pos_b_ref[tok], yb_ref, r, sem).start()
        return carry

    lax.fori_loop(0, rt, body, 0)
    pltpu.make_async_copy(y_ref.at[pl.ds(0, rt)], ya_ref, sem).wait()
    pltpu.make_async_copy(y_ref.at[pl.ds(0, rt)], yb_ref, sem).wait()
    moe = ga_ref[...] * ya_ref[...] + gb_ref[...] * yb_ref[...]
    o_ref[...] = _rmsnorm(h_ref[...] + moe, g_ref[...])


def _combine(pos_a, pos_b, h, gate_a, gate_b, g, y, base):
    rt = ROW_DMA_TILE
    m = h.shape[0]
    row = lambda i, *_: (i, 0)
    grid_spec = pltpu.PrefetchScalarGridSpec(
        num_scalar_prefetch=2,
        grid=(m // rt,),
        in_specs=[
            pl.BlockSpec((rt, D_MODEL), row),
            pl.BlockSpec((rt, 1), row),
            pl.BlockSpec((rt, 1), row),
            pl.BlockSpec((1, D_MODEL), lambda i, *_: (0, 0)),
            pl.BlockSpec(memory_space=pl.ANY),
        ],
        out_specs=pl.BlockSpec((rt, D_MODEL), row),
        scratch_shapes=[
            pltpu.VMEM((rt, D_MODEL), F32),
            pltpu.VMEM((rt, D_MODEL), F32),
            pltpu.SemaphoreType.DMA,
        ],
    )
    return pl.pallas_call(
        functools.partial(_combine_kernel, base=base),
        grid_spec=grid_spec,
        out_shape=jax.ShapeDtypeStruct((m, D_MODEL), F32),
        compiler_params=_params(1),
        name="moe_combine",
    )(pos_a, pos_b, h, gate_a, gate_b, g, y)


def _route_tables(sel, comb, n_tiles):
    tm = MOE_TILE
    cum = jnp.cumsum(sel, axis=0)
    counts = cum[-1]
    padded = ((counts + tm - 1) // tm) * tm
    ends = jnp.cumsum(padded)
    starts = ends - padded
    slot = starts[None, :] + cum - sel
    order = jnp.cumsum(sel, axis=1)
    first = (sel == 1) & (order == 1)
    second = (sel == 1) & (order == 2)
    pos_a = jnp.sum(jnp.where(first, slot, 0), axis=1).astype(jnp.int32)
    pos_b = jnp.sum(jnp.where(second, slot, 0), axis=1).astype(jnp.int32)
    gate_a = jnp.sum(jnp.where(first, comb, 0.0), axis=1, keepdims=True)
    gate_b = jnp.sum(jnp.where(second, comb, 0.0), axis=1, keepdims=True)
    tile_start = jnp.arange(n_tiles, dtype=jnp.int32) * tm
    tile_expert = jnp.minimum(jnp.sum(tile_start[:, None] >= ends[None, :], axis=1), N_EXPERTS - 1)
    nvalid = (ends[-1:] // tm).astype(jnp.int32)
    last_tile = jnp.maximum(ends - tm, 0).astype(jnp.int32)
    has_rows = (counts > 0).astype(jnp.int32)
    return pos_a, pos_b, gate_a, gate_b, tile_expert.astype(jnp.int32), nvalid, last_tile, has_rows


def kernel(x_prompt, x_sample, state_pool, cache_k, cache_v, meta_tokens, g_pool, w_pool, ls_pool, g_ffn0, w_ff1, w_ff3, w_ff2, g_kv, w_kv, b_kv, g_attn, w_q, b_q, sinks, w_o, b_o, g_ffn1, w_router, w_e1, w_e3, w_e2, g_final):
    n_seq, seq, _ = x_prompt.shape
    n_dec, n_t, _ = x_sample.shape
    past_len = PAST_LEN
    window = cache_k.shape[1]
    assert window == WINDOW and seq % POOL_TILE == 0 and N_META <= POOL_TILE

    vec = lambda a: a.reshape(1, -1).astype(F32)
    g_pool, ls_pool, g_ffn0, g_kv, b_kv, g_attn, b_q, b_o, g_ffn1, g_final = map(
        vec, (g_pool, ls_pool, g_ffn0, g_kv, b_kv, g_attn, b_q, b_o, g_ffn1, g_final))
    w_pool, w_ff1, w_ff3, w_ff2, w_kv, w_q, w_o, w_e1, w_e3, w_e2 = (
        w.astype(BF16) for w in (w_pool, w_ff1, w_ff3, w_ff2, w_kv, w_q, w_o, w_e1, w_e3, w_e2))
    w_router = jnp.pad(w_router, ((0, 0), (0, LANES - N_EXPERTS))).astype(BF16)
    meta = meta_tokens.astype(F32)

    def layer0_tail(h1, tm_ffn, tm_proj):
        h2 = _ffn(h1, g_ffn0, w_ff1, w_ff3, w_ff2, tm_ffn)
        kv, q = _qkv(h2, g_kv, w_kv, b_kv, g_attn, w_q, b_q, tm_proj)
        return h2, kv, q

    x_meta = jnp.pad(meta, ((0, POOL_TILE - N_META), (0, 0)))[None]
    h1_m, _ = _pool_seq(x_meta, jnp.zeros_like(meta), g_pool, w_pool, ls_pool, 0)
    _, kv_m, _ = layer0_tail(h1_m[0], POOL_TILE, POOL_TILE)
    past_kv = jnp.pad(kv_m[:N_META], ((WINDOW - N_META, 0), (0, 0)))

    h1_p, pool_p = _pool_seq(x_prompt, meta, g_pool, w_pool, ls_pool, N_META)
    h2_p, kv_p, q_p = layer0_tail(h1_p.reshape(n_seq * seq, D_MODEL), FFN_TILE, PROJ_TILE)
    o_p = _attn_seq(q_p, kv_p, past_kv, sinks.astype(F32), n_seq, N_META)
    h3_p, u_p, comb_p, sel_p = _oproj(o_p, h2_p, w_o, b_o, g_ffn1, w_router, PROJ_TILE)
    kv_tail = kv_p.reshape(n_seq, seq, 2 * HKV)[:, seq - WINDOW:]
    k_p = kv_tail[..., :HKV].reshape(n_seq, WINDOW, N_KV_HEADS, HEAD_DIM)
    v_p = kv_tail[..., HKV:].reshape(n_seq, WINDOW, N_KV_HEADS, HEAD_DIM)

    h1_s, pool_s = _pool_step(x_sample.transpose(1, 0, 2), state_pool.transpose(1, 0, 2),
                              g_pool, w_pool, ls_pool, past_len)
    n_s = n_t * n_dec
    h2_s, kv_s, q_s = layer0_tail(h1_s.reshape(n_s, D_MODEL), FFN_TILE, PROJ_TILE)
    q5 = q_s.reshape(n_t, n_dec, N_KV_HEADS, GROUP, HEAD_DIM).transpose(1, 2, 0, 3, 4)
    eye = jnp.eye(N_KV_HEADS, dtype=BF16)
    q_bd = (q5[:, :, :, :, None, :] * eye[None, :, None, None, :, None]).reshape(
        n_dec, N_KV_HEADS * n_t * GROUP, HKV)
    new_kv = jnp.pad(kv_s.reshape(n_t, n_dec, 2 * HKV).transpose(1, 0, 2), ((0, 0), (0, 16 - n_t), (0, 0)))
    sink_rows = jnp.broadcast_to(sinks.astype(F32).reshape(N_KV_HEADS, 1, GROUP),
                                 (N_KV_HEADS, n_t, GROUP)).reshape(-1, 1)
    o_s4, k_s, v_s = _attn_step(q_bd, new_kv, cache_k.reshape(n_dec, WINDOW, HKV),
                                cache_v.reshape(n_dec, WINDOW, HKV), sink_rows, past_len, n_t)
    o_s = o_s4.reshape(n_dec, N_KV_HEADS, n_t, GROUP, HEAD_DIM).transpose(2, 0, 1, 3, 4).reshape(n_s, D_MODEL)
    h3_s, u_s, comb_s, sel_s = _oproj(o_s, h2_s, w_o, b_o, g_ffn1, w_router, PROJ_TILE)

    n_tok = n_seq * seq + n_s
    min_tiles = 2 * n_tok // MOE_TILE
    n_tiles = min_tiles + N_EXPERTS
    pos_a, pos_b, gate_a, gate_b, tile_expert, nvalid, last_tile, has_rows = _route_tables(
        jnp.concatenate([sel_p, sel_s]), jnp.concatenate([comb_p, comb_s]), n_tiles)
    xs = _dispatch(pos_a, pos_b, last_tile, has_rows, nvalid, u_p, u_s, n_tiles, min_tiles)
    y = _moe(tile_expert, nvalid, xs, w_e1, w_e3, w_e2)
    n_p = n_seq * seq
    y_p = _combine(pos_a, pos_b, h3_p, gate_a[:n_p], gate_b[:n_p], g_final, y, 0)
    y_s = _combine(pos_a, pos_b, h3_s, gate_a[n_p:], gate_b[n_p:], g_final, y, n_p)

    return (y_p.reshape(n_seq, seq, D_MODEL),
            y_s.reshape(n_t, n_dec, D_MODEL).transpose(1, 0, 2),
            pool_p,
            pool_s.transpose(1, 0, 2),
            k_p, v_p,
            k_s.reshape(n_dec, WINDOW, N_KV_HEADS, HEAD_DIM),
            v_s.reshape(n_dec, WINDOW, N_KV_HEADS, HEAD_DIM))
```

```python
import functools

import jax
import jax.numpy as jnp
from jax import lax
from jax.experimental import pallas as pl
from jax.experimental.pallas import tpu as pltpu

F32 = jnp.float32
BF16 = jnp.bfloat16

D_MODEL = 2048
N_META = 16
POOL_WINDOWS = (2, 4, 8, 16)
POOL_GROUP_DIM = D_MODEL // len(POOL_WINDOWS)
POOL_STATE = max(POOL_WINDOWS) - 1
HEAD_DIM = 64
N_HEADS = D_MODEL // HEAD_DIM
N_KV_HEADS = 4
GROUP = N_HEADS // N_KV_HEADS
HKV = N_KV_HEADS * HEAD_DIM
WINDOW = 128
N_EXPERTS = 8
EPS = 1e-5
PAST_LEN = 8192
MASKED = -1e30

LANES = 128
V7X_VMEM_BYTES = 64 * 2 ** 20
VMEM_LIMIT = V7X_VMEM_BYTES - 8 * 2 ** 20

POOL_TILE = 128
ATTN_TILE = WINDOW
MOE_TILE = 512
MOE_FF_TILE = 1024
FFN_FF_TILE = 512
FFN_TILE = 512
PROJ_TILE = 256
ROW_DMA_TILE = 256


def _dot(a, b):
    return jnp.dot(a, b, preferred_element_type=F32)


def _dot_nt(a, b):
    return lax.dot_general(a, b, (((1,), (1,)), ((), ())), preferred_element_type=F32)


def _rmsnorm(x, g):
    return x * lax.rsqrt(jnp.mean(x * x, axis=-1, keepdims=True) + EPS) * g


def _params(n_axes):
    return pltpu.CompilerParams(dimension_semantics=("arbitrary",) * n_axes,
                                vmem_limit_bytes=VMEM_LIMIT)


def _pool_seq_kernel(x_ref, meta_ref, g_ref, w_ref, ls_ref, h_ref, st_ref, e_ref, *, pos0):
    t = pl.program_id(1)
    tt = POOL_TILE
    g = g_ref[...]

    @pl.when(t == 0)
    def _():
        e_ref[0:tt - N_META, :] = jnp.zeros((tt - N_META, D_MODEL), F32)
        e_ref[tt - N_META:tt, :] = _rmsnorm(meta_ref[...], g)

    @pl.when(t > 0)
    def _():
        e_ref[0:tt, :] = e_ref[tt:2 * tt, :]

    x = x_ref[0]
    u = _rmsnorm(x, g)
    e_ref[tt:2 * tt, :] = u

    e = e_ref[...]
    hi = e.astype(BF16)
    lo = (e - hi.astype(F32)).astype(BF16)
    r = lax.broadcasted_iota(jnp.int32, (tt, 2 * tt), 0)
    c = lax.broadcasted_iota(jnp.int32, (tt, 2 * tt), 1)
    pos = pos0 + t * tt + lax.broadcasted_iota(jnp.int32, (tt, 1), 0)
    for gi, w in enumerate(POOL_WINDOWS):
        sl = slice(gi * POOL_GROUP_DIM, (gi + 1) * POOL_GROUP_DIM)
        band = jnp.where((c > r + tt - w) & (c <= r + tt), 1.0, 0.0).astype(BF16)
        win_sum = _dot(band, hi[:, sl]) + _dot(band, lo[:, sl])
        cnt = jnp.minimum(w, pos + 1).astype(F32)
        mix = win_sum / cnt - u[:, sl]
        o = _dot(mix.astype(BF16), w_ref[gi]) * ls_ref[:, sl]
        h_ref[0, :, sl] = x[:, sl] + o

    @pl.when(t == pl.num_programs(1) - 1)
    def _():
        st_ref[0] = e_ref[2 * tt - POOL_STATE:2 * tt, :]


def _pool_seq(x, meta, g, w, ls, pos0):
    b, t, _ = x.shape
    tt = POOL_TILE
    return pl.pallas_call(
        functools.partial(_pool_seq_kernel, pos0=pos0),
        grid=(b, t // tt),
        in_specs=[
            pl.BlockSpec((1, tt, D_MODEL), lambda i, j: (i, j, 0)),
            pl.BlockSpec((N_META, D_MODEL), lambda i, j: (0, 0)),
            pl.BlockSpec((1, D_MODEL), lambda i, j: (0, 0)),
            pl.BlockSpec((len(POOL_WINDOWS), POOL_GROUP_DIM, POOL_GROUP_DIM), lambda i, j: (0, 0, 0)),
            pl.BlockSpec((1, D_MODEL), lambda i, j: (0, 0)),
        ],
        out_specs=[
            pl.BlockSpec((1, tt, D_MODEL), lambda i, j: (i, j, 0)),
            pl.BlockSpec((1, POOL_STATE, D_MODEL), lambda i, j: (i, 0, 0)),
        ],
        out_shape=[
            jax.ShapeDtypeStruct((b, t, D_MODEL), F32),
            jax.ShapeDtypeStruct((b, POOL_STATE, D_MODEL), F32),
        ],
        scratch_shapes=[pltpu.VMEM((2 * tt, D_MODEL), F32)],
        compiler_params=_params(2),
        name="pool_seq",
    )(x, meta, g, w, ls)


def _pool_step_kernel(x_ref, st_ref, g_ref, w_ref, ls_ref, h_ref, nst_ref, e_ref, *, pos0):
    n_t, bb, _ = x_ref.shape
    g = g_ref[...]
    for j in range(POOL_STATE):
        e_ref[j] = st_ref[j]
    for t in range(n_t):
        e_ref[POOL_STATE + t] = _rmsnorm(x_ref[t], g)
    for j in range(POOL_STATE):
        nst_ref[j] = e_ref[j + n_t]
    for gi, w in enumerate(POOL_WINDOWS):
        sl = slice(gi * POOL_GROUP_DIM, (gi + 1) * POOL_GROUP_DIM)
        mixes = []
        for t in range(n_t):
            win_sum = e_ref[POOL_STATE + t, :, sl]
            for i in range(1, w):
                win_sum = win_sum + e_ref[POOL_STATE + t - i, :, sl]
            cnt = float(min(w, pos0 + t + 1))
            mixes.append(win_sum / cnt - e_ref[POOL_STATE + t, :, sl])
        mix = jnp.concatenate(mixes, axis=0)
        o = _dot(mix.astype(BF16), w_ref[gi]) * ls_ref[:, sl]
        for t in range(n_t):
            h_ref[t, :, sl] = x_ref[t, :, sl] + o[t * bb:(t + 1) * bb]


def _pool_step(x_t, st_t, g, w, ls, pos0, bb=32):
    n_t, b, _ = x_t.shape
    return pl.pallas_call(
        functools.partial(_pool_step_kernel, pos0=pos0),
        grid=(b // bb,),
        in_specs=[
            pl.BlockSpec((n_t, bb, D_MODEL), lambda i: (0, i, 0)),
            pl.BlockSpec((POOL_STATE, bb, D_MODEL), lambda i: (0, i, 0)),
            pl.BlockSpec((1, D_MODEL), lambda i: (0, 0)),
            pl.BlockSpec((len(POOL_WINDOWS), POOL_GROUP_DIM, POOL_GROUP_DIM), lambda i: (0, 0, 0)),
            pl.BlockSpec((1, D_MODEL), lambda i: (0, 0)),
        ],
        out_specs=[
            pl.BlockSpec((n_t, bb, D_MODEL), lambda i: (0, i, 0)),
            pl.BlockSpec((POOL_STATE, bb, D_MODEL), lambda i: (0, i, 0)),
        ],
        out_shape=[
            jax.ShapeDtypeStruct((n_t, b, D_MODEL), F32),
            jax.ShapeDtypeStruct((POOL_STATE, b, D_MODEL), F32),
        ],
        scratch_shapes=[pltpu.VMEM((POOL_STATE + n_t, bb, D_MODEL), F32)],
        compiler_params=_params(1),
        name="pool_step",
    )(x_t, st_t, g, w, ls)


def _swiglu_skewed(k, n_chunks, u_ref, w1, w3, w2, h_ref, acc_ref, first):
    def up():
        u = u_ref[...]
        h_ref[...] = (jax.nn.silu(_dot(u, w1())) * _dot(u, w3())).astype(BF16)

    def down():
        acc_ref[...] += _dot(h_ref[...], w2())

    @pl.when(k == 0)
    def _():
        first()
        up()

    @pl.when((k > 0) & (k < n_chunks))
    def _():
        down()
        up()

    @pl.when(k == n_chunks)
    def _():
        down()


def _ffn_kernel(x_ref, g_ref, w1_ref, w3_ref, w2_ref, o_ref, u_ref, h_ref):
    def first():
        x = x_ref[...]
        u_ref[...] = _rmsnorm(x, g_ref[...]).astype(BF16)
        o_ref[...] = x

    _swiglu_skewed(pl.program_id(1), pl.num_programs(1) - 1, u_ref,
                   lambda: w1_ref[...], lambda: w3_ref[...], lambda: w2_ref[...], h_ref, o_ref, first)


def _ffn(x, g, w1, w3, w2, tm):
    m = x.shape[0]
    f = w1.shape[1]
    tf = FFN_FF_TILE
    kf = f // tf
    return pl.pallas_call(
        _ffn_kernel,
        grid=(m // tm, kf + 1),
        in_specs=[
            pl.BlockSpec((tm, D_MODEL), lambda i, k: (i, 0)),
            pl.BlockSpec((1, D_MODEL), lambda i, k: (0, 0)),
            pl.BlockSpec((D_MODEL, tf), lambda i, k: (0, jnp.minimum(k, kf - 1))),
            pl.BlockSpec((D_MODEL, tf), lambda i, k: (0, jnp.minimum(k, kf - 1))),
            pl.BlockSpec((tf, D_MODEL), lambda i, k: (jnp.maximum(k - 1, 0), 0)),
        ],
        out_specs=pl.BlockSpec((tm, D_MODEL), lambda i, k: (i, 0)),
        out_shape=jax.ShapeDtypeStruct((m, D_MODEL), F32),
        scratch_shapes=[pltpu.VMEM((tm, D_MODEL), BF16), pltpu.VMEM((tm, tf), BF16)],
        compiler_params=_params(2),
        name="ffn0",
    )(x, g, w1, w3, w2)


def _qkv_kernel(x_ref, gkv_ref, wkv_ref, bkv_ref, gq_ref, wq_ref, bq_ref, kv_ref, q_ref):
    x = x_ref[...]
    xn = x * lax.rsqrt(jnp.mean(x * x, axis=-1, keepdims=True) + EPS)
    kv_ref[...] = _dot((xn * gkv_ref[...]).astype(BF16), wkv_ref[...]) + bkv_ref[...]
    q = _dot((xn * gq_ref[...]).astype(BF16), wq_ref[...]) + bq_ref[...]
    q_ref[...] = (q * HEAD_DIM ** -0.5).astype(BF16)


def _qkv(x, gkv, wkv, bkv, gq, wq, bq, tm):
    m = x.shape[0]
    row = lambda i: (i, 0)
    fixed = lambda i: (0, 0)
    return pl.pallas_call(
        _qkv_kernel,
        grid=(m // tm,),
        in_specs=[
            pl.BlockSpec((tm, D_MODEL), row),
            pl.BlockSpec((1, D_MODEL), fixed),
            pl.BlockSpec((D_MODEL, 2 * HKV), fixed),
            pl.BlockSpec((1, 2 * HKV), fixed),
            pl.BlockSpec((1, D_MODEL), fixed),
            pl.BlockSpec((D_MODEL, D_MODEL), fixed),
            pl.BlockSpec((1, D_MODEL), fixed),
        ],
        out_specs=[pl.BlockSpec((tm, 2 * HKV), row), pl.BlockSpec((tm, D_MODEL), row)],
        out_shape=[jax.ShapeDtypeStruct((m, 2 * HKV), F32), jax.ShapeDtypeStruct((m, D_MODEL), BF16)],
        compiler_params=_params(1),
        name="qkv",
    )(x, gkv, wkv, bkv, gq, wq, bq)


def _attn_seq_kernel(sinks_ref, q_ref, cur_ref, prev_ref, past_ref, o_ref, *, pos0):
    j = pl.program_id(1)
    tq = ATTN_TILE
    prev = jnp.where(j == 0, past_ref[...], prev_ref[...])
    band = jnp.concatenate([prev, cur_ref[...]], axis=0)
    c = lax.broadcasted_iota(jnp.int32, (2 * tq, tq), 0)
    r = lax.broadcasted_iota(jnp.int32, (2 * tq, tq), 1)
    key_pos = pos0 + (j - 1) * tq + c
    valid = (c > r) & (c <= r + tq) & (key_pos >= 0)
    low = lax.broadcasted_iota(jnp.int32, (2 * tq, LANES), 1) < HEAD_DIM
    zeros = jnp.zeros((2 * tq, LANES), F32)
    k_bd, v_bd_t = [], []
    v_t = band[:, HKV:].T
    zv = jnp.zeros((HEAD_DIM, 2 * tq), F32)
    for kvh in range(N_KV_HEADS):
        tile = band[:, (kvh // 2) * LANES:(kvh // 2 + 1) * LANES]
        swapped = pltpu.roll(tile, HEAD_DIM, axis=1)
        in_low, in_high = (tile, swapped) if kvh % 2 == 0 else (swapped, tile)
        k_bd.append(jnp.concatenate([jnp.where(low, in_low, zeros), jnp.where(low, zeros, in_high)],
                                    axis=0).astype(BF16))
        vt = v_t[kvh * HEAD_DIM:(kvh + 1) * HEAD_DIM]
        v_bd_t.append(jnp.concatenate([jnp.concatenate([vt, zv], axis=1),
                                       jnp.concatenate([zv, vt], axis=1)], axis=0).astype(BF16))
    for pair in range(N_HEADS // 2):
        kvh = (2 * pair) // GROUP
        ps = slice(pair * LANES, (pair + 1) * LANES)
        s_t = _dot_nt(k_bd[kvh], q_ref[:, ps])
        probs, inv = [], []
        for half in range(2):
            s = jnp.where(valid, s_t[half * 2 * tq:(half + 1) * 2 * tq], MASKED)
            sink = sinks_ref[2 * pair + half]
            m = jnp.maximum(jnp.max(s, axis=0, keepdims=True), sink)
            p = jnp.exp(s - m)
            denom = jnp.sum(p, axis=0, keepdims=True) + jnp.exp(sink - m)
            probs.append(p.astype(BF16))
            inv.append(jnp.broadcast_to(1.0 / denom, (HEAD_DIM, tq)))
        o_t = _dot(v_bd_t[kvh], jnp.concatenate(probs, axis=0)) * jnp.concatenate(inv, axis=0)
        o_ref[:, ps] = o_t.T.astype(BF16)


def _attn_seq(q, kv, past_kv, sinks, n_seq, pos0):
    m = q.shape[0]
    tq = ATTN_TILE
    nb = m // n_seq // tq
    return pl.pallas_call(
        functools.partial(_attn_seq_kernel, pos0=pos0),
        grid=(n_seq, nb),
        in_specs=[
            pl.BlockSpec(memory_space=pltpu.SMEM),
            pl.BlockSpec((tq, D_MODEL), lambda b, j: (b * nb + j, 0)),
            pl.BlockSpec((tq, 2 * HKV), lambda b, j: (b * nb + j, 0)),
            pl.BlockSpec((tq, 2 * HKV), lambda b, j: (b * nb + jnp.maximum(j - 1, 0), 0)),
            pl.BlockSpec((tq, 2 * HKV), lambda b, j: (0, 0)),
        ],
        out_specs=pl.BlockSpec((tq, D_MODEL), lambda b, j: (b * nb + j, 0)),
        out_shape=jax.ShapeDtypeStruct((m, D_MODEL), BF16),
        compiler_params=_params(2),
        name="attn_seq",
    )(sinks, q, kv, kv, past_kv)


def _attn_step_kernel(q_ref, new_ref, ck_ref, cv_ref, sink_ref, o_ref, nk_ref, nv_ref, *, pos0, n_t):
    bb = q_ref.shape[0]
    rows = N_KV_HEADS * n_t * GROUP
    pad = WINDOW - new_ref.shape[1]
    rho = lax.broadcasted_iota(jnp.int32, (rows, 2 * WINDOW), 0)
    c = lax.broadcasted_iota(jnp.int32, (rows, 2 * WINDOW), 1)
    t = (rho // GROUP) % n_t
    valid = (c > t) & (c <= t + WINDOW) & (pos0 - WINDOW + c >= 0)
    sink = sink_ref[...]
    for i in range(bb):
        new = jnp.concatenate([new_ref[i], jnp.zeros((pad, 2 * HKV), F32)], axis=0)
        kcat = jnp.concatenate([ck_ref[i], new[:, :HKV]], axis=0).astype(BF16)
        vcat = jnp.concatenate([cv_ref[i], new[:, HKV:]], axis=0).astype(BF16)
        s = jnp.where(valid, _dot_nt(q_ref[i], kcat), MASKED)
        m = jnp.maximum(jnp.max(s, axis=-1, keepdims=True), sink)
        p = jnp.exp(s - m)
        denom = jnp.sum(p, axis=-1, keepdims=True) + jnp.exp(sink - m)
        o = (_dot(p.astype(BF16), vcat) / denom).astype(BF16)
        for kvh in range(N_KV_HEADS):
            rs = n_t * GROUP
            o_ref[i, kvh] = o[kvh * rs:(kvh + 1) * rs, kvh * HEAD_DIM:(kvh + 1) * HEAD_DIM]
        nk_ref[i, 0:WINDOW - n_t, :] = ck_ref[i, n_t:WINDOW, :]
        nk_ref[i, WINDOW - n_t:WINDOW, :] = new_ref[i, 0:n_t, 0:HKV]
        nv_ref[i, 0:WINDOW - n_t, :] = cv_ref[i, n_t:WINDOW, :]
        nv_ref[i, WINDOW - n_t:WINDOW, :] = new_ref[i, 0:n_t, HKV:2 * HKV]


def _attn_step(q_bd, new_kv, cache_k, cache_v, sink_rows, pos0, n_t, bb=8):
    b = q_bd.shape[0]
    rows = q_bd.shape[1]
    n_new = new_kv.shape[1]
    blk = lambda *s: pl.BlockSpec((bb,) + s, lambda i: (i,) + (0,) * len(s))
    return pl.pallas_call(
        functools.partial(_attn_step_kernel, pos0=pos0, n_t=n_t),
        grid=(b // bb,),
        in_specs=[
            blk(rows, HKV),
            blk(n_new, 2 * HKV),
            blk(WINDOW, HKV),
            blk(WINDOW, HKV),
            pl.BlockSpec((rows, 1), lambda i: (0, 0)),
        ],
        out_specs=[blk(N_KV_HEADS, n_t * GROUP, HEAD_DIM), blk(WINDOW, HKV), blk(WINDOW, HKV)],
        out_shape=[
            jax.ShapeDtypeStruct((b, N_KV_HEADS, n_t * GROUP, HEAD_DIM), BF16),
            jax.ShapeDtypeStruct((b, WINDOW, HKV), F32),
            jax.ShapeDtypeStruct((b, WINDOW, HKV), F32),
        ],
        compiler_params=_params(1),
        name="attn_step",
    )(q_bd, new_kv, cache_k, cache_v, sink_rows)


def _oproj_kernel(o_ref, h_ref, wo_ref, bo_ref, g_ref, wr_ref, h3_ref, u_ref, comb_ref, sel_ref):
    h3 = h_ref[...] + (_dot(o_ref[...], wo_ref[...]) + bo_ref[...])
    h3_ref[...] = h3
    u = _rmsnorm(h3, g_ref[...])
    u_ref[...] = u
    logits = _dot(u.astype(BF16), wr_ref[...])
    lane = lax.broadcasted_iota(jnp.int32, logits.shape, 1)
    lg = jnp.where(lane < N_EXPERTS, logits, -jnp.inf)
    m1 = jnp.max(lg, axis=-1, keepdims=True)
    i1 = jnp.min(jnp.where(lg == m1, lane, LANES), axis=-1, keepdims=True)
    top1 = lane == i1
    lg2 = jnp.where(top1, -jnp.inf, lg)
    m2 = jnp.max(lg2, axis=-1, keepdims=True)
    i2 = jnp.min(jnp.where(lg2 == m2, lane, LANES), axis=-1, keepdims=True)
    top2 = lane == i2
    e2 = jnp.exp(m2 - m1)
    denom = 1.0 + e2
    comb = jnp.where(top1, 1.0 / denom, 0.0) + jnp.where(top2, e2 / denom, 0.0)
    comb_ref[...] = comb[:, :N_EXPERTS]
    sel_ref[...] = jnp.where(top1 | top2, 1, 0).astype(jnp.int32)[:, :N_EXPERTS]


def _oproj(o, h, wo, bo, g, wr, tm):
    m = o.shape[0]
    row = lambda i: (i, 0)
    fixed = lambda i: (0, 0)
    return pl.pallas_call(
        _oproj_kernel,
        grid=(m // tm,),
        in_specs=[
            pl.BlockSpec((tm, D_MODEL), row),
            pl.BlockSpec((tm, D_MODEL), row),
            pl.BlockSpec((D_MODEL, D_MODEL), fixed),
            pl.BlockSpec((1, D_MODEL), fixed),
            pl.BlockSpec((1, D_MODEL), fixed),
            pl.BlockSpec((D_MODEL, LANES), fixed),
        ],
        out_specs=[
            pl.BlockSpec((tm, D_MODEL), row),
            pl.BlockSpec((tm, D_MODEL), row),
            pl.BlockSpec((tm, N_EXPERTS), row),
            pl.BlockSpec((tm, N_EXPERTS), row),
        ],
        out_shape=[
            jax.ShapeDtypeStruct((m, D_MODEL), F32),
            jax.ShapeDtypeStruct((m, D_MODEL), F32),
            jax.ShapeDtypeStruct((m, N_EXPERTS), F32),
            jax.ShapeDtypeStruct((m, N_EXPERTS), jnp.int32),
        ],
        compiler_params=_params(1),
        name="oproj_router",
    )(o, h, wo, bo, g, wr)


def _row_copy(src, r, dst, p, sem):
    return pltpu.make_async_copy(src.at[pl.ds(r, 1)], dst.at[pl.ds(p, 1)], sem)


def _zero_fill_copies(zeros_ref, xs_ref, last_tile_ref, has_rows_ref, nvalid_ref, sem, n_tiles, min_tiles):
    tm = MOE_TILE
    pairs = []
    for e in range(N_EXPERTS):
        start = pl.multiple_of(last_tile_ref[e], tm)
        cp = pltpu.make_async_copy(zeros_ref, xs_ref.at[pl.ds(start, tm)], sem)
        pairs.append((has_rows_ref[e] != 0, cp))
    for tile in range(min_tiles, n_tiles):
        cp = pltpu.make_async_copy(zeros_ref, xs_ref.at[pl.ds(tile * tm, tm)], sem)
        pairs.append((tile >= nvalid_ref[0], cp))
    return pairs


def _dispatch_kernel(pos_a_ref, pos_b_ref, last_tile_ref, has_rows_ref, nvalid_ref,
                     up_ref, us_ref, xs_ref, zeros_ref, sem, zsem, *, n_blocks_p, n_tiles, min_tiles):
    i = pl.program_id(0)
    rt = ROW_DMA_TILE

    @pl.when(i == 0)
    def _():
        zeros_ref[...] = jnp.zeros(zeros_ref.shape, F32)
        pairs = _zero_fill_copies(zeros_ref, xs_ref, last_tile_ref, has_rows_ref, nvalid_ref,
                                  zsem, n_tiles, min_tiles)
        for cond, cp in pairs:
            @pl.when(cond)
            def _():
                cp.start()
        for cond, cp in pairs:
            @pl.when(cond)
            def _():
                cp.wait()

    def scatter(src_ref):
        def body(r, carry):
            tok = i * rt + r
            _row_copy(src_ref, r, xs_ref, pos_a_ref[tok], sem).start()
            _row_copy(src_ref, r, xs_ref, pos_b_ref[tok], sem).start()
            return carry
        lax.fori_loop(0, rt, body, 0)
        for _ in range(2):
            pltpu.make_async_copy(src_ref, xs_ref.at[pl.ds(0, rt)], sem).wait()

    @pl.when(i < n_blocks_p)
    def _():
        scatter(up_ref)

    @pl.when(i >= n_blocks_p)
    def _():
        scatter(us_ref)


def _dispatch(pos_a, pos_b, last_tile, has_rows, nvalid, u_p, u_s, n_tiles, min_tiles):
    rt = ROW_DMA_TILE
    nbp = u_p.shape[0] // rt
    nbs = u_s.shape[0] // rt
    grid_spec = pltpu.PrefetchScalarGridSpec(
        num_scalar_prefetch=5,
        grid=(nbp + nbs,),
        in_specs=[
            pl.BlockSpec((rt, D_MODEL), lambda i, *_: (jnp.minimum(i, nbp - 1), 0)),
            pl.BlockSpec((rt, D_MODEL), lambda i, *_: (jnp.maximum(i - nbp, 0), 0)),
        ],
        out_specs=pl.BlockSpec(memory_space=pl.ANY),
        scratch_shapes=[
            pltpu.VMEM((MOE_TILE, D_MODEL), F32),
            pltpu.SemaphoreType.DMA,
            pltpu.SemaphoreType.DMA,
        ],
    )
    return pl.pallas_call(
        functools.partial(_dispatch_kernel, n_blocks_p=nbp, n_tiles=n_tiles, min_tiles=min_tiles),
        grid_spec=grid_spec,
        out_shape=jax.ShapeDtypeStruct((n_tiles * MOE_TILE, D_MODEL), F32),
        compiler_params=_params(1),
        name="moe_dispatch",
    )(pos_a, pos_b, last_tile, has_rows, nvalid, u_p, u_s)


def _moe_kernel(te_ref, nv_ref, x_ref, w1_ref, w3_ref, w2_ref, y_ref, xb_ref, h_ref):
    i = pl.program_id(0)
    k = pl.program_id(1)

    def first():
        xb_ref[...] = x_ref[...].astype(BF16)
        y_ref[...] = jnp.zeros(y_ref.shape, F32)

    @pl.when(i < nv_ref[0])
    def _():
        _swiglu_skewed(k, pl.num_programs(1) - 1, xb_ref,
                       lambda: w1_ref[0], lambda: w3_ref[0], lambda: w2_ref[0], h_ref, y_ref, first)

    @pl.when((i >= nv_ref[0]) & (k == 0))
    def _():
        y_ref[...] = jnp.zeros(y_ref.shape, F32)


def _moe(tile_expert, nvalid, xs, w1, w3, w2):
    tm, tf = MOE_TILE, MOE_FF_TILE
    n_tiles = xs.shape[0] // tm
    kf = w1.shape[2] // tf

    def row(i, k, te, nv):
        return (jnp.minimum(i, nv[0] - 1), 0)

    def out_row(i, k, te, nv):
        return (i, 0)

    def up(i, k, te, nv):
        return (te[jnp.minimum(i, nv[0] - 1)], 0, jnp.where(i < nv[0], jnp.minimum(k, kf - 1), kf - 1))

    def down(i, k, te, nv):
        return (te[jnp.minimum(i, nv[0] - 1)], jnp.where(i < nv[0], jnp.maximum(k - 1, 0), kf - 1), 0)

    grid_spec = pltpu.PrefetchScalarGridSpec(
        num_scalar_prefetch=2,
        grid=(n_tiles, kf + 1),
        in_specs=[
            pl.BlockSpec((tm, D_MODEL), row),
            pl.BlockSpec((1, D_MODEL, tf), up),
            pl.BlockSpec((1, D_MODEL, tf), up),
            pl.BlockSpec((1, tf, D_MODEL), down),
        ],
        out_specs=pl.BlockSpec((tm, D_MODEL), out_row),
        scratch_shapes=[pltpu.VMEM((tm, D_MODEL), BF16), pltpu.VMEM((tm, tf), BF16)],
    )
    return pl.pallas_call(
        _moe_kernel,
        grid_spec=grid_spec,
        out_shape=jax.ShapeDtypeStruct(xs.shape, F32),
        compiler_params=_params(2),
        name="moe_ffn",
    )(tile_expert, nvalid, xs, w1, w3, w2)


def _combine_kernel(pos_a_ref, pos_b_ref, h_ref, ga_ref, gb_ref, g_ref, y_ref, o_ref, ya_ref, yb_ref, sem, *, base):
    i = pl.program_id(0)
    rt = ROW_DMA_TILE

    def body(r, carry):
        tok = base + i * rt + r
        _row_copy(y_ref, pos_a_ref[tok], ya_ref, r, sem).start()
        _row_copy(y_ref, pos_b_ref[tok], yb_ref, r, sem).start()
        return carry

    lax.fori_loop(0, rt, body, 0)
    pltpu.make_async_copy(y_ref.at[pl.ds(0, rt)], ya_ref, sem).wait()
    pltpu.make_async_copy(y_ref.at[pl.ds(0, rt)], yb_ref, sem).wait()
    moe = ga_ref[...] * ya_ref[...] + gb_ref[...] * yb_ref[...]
    o_ref[...] = _rmsnorm(h_ref[...] + moe, g_ref[...])


def _combine(pos_a, pos_b, h, gate_a, gate_b, g, y, base):
    rt = ROW_DMA_TILE
    m = h.shape[0]
    row = lambda i, *_: (i, 0)
    grid_spec = pltpu.PrefetchScalarGridSpec(
        num_scalar_prefetch=2,
        grid=(m // rt,),
        in_specs=[
            pl.BlockSpec((rt, D_MODEL), row),
            pl.BlockSpec((rt, 1), row),
            pl.BlockSpec((rt, 1), row),
            pl.BlockSpec((1, D_MODEL), lambda i, *_: (0, 0)),
            pl.BlockSpec(memory_space=pl.ANY),
        ],
        out_specs=pl.BlockSpec((rt, D_MODEL), row),
        scratch_shapes=[
            pltpu.VMEM((rt, D_MODEL), F32),
            pltpu.VMEM((rt, D_MODEL), F32),
            pltpu.SemaphoreType.DMA,
        ],
    )
    return pl.pallas_call(
        functools.partial(_combine_kernel, base=base),
        grid_spec=grid_spec,
        out_shape=jax.ShapeDtypeStruct((m, D_MODEL), F32),
        compiler_params=_params(1),
        name="moe_combine",
    )(pos_a, pos_b, h, gate_a, gate_b, g, y)


def _route_tables(sel, comb, n_tiles):
    tm = MOE_TILE
    cum = jnp.cumsum(sel, axis=0)
    counts = cum[-1]
    padded = ((counts + tm - 1) // tm) * tm
    ends = jnp.cumsum(padded)
    starts = ends - padded
    slot = starts[None, :] + cum - sel
    order = jnp.cumsum(sel, axis=1)
    first = (sel == 1) & (order == 1)
    second = (sel == 1) & (order == 2)
    pos_a = jnp.sum(jnp.where(first, slot, 0), axis=1).astype(jnp.int32)
    pos_b = jnp.sum(jnp.where(second, slot, 0), axis=1).astype(jnp.int32)
    gate_a = jnp.sum(jnp.where(first, comb, 0.0), axis=1, keepdims=True)
    gate_b = jnp.sum(jnp.where(second, comb, 0.0), axis=1, keepdims=True)
    tile_start = jnp.arange(n_tiles, dtype=jnp.int32) * tm
    tile_expert = jnp.minimum(jnp.sum(tile_start[:, None] >= ends[None, :], axis=1), N_EXPERTS - 1)
    nvalid = (ends[-1:] // tm).astype(jnp.int32)
    last_tile = jnp.maximum(ends - tm, 0).astype(jnp.int32)
    has_rows = (counts > 0).astype(jnp.int32)
    return pos_a, pos_b, gate_a, gate_b, tile_expert.astype(jnp.int32), nvalid, last_tile, has_rows


def kernel(x_prompt, x_sample, state_pool, cache_k, cache_v, meta_tokens, g_pool, w_pool, ls_pool, g_ffn0, w_ff1, w_ff3, w_ff2, g_kv, w_kv, b_kv, g_attn, w_q, b_q, sinks, w_o, b_o, g_ffn1, w_router, w_e1, w_e3, w_e2, g_final):
    n_seq, seq, _ = x_prompt.shape
    n_dec, n_t, _ = x_sample.shape
    past_len = PAST_LEN
    window = cache_k.shape[1]
    assert window == WINDOW and seq % POOL_TILE == 0 and N_META <= POOL_TILE

    vec = lambda a: a.reshape(1, -1).astype(F32)
    g_pool, ls_pool, g_ffn0, g_kv, b_kv, g_attn, b_q, b_o, g_ffn1, g_final = map(
        vec, (g_pool, ls_pool, g_ffn0, g_kv, b_kv, g_attn, b_q, b_o, g_ffn1, g_final))
    w_pool, w_ff1, w_ff3, w_ff2, w_kv, w_q, w_o, w_e1, w_e3, w_e2 = (
        w.astype(BF16) for w in (w_pool, w_ff1, w_ff3, w_ff2, w_kv, w_q, w_o, w_e1, w_e3, w_e2))
    w_router = jnp.pad(w_router, ((0, 0), (0, LANES - N_EXPERTS))).astype(BF16)
    meta = meta_tokens.astype(F32)

    def layer0_tail(h1, tm_ffn, tm_proj):
        h2 = _ffn(h1, g_ffn0, w_ff1, w_ff3, w_ff2, tm_ffn)
        kv, q = _qkv(h2, g_kv, w_kv, b_kv, g_attn, w_q, b_q, tm_proj)
        return h2, kv, q

    x_meta = jnp.pad(meta, ((0, POOL_TILE - N_META), (0, 0)))[None]
    h1_m, _ = _pool_seq(x_meta, jnp.zeros_like(meta), g_pool, w_pool, ls_pool, 0)
    _, kv_m, _ = layer0_tail(h1_m[0], POOL_TILE, POOL_TILE)
    past_kv = jnp.pad(kv_m[:N_META], ((WINDOW - N_META, 0), (0, 0)))

    h1_p, pool_p = _pool_seq(x_prompt, meta, g_pool, w_pool, ls_pool, N_META)
    h2_p, kv_p, q_p = layer0_tail(h1_p.reshape(n_seq * seq, D_MODEL), FFN_TILE, PROJ_TILE)
    o_p = _attn_seq(q_p, kv_p, past_kv, sinks.astype(F32), n_seq, N_META)
    h3_p, u_p, comb_p, sel_p = _oproj(o_p, h2_p, w_o, b_o, g_ffn1, w_router, PROJ_TILE)
    kv_tail = kv_p.reshape(n_seq, seq, 2 * HKV)[:, seq - WINDOW:]
    k_p = kv_tail[..., :HKV].reshape(n_seq, WINDOW, N_KV_HEADS, HEAD_DIM)
    v_p = kv_tail[..., HKV:].reshape(n_seq, WINDOW, N_KV_HEADS, HEAD_DIM)

    h1_s, pool_s = _pool_step(x_sample.transpose(1, 0, 2), state_pool.transpose(1, 0, 2),
                              g_pool, w_pool, ls_pool, past_len)
    n_s = n_t * n_dec
    h2_s, kv_s, q_s = layer0_tail(h1_s.reshape(n_s, D_MODEL), FFN_TILE, PROJ_TILE)
    q5 = q_s.reshape(n_t, n_dec, N_KV_HEADS, GROUP, HEAD_DIM).transpose(1, 2, 0, 3, 4)
    eye = jnp.eye(N_KV_HEADS, dtype=BF16)
    q_bd = (q5[:, :, :, :, None, :] * eye[None, :, None, None, :, None]).reshape(
        n_dec, N_KV_HEADS * n_t * GROUP, HKV)
    new_kv = jnp.pad(kv_s.reshape(n_t, n_dec, 2 * HKV).transpose(1, 0, 2), ((0, 0), (0, 16 - n_t), (0, 0)))
    sink_rows = jnp.broadcast_to(sinks.astype(F32).reshape(N_KV_HEADS, 1, GROUP),
                                 (N_KV_HEADS, n_t, GROUP)).reshape(-1, 1)
    o_s4, k_s, v_s = _attn_step(q_bd, new_kv, cache_k.reshape(n_dec, WINDOW, HKV),
                                cache_v.reshape(n_dec, WINDOW, HKV), sink_rows, past_len, n_t)
    o_s = o_s4.reshape(n_dec, N_KV_HEADS, n_t, GROUP, HEAD_DIM).transpose(2, 0, 1, 3, 4).reshape(n_s, D_MODEL)
    h3_s, u_s, comb_s, sel_s = _oproj(o_s, h2_s, w_o, b_o, g_ffn1, w_router, PROJ_TILE)

    n_tok = n_seq * seq + n_s
    min_tiles = 2 * n_tok // MOE_TILE
    n_tiles = min_tiles + N_EXPERTS
    pos_a, pos_b, gate_a, gate_b, tile_expert, nvalid, last_tile, has_rows = _route_tables(
        jnp.concatenate([sel_p, sel_s]), jnp.concatenate([comb_p, comb_s]), n_tiles)
    xs = _dispatch(pos_a, pos_b, last_tile, has_rows, nvalid, u_p, u_s, n_tiles, min_tiles)
    y = _moe(tile_expert, nvalid, xs, w_e1, w_e3, w_e2)
    n_p = n_seq * seq
    y_p = _combine(pos_a, pos_b, h3_p, gate_a[:n_p], gate_b[:n_p], g_final, y, 0)
    y_s = _combine(pos_a, pos_b, h3_s, gate_a[n_p:], gate_b[n_p:], g_final, y, n_p)

    return (y_p.reshape(n_seq, seq, D_MODEL),
            y_s.reshape(n_t, n_dec, D_MODEL).transpose(1, 0, 2),
            pool_p,
            pool_s.transpose(1, 0, 2),
            k_p, v_p,
            k_s.reshape(n_dec, WINDOW, N_KV_HEADS, HEAD_DIM),
            v_s.reshape(n_dec, WINDOW, N_KV_HEADS, HEAD_DIM))
```

```python
import functools

import jax
import jax.numpy as jnp
from jax import lax
from jax.experimental import pallas as pl
from jax.experimental.pallas import tpu as pltpu

F32 = jnp.float32
BF16 = jnp.bfloat16

D_MODEL = 2048
N_META = 16
POOL_WINDOWS = (2, 4, 8, 16)
POOL_GROUP_DIM = D_MODEL // len(POOL_WINDOWS)
POOL_STATE = max(POOL_WINDOWS) - 1
HEAD_DIM = 64
N_HEADS = D_MODEL // HEAD_DIM
N_KV_HEADS = 4
GROUP = N_HEADS // N_KV_HEADS
HKV = N_KV_HEADS * HEAD_DIM
WINDOW = 128
N_EXPERTS = 8
EPS = 1e-5
PAST_LEN = 8192
MASKED = -1e30

LANES = 128
V7X_VMEM_BYTES = 64 * 2 ** 20
VMEM_LIMIT = V7X_VMEM_BYTES - 8 * 2 ** 20

POOL_TILE = 128
ATTN_TILE = WINDOW
MOE_TILE = 1024
MOE_FF_TILE = 256
FFN_FF_TILE = 512
FFN_TILE = 1024
PROJ_TILE = 256
ROW_DMA_TILE = 256


def _dot(a, b):
    return jnp.dot(a, b, preferred_element_type=F32)


def _dot_nt(a, b):
    return lax.dot_general(a, b, (((1,), (1,)), ((), ())), preferred_element_type=F32)


def _rmsnorm(x, g):
    return x * lax.rsqrt(jnp.mean(x * x, axis=-1, keepdims=True) + EPS) * g


def _params(n_axes):
    return pltpu.CompilerParams(dimension_semantics=("arbitrary",) * n_axes,
                                vmem_limit_bytes=VMEM_LIMIT)


def _pool_seq_kernel(x_ref, meta_ref, g_ref, w_ref, ls_ref, h_ref, st_ref, e_ref, *, pos0):
    t = pl.program_id(1)
    tt = POOL_TILE
    g = g_ref[...]

    @pl.when(t == 0)
    def _():
        e_ref[0:tt - N_META, :] = jnp.zeros((tt - N_META, D_MODEL), F32)
        e_ref[tt - N_META:tt, :] = _rmsnorm(meta_ref[...], g)

    @pl.when(t > 0)
    def _():
        e_ref[0:tt, :] = e_ref[tt:2 * tt, :]

    x = x_ref[0]
    u = _rmsnorm(x, g)
    e_ref[tt:2 * tt, :] = u

    e = e_ref[...]
    hi = e.astype(BF16)
    lo = (e - hi.astype(F32)).astype(BF16)
    r = lax.broadcasted_iota(jnp.int32, (tt, 2 * tt), 0)
    c = lax.broadcasted_iota(jnp.int32, (tt, 2 * tt), 1)
    pos = pos0 + t * tt + lax.broadcasted_iota(jnp.int32, (tt, 1), 0)
    for gi, w in enumerate(POOL_WINDOWS):
        sl = slice(gi * POOL_GROUP_DIM, (gi + 1) * POOL_GROUP_DIM)
        band = jnp.where((c > r + tt - w) & (c <= r + tt), 1.0, 0.0).astype(BF16)
        win_sum = _dot(band, hi[:, sl]) + _dot(band, lo[:, sl])
        cnt = jnp.minimum(w, pos + 1).astype(F32)
        mix = win_sum / cnt - u[:, sl]
        o = _dot(mix.astype(BF16), w_ref[gi]) * ls_ref[:, sl]
        h_ref[0, :, sl] = x[:, sl] + o

    @pl.when(t == pl.num_programs(1) - 1)
    def _():
        st_ref[0] = e_ref[2 * tt - POOL_STATE:2 * tt, :]


def _pool_seq(x, meta, g, w, ls, pos0):
    b, t, _ = x.shape
    tt = POOL_TILE
    return pl.pallas_call(
        functools.partial(_pool_seq_kernel, pos0=pos0),
        grid=(b, t // tt),
        in_specs=[
            pl.BlockSpec((1, tt, D_MODEL), lambda i, j: (i, j, 0)),
            pl.BlockSpec((N_META, D_MODEL), lambda i, j: (0, 0)),
            pl.BlockSpec((1, D_MODEL), lambda i, j: (0, 0)),
            pl.BlockSpec((len(POOL_WINDOWS), POOL_GROUP_DIM, POOL_GROUP_DIM), lambda i, j: (0, 0, 0)),
            pl.BlockSpec((1, D_MODEL), lambda i, j: (0, 0)),
        ],
        out_specs=[
            pl.BlockSpec((1, tt, D_MODEL), lambda i, j: (i, j, 0)),
            pl.BlockSpec((1, POOL_STATE, D_MODEL), lambda i, j: (i, 0, 0)),
        ],
        out_shape=[
            jax.ShapeDtypeStruct((b, t, D_MODEL), F32),
            jax.ShapeDtypeStruct((b, POOL_STATE, D_MODEL), F32),
        ],
        scratch_shapes=[pltpu.VMEM((2 * tt, D_MODEL), F32)],
        compiler_params=_params(2),
        name="pool_seq",
    )(x, meta, g, w, ls)


def _pool_step_kernel(x_ref, st_ref, g_ref, w_ref, ls_ref, h_ref, nst_ref, e_ref, *, pos0):
    n_t, bb, _ = x_ref.shape
    g = g_ref[...]
    for j in range(POOL_STATE):
        e_ref[j] = st_ref[j]
    for t in range(n_t):
        e_ref[POOL_STATE + t] = _rmsnorm(x_ref[t], g)
    for j in range(POOL_STATE):
        nst_ref[j] = e_ref[j + n_t]
    for gi, w in enumerate(POOL_WINDOWS):
        sl = slice(gi * POOL_GROUP_DIM, (gi + 1) * POOL_GROUP_DIM)
        mixes = []
        for t in range(n_t):
            win_sum = e_ref[POOL_STATE + t, :, sl]
            for i in range(1, w):
                win_sum = win_sum + e_ref[POOL_STATE + t - i, :, sl]
            cnt = float(min(w, pos0 + t + 1))
            mixes.append(win_sum / cnt - e_ref[POOL_STATE + t, :, sl])
        mix = jnp.concatenate(mixes, axis=0)
        o = _dot(mix.astype(BF16), w_ref[gi]) * ls_ref[:, sl]
        for t in range(n_t):
            h_ref[t, :, sl] = x_ref[t, :, sl] + o[t * bb:(t + 1) * bb]


def _pool_step(x_t, st_t, g, w, ls, pos0, bb=32):
    n_t, b, _ = x_t.shape
    return pl.pallas_call(
        functools.partial(_pool_step_kernel, pos0=pos0),
        grid=(b // bb,),
        in_specs=[
            pl.BlockSpec((n_t, bb, D_MODEL), lambda i: (0, i, 0)),
            pl.BlockSpec((POOL_STATE, bb, D_MODEL), lambda i: (0, i, 0)),
            pl.BlockSpec((1, D_MODEL), lambda i: (0, 0)),
            pl.BlockSpec((len(POOL_WINDOWS), POOL_GROUP_DIM, POOL_GROUP_DIM), lambda i: (0, 0, 0)),
            pl.BlockSpec((1, D_MODEL), lambda i: (0, 0)),
        ],
        out_specs=[
            pl.BlockSpec((n_t, bb, D_MODEL), lambda i: (0, i, 0)),
            pl.BlockSpec((POOL_STATE, bb, D_MODEL), lambda i: (0, i, 0)),
        ],
        out_shape=[
            jax.ShapeDtypeStruct((n_t, b, D_MODEL), F32),
            jax.ShapeDtypeStruct((POOL_STATE, b, D_MODEL), F32),
        ],
        scratch_shapes=[pltpu.VMEM((POOL_STATE + n_t, bb, D_MODEL), F32)],
        compiler_params=_params(1),
        name="pool_step",
    )(x_t, st_t, g, w, ls)


def _swiglu_skewed(k, n_chunks, parts, w1, w3, w2, first):
    def up():
        for u_ref, h_ref, _ in parts:
            u = u_ref[...]
            h_ref[...] = (jax.nn.silu(_dot(u, w1())) * _dot(u, w3())).astype(h_ref.dtype)

    def down():
        for _, h_ref, acc_ref in parts:
            acc_ref[...] += _dot(h_ref[...], w2())

    @pl.when(k == 0)
    def _():
        first()
        up()

    @pl.when((k > 0) & (k < n_chunks))
    def _():
        down()
        up()

    @pl.when(k == n_chunks)
    def _():
        down()


def _ffn_kernel(x_ref, g_ref, w1_ref, w3_ref, w2_ref, o_ref, u_ref, h_ref):
    def first():
        x = x_ref[...]
        u_ref[...] = _rmsnorm(x, g_ref[...]).astype(BF16)
        o_ref[...] = x

    _swiglu_skewed(pl.program_id(1), pl.num_programs(1) - 1, [(u_ref, h_ref, o_ref)],
                   lambda: w1_ref[...], lambda: w3_ref[...], lambda: w2_ref[...], first)


def _ffn(x, g, w1, w3, w2, tm):
    m = x.shape[0]
    f = w1.shape[1]
    tf = FFN_FF_TILE
    kf = f // tf
    return pl.pallas_call(
        _ffn_kernel,
        grid=(m // tm, kf + 1),
        in_specs=[
            pl.BlockSpec((tm, D_MODEL), lambda i, k: (i, 0)),
            pl.BlockSpec((1, D_MODEL), lambda i, k: (0, 0)),
            pl.BlockSpec((D_MODEL, tf), lambda i, k: (0, jnp.minimum(k, kf - 1))),
            pl.BlockSpec((D_MODEL, tf), lambda i, k: (0, jnp.minimum(k, kf - 1))),
            pl.BlockSpec((tf, D_MODEL), lambda i, k: (jnp.maximum(k - 1, 0), 0)),
        ],
        out_specs=pl.BlockSpec((tm, D_MODEL), lambda i, k: (i, 0)),
        out_shape=jax.ShapeDtypeStruct((m, D_MODEL), F32),
        scratch_shapes=[pltpu.VMEM((tm, D_MODEL), BF16), pltpu.VMEM((tm, tf), BF16)],
        compiler_params=_params(2),
        name="ffn0",
    )(x, g, w1, w3, w2)


def _qkv_kernel(x_ref, gkv_ref, wkv_ref, bkv_ref, gq_ref, wq_ref, bq_ref, kv_ref, q_ref):
    x = x_ref[...]
    xn = x * lax.rsqrt(jnp.mean(x * x, axis=-1, keepdims=True) + EPS)
    kv_ref[...] = _dot((xn * gkv_ref[...]).astype(BF16), wkv_ref[...]) + bkv_ref[...]
    q = _dot((xn * gq_ref[...]).astype(BF16), wq_ref[...]) + bq_ref[...]
    q_ref[...] = (q * HEAD_DIM ** -0.5).astype(BF16)


def _qkv(x, gkv, wkv, bkv, gq, wq, bq, tm):
    m = x.shape[0]
    row = lambda i: (i, 0)
    fixed = lambda i: (0, 0)
    return pl.pallas_call(
        _qkv_kernel,
        grid=(m // tm,),
        in_specs=[
            pl.BlockSpec((tm, D_MODEL), row),
            pl.BlockSpec((1, D_MODEL), fixed),
            pl.BlockSpec((D_MODEL, 2 * HKV), fixed),
            pl.BlockSpec((1, 2 * HKV), fixed),
            pl.BlockSpec((1, D_MODEL), fixed),
            pl.BlockSpec((D_MODEL, D_MODEL), fixed),
            pl.BlockSpec((1, D_MODEL), fixed),
        ],
        out_specs=[pl.BlockSpec((tm, 2 * HKV), row), pl.BlockSpec((tm, D_MODEL), row)],
        out_shape=[jax.ShapeDtypeStruct((m, 2 * HKV), F32), jax.ShapeDtypeStruct((m, D_MODEL), BF16)],
        compiler_params=_params(1),
        name="qkv",
    )(x, gkv, wkv, bkv, gq, wq, bq)


def _attn_seq_kernel(sinks_ref, q_ref, cur_ref, prev_ref, past_ref, o_ref, *, pos0):
    j = pl.program_id(1)
    tq = ATTN_TILE
    prev = jnp.where(j == 0, past_ref[...], prev_ref[...])
    band = jnp.concatenate([prev, cur_ref[...]], axis=0)
    c = lax.broadcasted_iota(jnp.int32, (2 * tq, tq), 0)
    r = lax.broadcasted_iota(jnp.int32, (2 * tq, tq), 1)
    key_pos = pos0 + (j - 1) * tq + c
    valid = (c > r) & (c <= r + tq) & (key_pos >= 0)
    low = lax.broadcasted_iota(jnp.int32, (2 * tq, LANES), 1) < HEAD_DIM
    zeros = jnp.zeros((2 * tq, LANES), F32)
    k_bd, v_bd_t = [], []
    v_t = band[:, HKV:].T
    zv = jnp.zeros((HEAD_DIM, 2 * tq), F32)
    for kvh in range(N_KV_HEADS):
        tile = band[:, (kvh // 2) * LANES:(kvh // 2 + 1) * LANES]
        swapped = pltpu.roll(tile, HEAD_DIM, axis=1)
        in_low, in_high = (tile, swapped) if kvh % 2 == 0 else (swapped, tile)
        k_bd.append(jnp.concatenate([jnp.where(low, in_low, zeros), jnp.where(low, zeros, in_high)],
                                    axis=0).astype(BF16))
        vt = v_t[kvh * HEAD_DIM:(kvh + 1) * HEAD_DIM]
        v_bd_t.append(jnp.concatenate([jnp.concatenate([vt, zv], axis=1),
                                       jnp.concatenate([zv, vt], axis=1)], axis=0).astype(BF16))
    for pair in range(N_HEADS // 2):
        kvh = (2 * pair) // GROUP
        ps = slice(pair * LANES, (pair + 1) * LANES)
        s_t = _dot_nt(k_bd[kvh], q_ref[:, ps])
        probs, inv = [], []
        for half in range(2):
            s = jnp.where(valid, s_t[half * 2 * tq:(half + 1) * 2 * tq], MASKED)
            sink = sinks_ref[2 * pair + half]
            m = jnp.maximum(jnp.max(s, axis=0, keepdims=True), sink)
            p = jnp.exp(s - m)
            denom = jnp.sum(p, axis=0, keepdims=True) + jnp.exp(sink - m)
            probs.append(p.astype(BF16))
            inv.append(jnp.broadcast_to(1.0 / denom, (HEAD_DIM, tq)))
        o_t = _dot(v_bd_t[kvh], jnp.concatenate(probs, axis=0)) * jnp.concatenate(inv, axis=0)
        o_ref[:, ps] = o_t.T.astype(BF16)


def _attn_seq(q, kv, past_kv, sinks, n_seq, pos0):
    m = q.shape[0]
    tq = ATTN_TILE
    nb = m // n_seq // tq
    return pl.pallas_call(
        functools.partial(_attn_seq_kernel, pos0=pos0),
        grid=(n_seq, nb),
        in_specs=[
            pl.BlockSpec(memory_space=pltpu.SMEM),
            pl.BlockSpec((tq, D_MODEL), lambda b, j: (b * nb + j, 0)),
            pl.BlockSpec((tq, 2 * HKV), lambda b, j: (b * nb + j, 0)),
            pl.BlockSpec((tq, 2 * HKV), lambda b, j: (b * nb + jnp.maximum(j - 1, 0), 0)),
            pl.BlockSpec((tq, 2 * HKV), lambda b, j: (0, 0)),
        ],
        out_specs=pl.BlockSpec((tq, D_MODEL), lambda b, j: (b * nb + j, 0)),
        out_shape=jax.ShapeDtypeStruct((m, D_MODEL), BF16),
        compiler_params=_params(2),
        name="attn_seq",
    )(sinks, q, kv, kv, past_kv)


def _attn_step_kernel(q_ref, new_ref, ck_ref, cv_ref, sink_ref, o_ref, nk_ref, nv_ref, *, pos0, n_t):
    bb = q_ref.shape[0]
    rows = N_KV_HEADS * n_t * GROUP
    pad = WINDOW - new_ref.shape[1]
    rho = lax.broadcasted_iota(jnp.int32, (rows, 2 * WINDOW), 0)
    c = lax.broadcasted_iota(jnp.int32, (rows, 2 * WINDOW), 1)
    t = (rho // GROUP) % n_t
    valid = (c > t) & (c <= t + WINDOW) & (pos0 - WINDOW + c >= 0)
    sink = sink_ref[...]
    for i in range(bb):
        new = jnp.concatenate([new_ref[i], jnp.zeros((pad, 2 * HKV), F32)], axis=0)
        kcat = jnp.concatenate([ck_ref[i], new[:, :HKV]], axis=0).astype(BF16)
        vcat = jnp.concatenate([cv_ref[i], new[:, HKV:]], axis=0).astype(BF16)
        s = jnp.where(valid, _dot_nt(q_ref[i], kcat), MASKED)
        m = jnp.maximum(jnp.max(s, axis=-1, keepdims=True), sink)
        p = jnp.exp(s - m)
        denom = jnp.sum(p, axis=-1, keepdims=True) + jnp.exp(sink - m)
        o = (_dot(p.astype(BF16), vcat) / denom).astype(BF16)
        for kvh in range(N_KV_HEADS):
            rs = n_t * GROUP
            o_ref[i, kvh] = o[kvh * rs:(kvh + 1) * rs, kvh * HEAD_DIM:(kvh + 1) * HEAD_DIM]
        nk_ref[i, 0:WINDOW - n_t, :] = ck_ref[i, n_t:WINDOW, :]
        nk_ref[i, WINDOW - n_t:WINDOW, :] = new_ref[i, 0:n_t, 0:HKV]
        nv_ref[i, 0:WINDOW - n_t, :] = cv_ref[i, n_t:WINDOW, :]
        nv_ref[i, WINDOW - n_t:WINDOW, :] = new_ref[i, 0:n_t, HKV:2 * HKV]


def _attn_step(q_bd, new_kv, cache_k, cache_v, sink_rows, pos0, n_t, bb=8):
    b = q_bd.shape[0]
    rows = q_bd.shape[1]
    n_new = new_kv.shape[1]
    blk = lambda *s: pl.BlockSpec((bb,) + s, lambda i: (i,) + (0,) * len(s))
    return pl.pallas_call(
        functools.partial(_attn_step_kernel, pos0=pos0, n_t=n_t),
        grid=(b // bb,),
        in_specs=[
            blk(rows, HKV),
            blk(n_new, 2 * HKV),
            blk(WINDOW, HKV),
            blk(WINDOW, HKV),
            pl.BlockSpec((rows, 1), lambda i: (0, 0)),
        ],
        out_specs=[blk(N_KV_HEADS, n_t * GROUP, HEAD_DIM), blk(WINDOW, HKV), blk(WINDOW, HKV)],
        out_shape=[
            jax.ShapeDtypeStruct((b, N_KV_HEADS, n_t * GROUP, HEAD_DIM), BF16),
            jax.ShapeDtypeStruct((b, WINDOW, HKV), F32),
            jax.ShapeDtypeStruct((b, WINDOW, HKV), F32),
        ],
        compiler_params=_params(1),
        name="attn_step",
    )(q_bd, new_kv, cache_k, cache_v, sink_rows)


def _oproj_kernel(o_ref, h_ref, wo_ref, bo_ref, g_ref, wr_ref, h3_ref, u_ref, comb_ref, sel_ref):
    h3 = h_ref[...] + (_dot(o_ref[...], wo_ref[...]) + bo_ref[...])
    h3_ref[...] = h3
    u = _rmsnorm(h3, g_ref[...])
    u_ref[...] = u
    logits = _dot(u.astype(BF16), wr_ref[...])
    lane = lax.broadcasted_iota(jnp.int32, logits.shape, 1)
    lg = jnp.where(lane < N_EXPERTS, logits, -jnp.inf)
    m1 = jnp.max(lg, axis=-1, keepdims=True)
    i1 = jnp.min(jnp.where(lg == m1, lane, LANES), axis=-1, keepdims=True)
    top1 = lane == i1
    lg2 = jnp.where(top1, -jnp.inf, lg)
    m2 = jnp.max(lg2, axis=-1, keepdims=True)
    i2 = jnp.min(jnp.where(lg2 == m2, lane, LANES), axis=-1, keepdims=True)
    top2 = lane == i2
    e2 = jnp.exp(m2 - m1)
    denom = 1.0 + e2
    comb = jnp.where(top1, 1.0 / denom, 0.0) + jnp.where(top2, e2 / denom, 0.0)
    comb_ref[...] = comb[:, :N_EXPERTS]
    sel_ref[...] = jnp.where(top1 | top2, 1, 0).astype(jnp.int32)[:, :N_EXPERTS]


def _oproj(o, h, wo, bo, g, wr, tm):
    m = o.shape[0]
    row = lambda i: (i, 0)
    fixed = lambda i: (0, 0)
    return pl.pallas_call(
        _oproj_kernel,
        grid=(m // tm,),
        in_specs=[
            pl.BlockSpec((tm, D_MODEL), row),
            pl.BlockSpec((tm, D_MODEL), row),
            pl.BlockSpec((D_MODEL, D_MODEL), fixed),
            pl.BlockSpec((1, D_MODEL), fixed),
            pl.BlockSpec((1, D_MODEL), fixed),
            pl.BlockSpec((D_MODEL, LANES), fixed),
        ],
        out_specs=[
            pl.BlockSpec((tm, D_MODEL), row),
            pl.BlockSpec((tm, D_MODEL), row),
            pl.BlockSpec((tm, N_EXPERTS), row),
            pl.BlockSpec((tm, N_EXPERTS), row),
        ],
        out_shape=[
            jax.ShapeDtypeStruct((m, D_MODEL), F32),
            jax.ShapeDtypeStruct((m, D_MODEL), F32),
            jax.ShapeDtypeStruct((m, N_EXPERTS), F32),
            jax.ShapeDtypeStruct((m, N_EXPERTS), jnp.int32),
        ],
        compiler_params=_params(1),
        name="oproj_router",
    )(o, h, wo, bo, g, wr)


def _row_copy(src, r, dst, p, sem):
    return pltpu.make_async_copy(src.at[pl.ds(r, 1)], dst.at[pl.ds(p, 1)], sem)


def _zero_fill_copies(zeros_ref, xs_ref, last_tile_ref, has_rows_ref, nvalid_ref, sem, n_tiles, min_tiles):
    tm = MOE_TILE
    pairs = []
    for e in range(N_EXPERTS):
        start = pl.multiple_of(last_tile_ref[e], tm)
        cp = pltpu.make_async_copy(zeros_ref, xs_ref.at[pl.ds(start, tm)], sem)
        pairs.append((has_rows_ref[e] != 0, cp))
    for tile in range(min_tiles, n_tiles):
        cp = pltpu.make_async_copy(zeros_ref, xs_ref.at[pl.ds(tile * tm, tm)], sem)
        pairs.append((tile >= nvalid_ref[0], cp))
    return pairs


def _dispatch_kernel(pos_a_ref, pos_b_ref, last_tile_ref, has_rows_ref, nvalid_ref,
                     up_ref, us_ref, xs_ref, zeros_ref, sem, zsem, *, n_blocks_p, n_tiles, min_tiles):
    i = pl.program_id(0)
    rt = ROW_DMA_TILE

    @pl.when(i == 0)
    def _():
        zeros_ref[...] = jnp.zeros(zeros_ref.shape, F32)
        pairs = _zero_fill_copies(zeros_ref, xs_ref, last_tile_ref, has_rows_ref, nvalid_ref,
                                  zsem, n_tiles, min_tiles)
        for cond, cp in pairs:
            @pl.when(cond)
            def _():
                cp.start()
        for cond, cp in pairs:
            @pl.when(cond)
            def _():
                cp.wait()

    def scatter(src_ref):
        def body(r, carry):
            tok = i * rt + r
            _row_copy(src_ref, r, xs_ref, pos_a_ref[tok], sem).start()
            _row_copy(src_ref, r, xs_ref, pos_b_ref[tok], sem).start()
            return carry
        lax.fori_loop(0, rt, body, 0)
        for _ in range(2):
            pltpu.make_async_copy(src_ref, xs_ref.at[pl.ds(0, rt)], sem).wait()

    @pl.when(i < n_blocks_p)
    def _():
        scatter(up_ref)

    @pl.when(i >= n_blocks_p)
    def _():
        scatter(us_ref)


def _dispatch(pos_a, pos_b, last_tile, has_rows, nvalid, u_p, u_s, n_tiles, min_tiles):
    rt = ROW_DMA_TILE
    nbp = u_p.shape[0] // rt
    nbs = u_s.shape[0] // rt
    grid_spec = pltpu.PrefetchScalarGridSpec(
        num_scalar_prefetch=5,
        grid=(nbp + nbs,),
        in_specs=[
            pl.BlockSpec((rt, D_MODEL), lambda i, *_: (jnp.minimum(i, nbp - 1), 0)),
            pl.BlockSpec((rt, D_MODEL), lambda i, *_: (jnp.maximum(i - nbp, 0), 0)),
        ],
        out_specs=pl.BlockSpec(memory_space=pl.ANY),
        scratch_shapes=[
            pltpu.VMEM((MOE_TILE, D_MODEL), F32),
            pltpu.SemaphoreType.DMA,
            pltpu.SemaphoreType.DMA,
        ],
    )
    return pl.pallas_call(
        functools.partial(_dispatch_kernel, n_blocks_p=nbp, n_tiles=n_tiles, min_tiles=min_tiles),
        grid_spec=grid_spec,
        out_shape=jax.ShapeDtypeStruct((n_tiles * MOE_TILE, D_MODEL), F32),
        compiler_params=_params(1),
        name="moe_dispatch",
    )(pos_a, pos_b, last_tile, has_rows, nvalid, u_p, u_s)


def _moe_kernel(te_ref, full_ref, nv_ref, x_ref, w1_ref, w3_ref, w2_ref, y_ref, ha_ref, hb_ref):
    i = pl.program_id(0)
    k = pl.program_id(1)
    half = MOE_TILE // 2
    valid = i < nv_ref[0]
    full = full_ref[i] != 0
    n_chunks = pl.num_programs(1) - 1
    weights = (lambda: w1_ref[0], lambda: w3_ref[0], lambda: w2_ref[0])
    part_a = (x_ref.at[pl.ds(0, half)], ha_ref, y_ref.at[pl.ds(0, half)])
    part_b = (x_ref.at[pl.ds(half, half)], hb_ref, y_ref.at[pl.ds(half, half)])

    def zero_tile():
        y_ref[...] = jnp.zeros(y_ref.shape, F32)

    @pl.when(valid & full)
    def _():
        _swiglu_skewed(k, n_chunks, [part_a, part_b], *weights, zero_tile)

    @pl.when(valid & jnp.logical_not(full))
    def _():
        _swiglu_skewed(k, n_chunks, [part_a], *weights, zero_tile)

    @pl.when(jnp.logical_not(valid) & (k == 0))
    def _():
        zero_tile()


def _moe(tile_expert, tile_full, nvalid, xs, w1, w3, w2):
    tm, tf = MOE_TILE, MOE_FF_TILE
    n_tiles = xs.shape[0] // tm
    kf = w1.shape[2] // tf

    def row(i, k, te, full, nv):
        return (jnp.minimum(i, nv[0] - 1), 0)

    def out_row(i, k, te, full, nv):
        return (i, 0)

    def up(i, k, te, full, nv):
        return (te[jnp.minimum(i, nv[0] - 1)], 0, jnp.where(i < nv[0], jnp.minimum(k, kf - 1), kf - 1))

    def down(i, k, te, full, nv):
        return (te[jnp.minimum(i, nv[0] - 1)], jnp.where(i < nv[0], jnp.maximum(k - 1, 0), kf - 1), 0)

    grid_spec = pltpu.PrefetchScalarGridSpec(
        num_scalar_prefetch=3,
        grid=(n_tiles, kf + 1),
        in_specs=[
            pl.BlockSpec((tm, D_MODEL), row),
            pl.BlockSpec((1, D_MODEL, tf), up),
            pl.BlockSpec((1, D_MODEL, tf), up),
            pl.BlockSpec((1, tf, D_MODEL), down),
        ],
        out_specs=pl.BlockSpec((tm, D_MODEL), out_row),
        scratch_shapes=[pltpu.VMEM((tm // 2, tf), F32), pltpu.VMEM((tm // 2, tf), F32)],
    )
    return pl.pallas_call(
        _moe_kernel,
        grid_spec=grid_spec,
        out_shape=jax.ShapeDtypeStruct(xs.shape, F32),
        compiler_params=_params(2),
        name="moe_ffn",
    )(tile_expert, tile_full, nvalid, xs, w1, w3, w2)


def _combine_kernel(pos_a_ref, pos_b_ref, h_ref, ga_ref, gb_ref, g_ref, y_ref, o_ref, ya_ref, yb_ref, sem, *, base):
    i = pl.program_id(0)
    rt = ROW_DMA_TILE

    def fetch(step, slot):
        def body(r, carry):
            tok = base + step * rt + r
            _row_copy(y_ref, pos_a_ref[tok], ya_ref.at[slot], r, sem.at[slot]).start()
            _row_copy(y_ref, pos_b_ref[tok], yb_ref.at[slot], r, sem.at[slot]).start()
            return carry
        lax.fori_loop(0, rt, body, 0)

    @pl.when(i == 0)
    def _():
        fetch(0, 0)

    @pl.when(i + 1 < pl.num_programs(0))
    def _():
        fetch(i + 1, (i + 1) % 2)

    slot = i % 2
    pltpu.make_async_copy(y_ref.at[pl.ds(0, rt)], ya_ref.at[slot], sem.at[slot]).wait()
    pltpu.make_async_copy(y_ref.at[pl.ds(0, rt)], yb_ref.at[slot], sem.at[slot]).wait()
    moe = ga_ref[...] * ya_ref[slot] + gb_ref[...] * yb_ref[slot]
    o_ref[...] = _rmsnorm(h_ref[...] + moe, g_ref[...])


def _combine(pos_a, pos_b, h, gate_a, gate_b, g, y, base):
    rt = ROW_DMA_TILE
    m = h.shape[0]
    row = lambda i, *_: (i, 0)
    grid_spec = pltpu.PrefetchScalarGridSpec(
        num_scalar_prefetch=2,
        grid=(m // rt,),
        in_specs=[
            pl.BlockSpec((rt, D_MODEL), row),
            pl.BlockSpec((rt, 1), row),
            pl.BlockSpec((rt, 1), row),
            pl.BlockSpec((1, D_MODEL), lambda i, *_: (0, 0)),
            pl.BlockSpec(memory_space=pl.ANY),
        ],
        out_specs=pl.BlockSpec((rt, D_MODEL), row),
        scratch_shapes=[
            pltpu.VMEM((2, rt, D_MODEL), F32),
            pltpu.VMEM((2, rt, D_MODEL), F32),
            pltpu.SemaphoreType.DMA((2,)),
        ],
    )
    return pl.pallas_call(
        functools.partial(_combine_kernel, base=base),
        grid_spec=grid_spec,
        out_shape=jax.ShapeDtypeStruct((m, D_MODEL), F32),
        compiler_params=_params(1),
        name="moe_combine",
    )(pos_a, pos_b, h, gate_a, gate_b, g, y)


def _route_tables(sel, comb, n_tiles):
    tm = MOE_TILE
    cum = jnp.cumsum(sel, axis=0)
    counts = cum[-1]
    padded = ((counts + tm - 1) // tm) * tm
    ends = jnp.cumsum(padded)
    starts = ends - padded
    slot = starts[None, :] + cum - sel
    order = jnp.cumsum(sel, axis=1)
    first = (sel == 1) & (order == 1)
    second = (sel == 1) & (order == 2)
    pos_a = jnp.sum(jnp.where(first, slot, 0), axis=1).astype(jnp.int32)
    pos_b = jnp.sum(jnp.where(second, slot, 0), axis=1).astype(jnp.int32)
    gate_a = jnp.sum(jnp.where(first, comb, 0.0), axis=1, keepdims=True)
    gate_b = jnp.sum(jnp.where(second, comb, 0.0), axis=1, keepdims=True)
    tile_start = jnp.arange(n_tiles, dtype=jnp.int32) * tm
    tile_expert = jnp.minimum(jnp.sum(tile_start[:, None] >= ends[None, :], axis=1), N_EXPERTS - 1)
    rows_in_tile = (starts + counts)[tile_expert] - tile_start
    tile_full = (rows_in_tile > tm // 2).astype(jnp.int32)
    nvalid = (ends[-1:] // tm).astype(jnp.int32)
    last_tile = jnp.maximum(ends - tm, 0).astype(jnp.int32)
    has_rows = (counts > 0).astype(jnp.int32)
    return (pos_a, pos_b, gate_a, gate_b, tile_expert.astype(jnp.int32), tile_full, nvalid,
            last_tile, has_rows)


def kernel(x_prompt, x_sample, state_pool, cache_k, cache_v, meta_tokens, g_pool, w_pool, ls_pool, g_ffn0, w_ff1, w_ff3, w_ff2, g_kv, w_kv, b_kv, g_attn, w_q, b_q, sinks, w_o, b_o, g_ffn1, w_router, w_e1, w_e3, w_e2, g_final):
    n_seq, seq, _ = x_prompt.shape
    n_dec, n_t, _ = x_sample.shape
    past_len = PAST_LEN
    window = cache_k.shape[1]
    assert window == WINDOW and seq % POOL_TILE == 0 and N_META <= POOL_TILE

    vec = lambda a: a.reshape(1, -1).astype(F32)
    g_pool, ls_pool, g_ffn0, g_kv, b_kv, g_attn, b_q, b_o, g_ffn1, g_final = map(
        vec, (g_pool, ls_pool, g_ffn0, g_kv, b_kv, g_attn, b_q, b_o, g_ffn1, g_final))
    w_pool, w_ff1, w_ff3, w_ff2, w_kv, w_q, w_o = (
        w.astype(BF16) for w in (w_pool, w_ff1, w_ff3, w_ff2, w_kv, w_q, w_o))
    w_router = jnp.pad(w_router, ((0, 0), (0, LANES - N_EXPERTS))).astype(BF16)
    meta = meta_tokens.astype(F32)

    def layer0_tail(h1, tm_ffn, tm_proj):
        h2 = _ffn(h1, g_ffn0, w_ff1, w_ff3, w_ff2, min(tm_ffn, h1.shape[0]))
        kv, q = _qkv(h2, g_kv, w_kv, b_kv, g_attn, w_q, b_q, tm_proj)
        return h2, kv, q

    x_meta = jnp.pad(meta, ((0, POOL_TILE - N_META), (0, 0)))[None]
    h1_m, _ = _pool_seq(x_meta, jnp.zeros_like(meta), g_pool, w_pool, ls_pool, 0)
    _, kv_m, _ = layer0_tail(h1_m[0], POOL_TILE, POOL_TILE)
    past_kv = jnp.pad(kv_m[:N_META], ((WINDOW - N_META, 0), (0, 0)))

    h1_p, pool_p = _pool_seq(x_prompt, meta, g_pool, w_pool, ls_pool, N_META)
    h2_p, kv_p, q_p = layer0_tail(h1_p.reshape(n_seq * seq, D_MODEL), FFN_TILE, PROJ_TILE)
    o_p = _attn_seq(q_p, kv_p, past_kv, sinks.astype(F32), n_seq, N_META)
    h3_p, u_p, comb_p, sel_p = _oproj(o_p, h2_p, w_o, b_o, g_ffn1, w_router, PROJ_TILE)
    kv_tail = kv_p.reshape(n_seq, seq, 2 * HKV)[:, seq - WINDOW:]
    k_p = kv_tail[..., :HKV].reshape(n_seq, WINDOW, N_KV_HEADS, HEAD_DIM)
    v_p = kv_tail[..., HKV:].reshape(n_seq, WINDOW, N_KV_HEADS, HEAD_DIM)

    h1_s, pool_s = _pool_step(x_sample.transpose(1, 0, 2), state_pool.transpose(1, 0, 2),
                              g_pool, w_pool, ls_pool, past_len)
    n_s = n_t * n_dec
    h2_s, kv_s, q_s = layer0_tail(h1_s.reshape(n_s, D_MODEL), FFN_TILE, PROJ_TILE)
    q5 = q_s.reshape(n_t, n_dec, N_KV_HEADS, GROUP, HEAD_DIM).transpose(1, 2, 0, 3, 4)
    eye = jnp.eye(N_KV_HEADS, dtype=BF16)
    q_bd = (q5[:, :, :, :, None, :] * eye[None, :, None, None, :, None]).reshape(
        n_dec, N_KV_HEADS * n_t * GROUP, HKV)
    new_kv = jnp.pad(kv_s.reshape(n_t, n_dec, 2 * HKV).transpose(1, 0, 2), ((0, 0), (0, 16 - n_t), (0, 0)))
    sink_rows = jnp.broadcast_to(sinks.astype(F32).reshape(N_KV_HEADS, 1, GROUP),
                                 (N_KV_HEADS, n_t, GROUP)).reshape(-1, 1)
    o_s4, k_s, v_s = _attn_step(q_bd, new_kv, cache_k.reshape(n_dec, WINDOW, HKV),
                                cache_v.reshape(n_dec, WINDOW, HKV), sink_rows, past_len, n_t)
    o_s = o_s4.reshape(n_dec, N_KV_HEADS, n_t, GROUP, HEAD_DIM).transpose(2, 0, 1, 3, 4).reshape(n_s, D_MODEL)
    h3_s, u_s, comb_s, sel_s = _oproj(o_s, h2_s, w_o, b_o, g_ffn1, w_router, PROJ_TILE)

    n_tok = n_seq * seq + n_s
    min_tiles = 2 * n_tok // MOE_TILE
    n_tiles = min_tiles + N_EXPERTS
    pos_a, pos_b, gate_a, gate_b, tile_expert, tile_full, nvalid, last_tile, has_rows = _route_tables(
        jnp.concatenate([sel_p, sel_s]), jnp.concatenate([comb_p, comb_s]), n_tiles)
    xs = _dispatch(pos_a, pos_b, last_tile, has_rows, nvalid, u_p, u_s, n_tiles, min_tiles)
    y = _moe(tile_expert, tile_full, nvalid, xs, w_e1.astype(F32), w_e3.astype(F32), w_e2.astype(F32))
    n_p = n_seq * seq
    y_p = _combine(pos_a, pos_b, h3_p, gate_a[:n_p], gate_b[:n_p], g_final, y, 0)
    y_s = _combine(pos_a, pos_b, h3_s, gate_a[n_p:], gate_b[n_p:], g_final, y, n_p)

    return (y_p.reshape(n_seq, seq, D_MODEL),
            y_s.reshape(n_t, n_dec, D_MODEL).transpose(1, 0, 2),
            pool_p,
            pool_s.transpose(1, 0, 2),
            k_p, v_p,
            k_s.reshape(n_dec, WINDOW, N_KV_HEADS, HEAD_DIM),
            v_s.reshape(n_dec, WINDOW, N_KV_HEADS, HEAD_DIM))
```

```python
import functools

import jax
import jax.numpy as jnp
from jax import lax
from jax.experimental import pallas as pl
from jax.experimental.pallas import tpu as pltpu

F32 = jnp.float32
BF16 = jnp.bfloat16

D_MODEL = 2048
N_META = 16
POOL_WINDOWS = (2, 4, 8, 16)
POOL_GROUP_DIM = D_MODEL // len(POOL_WINDOWS)
POOL_STATE = max(POOL_WINDOWS) - 1
HEAD_DIM = 64
N_HEADS = D_MODEL // HEAD_DIM
N_KV_HEADS = 4
GROUP = N_HEADS // N_KV_HEADS
HKV = N_KV_HEADS * HEAD_DIM
WINDOW = 128
N_EXPERTS = 8
EPS = 1e-5
PAST_LEN = 8192
MASKED = -1e30

LANES = 128
V7X_VMEM_BYTES = 64 * 2 ** 20
VMEM_LIMIT = V7X_VMEM_BYTES - 8 * 2 ** 20

POOL_TILE = 128
ATTN_TILE = WINDOW
MOE_TILE = 1024
MOE_FF_TILE = 512
FFN_FF_TILE = 512
FFN_TILE = 1024
PROJ_TILE = 256
ROW_DMA_TILE = 256


def _dot(a, b):
    return jnp.dot(a, b, preferred_element_type=F32)


def _dot_nt(a, b):
    return lax.dot_general(a, b, (((1,), (1,)), ((), ())), preferred_element_type=F32)


def _rmsnorm(x, g):
    return x * lax.rsqrt(jnp.mean(x * x, axis=-1, keepdims=True) + EPS) * g


def _params(n_axes):
    return pltpu.CompilerParams(dimension_semantics=("arbitrary",) * n_axes,
                                vmem_limit_bytes=VMEM_LIMIT)


def _pool_seq_kernel(x_ref, meta_ref, g_ref, w_ref, ls_ref, h_ref, st_ref, e_ref, *, pos0):
    t = pl.program_id(1)
    tt = POOL_TILE
    g = g_ref[...]

    @pl.when(t == 0)
    def _():
        e_ref[0:tt - N_META, :] = jnp.zeros((tt - N_META, D_MODEL), F32)
        e_ref[tt - N_META:tt, :] = _rmsnorm(meta_ref[...], g)

    @pl.when(t > 0)
    def _():
        e_ref[0:tt, :] = e_ref[tt:2 * tt, :]

    x = x_ref[0]
    u = _rmsnorm(x, g)
    e_ref[tt:2 * tt, :] = u

    e = e_ref[...]
    hi = e.astype(BF16)
    lo = (e - hi.astype(F32)).astype(BF16)
    r = lax.broadcasted_iota(jnp.int32, (tt, 2 * tt), 0)
    c = lax.broadcasted_iota(jnp.int32, (tt, 2 * tt), 1)
    pos = pos0 + t * tt + lax.broadcasted_iota(jnp.int32, (tt, 1), 0)
    for gi, w in enumerate(POOL_WINDOWS):
        sl = slice(gi * POOL_GROUP_DIM, (gi + 1) * POOL_GROUP_DIM)
        band = jnp.where((c > r + tt - w) & (c <= r + tt), 1.0, 0.0).astype(BF16)
        win_sum = _dot(band, hi[:, sl]) + _dot(band, lo[:, sl])
        cnt = jnp.minimum(w, pos + 1).astype(F32)
        mix = win_sum / cnt - u[:, sl]
        o = _dot(mix.astype(BF16), w_ref[gi]) * ls_ref[:, sl]
        h_ref[0, :, sl] = x[:, sl] + o

    @pl.when(t == pl.num_programs(1) - 1)
    def _():
        st_ref[0] = e_ref[2 * tt - POOL_STATE:2 * tt, :]


def _pool_seq(x, meta, g, w, ls, pos0):
    b, t, _ = x.shape
    tt = POOL_TILE
    return pl.pallas_call(
        functools.partial(_pool_seq_kernel, pos0=pos0),
        grid=(b, t // tt),
        in_specs=[
            pl.BlockSpec((1, tt, D_MODEL), lambda i, j: (i, j, 0)),
            pl.BlockSpec((N_META, D_MODEL), lambda i, j: (0, 0)),
            pl.BlockSpec((1, D_MODEL), lambda i, j: (0, 0)),
            pl.BlockSpec((len(POOL_WINDOWS), POOL_GROUP_DIM, POOL_GROUP_DIM), lambda i, j: (0, 0, 0)),
            pl.BlockSpec((1, D_MODEL), lambda i, j: (0, 0)),
        ],
        out_specs=[
            pl.BlockSpec((1, tt, D_MODEL), lambda i, j: (i, j, 0)),
            pl.BlockSpec((1, POOL_STATE, D_MODEL), lambda i, j: (i, 0, 0)),
        ],
        out_shape=[
            jax.ShapeDtypeStruct((b, t, D_MODEL), F32),
            jax.ShapeDtypeStruct((b, POOL_STATE, D_MODEL), F32),
        ],
        scratch_shapes=[pltpu.VMEM((2 * tt, D_MODEL), F32)],
        compiler_params=_params(2),
        name="pool_seq",
    )(x, meta, g, w, ls)


def _pool_step_kernel(x_ref, st_ref, g_ref, w_ref, ls_ref, h_ref, nst_ref, e_ref, *, pos0):
    n_t, bb, _ = x_ref.shape
    g = g_ref[...]
    for j in range(POOL_STATE):
        e_ref[j] = st_ref[j]
    for t in range(n_t):
        e_ref[POOL_STATE + t] = _rmsnorm(x_ref[t], g)
    for j in range(POOL_STATE):
        nst_ref[j] = e_ref[j + n_t]
    for gi, w in enumerate(POOL_WINDOWS):
        sl = slice(gi * POOL_GROUP_DIM, (gi + 1) * POOL_GROUP_DIM)
        mixes = []
        for t in range(n_t):
            win_sum = e_ref[POOL_STATE + t, :, sl]
            for i in range(1, w):
                win_sum = win_sum + e_ref[POOL_STATE + t - i, :, sl]
            cnt = float(min(w, pos0 + t + 1))
            mixes.append(win_sum / cnt - e_ref[POOL_STATE + t, :, sl])
        mix = jnp.concatenate(mixes, axis=0)
        o = _dot(mix.astype(BF16), w_ref[gi]) * ls_ref[:, sl]
        for t in range(n_t):
            h_ref[t, :, sl] = x_ref[t, :, sl] + o[t * bb:(t + 1) * bb]


def _pool_step(x_t, st_t, g, w, ls, pos0, bb=32):
    n_t, b, _ = x_t.shape
    return pl.pallas_call(
        functools.partial(_pool_step_kernel, pos0=pos0),
        grid=(b // bb,),
        in_specs=[
            pl.BlockSpec((n_t, bb, D_MODEL), lambda i: (0, i, 0)),
            pl.BlockSpec((POOL_STATE, bb, D_MODEL), lambda i: (0, i, 0)),
            pl.BlockSpec((1, D_MODEL), lambda i: (0, 0)),
            pl.BlockSpec((len(POOL_WINDOWS), POOL_GROUP_DIM, POOL_GROUP_DIM), lambda i: (0, 0, 0)),
            pl.BlockSpec((1, D_MODEL), lambda i: (0, 0)),
        ],
        out_specs=[
            pl.BlockSpec((n_t, bb, D_MODEL), lambda i: (0, i, 0)),
            pl.BlockSpec((POOL_STATE, bb, D_MODEL), lambda i: (0, i, 0)),
        ],
        out_shape=[
            jax.ShapeDtypeStruct((n_t, b, D_MODEL), F32),
            jax.ShapeDtypeStruct((POOL_STATE, b, D_MODEL), F32),
        ],
        scratch_shapes=[pltpu.VMEM((POOL_STATE + n_t, bb, D_MODEL), F32)],
        compiler_params=_params(1),
        name="pool_step",
    )(x_t, st_t, g, w, ls)


def _swiglu_skewed(k, n_chunks, parts, w1, w3, w2, first):
    def up():
        for u_ref, h_ref, _ in parts:
            u = u_ref[...]
            h_ref[...] = (jax.nn.silu(_dot(u, w1())) * _dot(u, w3())).astype(h_ref.dtype)

    def down():
        for _, h_ref, acc_ref in parts:
            acc_ref[...] += _dot(h_ref[...], w2())

    @pl.when(k == 0)
    def _():
        first()
        up()

    @pl.when((k > 0) & (k < n_chunks))
    def _():
        down()
        up()

    @pl.when(k == n_chunks)
    def _():
        down()


def _ffn_kernel(x_ref, g_ref, w1_ref, w3_ref, w2_ref, o_ref, u_ref, h_ref):
    def first():
        x = x_ref[...]
        u_ref[...] = _rmsnorm(x, g_ref[...]).astype(BF16)
        o_ref[...] = x

    _swiglu_skewed(pl.program_id(1), pl.num_programs(1) - 1, [(u_ref, h_ref, o_ref)],
                   lambda: w1_ref[...], lambda: w3_ref[...], lambda: w2_ref[...], first)


def _ffn(x, g, w1, w3, w2, tm):
    m = x.shape[0]
    f = w1.shape[1]
    tf = FFN_FF_TILE
    kf = f // tf
    return pl.pallas_call(
        _ffn_kernel,
        grid=(m // tm, kf + 1),
        in_specs=[
            pl.BlockSpec((tm, D_MODEL), lambda i, k: (i, 0)),
            pl.BlockSpec((1, D_MODEL), lambda i, k: (0, 0)),
            pl.BlockSpec((D_MODEL, tf), lambda i, k: (0, jnp.minimum(k, kf - 1))),
            pl.BlockSpec((D_MODEL, tf), lambda i, k: (0, jnp.minimum(k, kf - 1))),
            pl.BlockSpec((tf, D_MODEL), lambda i, k: (jnp.maximum(k - 1, 0), 0)),
        ],
        out_specs=pl.BlockSpec((tm, D_MODEL), lambda i, k: (i, 0)),
        out_shape=jax.ShapeDtypeStruct((m, D_MODEL), F32),
        scratch_shapes=[pltpu.VMEM((tm, D_MODEL), BF16), pltpu.VMEM((tm, tf), BF16)],
        compiler_params=_params(2),
        name="ffn0",
    )(x, g, w1, w3, w2)


def _qkv_kernel(x_ref, gkv_ref, wkv_ref, bkv_ref, gq_ref, wq_ref, bq_ref, kv_ref, q_ref):
    x = x_ref[...]
    xn = x * lax.rsqrt(jnp.mean(x * x, axis=-1, keepdims=True) + EPS)
    kv_ref[...] = _dot((xn * gkv_ref[...]).astype(BF16), wkv_ref[...]) + bkv_ref[...]
    q = _dot((xn * gq_ref[...]).astype(BF16), wq_ref[...]) + bq_ref[...]
    q_ref[...] = (q * HEAD_DIM ** -0.5).astype(BF16)


def _qkv(x, gkv, wkv, bkv, gq, wq, bq, tm):
    m = x.shape[0]
    row = lambda i: (i, 0)
    fixed = lambda i: (0, 0)
    return pl.pallas_call(
        _qkv_kernel,
        grid=(m // tm,),
        in_specs=[
            pl.BlockSpec((tm, D_MODEL), row),
            pl.BlockSpec((1, D_MODEL), fixed),
            pl.BlockSpec((D_MODEL, 2 * HKV), fixed),
            pl.BlockSpec((1, 2 * HKV), fixed),
            pl.BlockSpec((1, D_MODEL), fixed),
            pl.BlockSpec((D_MODEL, D_MODEL), fixed),
            pl.BlockSpec((1, D_MODEL), fixed),
        ],
        out_specs=[pl.BlockSpec((tm, 2 * HKV), row), pl.BlockSpec((tm, D_MODEL), row)],
        out_shape=[jax.ShapeDtypeStruct((m, 2 * HKV), F32), jax.ShapeDtypeStruct((m, D_MODEL), BF16)],
        compiler_params=_params(1),
        name="qkv",
    )(x, gkv, wkv, bkv, gq, wq, bq)


def _attn_seq_kernel(sinks_ref, q_ref, cur_ref, prev_ref, past_ref, o_ref, *, pos0):
    j = pl.program_id(1)
    tq = ATTN_TILE
    prev = jnp.where(j == 0, past_ref[...], prev_ref[...])
    band = jnp.concatenate([prev, cur_ref[...]], axis=0)
    c = lax.broadcasted_iota(jnp.int32, (2 * tq, tq), 0)
    r = lax.broadcasted_iota(jnp.int32, (2 * tq, tq), 1)
    key_pos = pos0 + (j - 1) * tq + c
    valid = (c > r) & (c <= r + tq) & (key_pos >= 0)
    low = lax.broadcasted_iota(jnp.int32, (2 * tq, LANES), 1) < HEAD_DIM
    zeros = jnp.zeros((2 * tq, LANES), F32)
    k_bd, v_bd_t = [], []
    v_t = band[:, HKV:].T
    zv = jnp.zeros((HEAD_DIM, 2 * tq), F32)
    for kvh in range(N_KV_HEADS):
        tile = band[:, (kvh // 2) * LANES:(kvh // 2 + 1) * LANES]
        swapped = pltpu.roll(tile, HEAD_DIM, axis=1)
        in_low, in_high = (tile, swapped) if kvh % 2 == 0 else (swapped, tile)
        k_bd.append(jnp.concatenate([jnp.where(low, in_low, zeros), jnp.where(low, zeros, in_high)],
                                    axis=0).astype(BF16))
        vt = v_t[kvh * HEAD_DIM:(kvh + 1) * HEAD_DIM]
        v_bd_t.append(jnp.concatenate([jnp.concatenate([vt, zv], axis=1),
                                       jnp.concatenate([zv, vt], axis=1)], axis=0).astype(BF16))
    for pair in range(N_HEADS // 2):
        kvh = (2 * pair) // GROUP
        ps = slice(pair * LANES, (pair + 1) * LANES)
        s_t = _dot_nt(k_bd[kvh], q_ref[:, ps])
        probs, inv = [], []
        for half in range(2):
            s = jnp.where(valid, s_t[half * 2 * tq:(half + 1) * 2 * tq], MASKED)
            sink = sinks_ref[2 * pair + half]
            m = jnp.maximum(jnp.max(s, axis=0, keepdims=True), sink)
            p = jnp.exp(s - m)
            denom = jnp.sum(p, axis=0, keepdims=True) + jnp.exp(sink - m)
            probs.append(p.astype(BF16))
            inv.append(jnp.broadcast_to(1.0 / denom, (HEAD_DIM, tq)))
        o_t = _dot(v_bd_t[kvh], jnp.concatenate(probs, axis=0)) * jnp.concatenate(inv, axis=0)
        o_ref[:, ps] = o_t.T.astype(BF16)


def _attn_seq(q, kv, past_kv, sinks, n_seq, pos0):
    m = q.shape[0]
    tq = ATTN_TILE
    nb = m // n_seq // tq
    return pl.pallas_call(
        functools.partial(_attn_seq_kernel, pos0=pos0),
        grid=(n_seq, nb),
        in_specs=[
            pl.BlockSpec(memory_space=pltpu.SMEM),
            pl.BlockSpec((tq, D_MODEL), lambda b, j: (b * nb + j, 0)),
            pl.BlockSpec((tq, 2 * HKV), lambda b, j: (b * nb + j, 0)),
            pl.BlockSpec((tq, 2 * HKV), lambda b, j: (b * nb + jnp.maximum(j - 1, 0), 0)),
            pl.BlockSpec((tq, 2 * HKV), lambda b, j: (0, 0)),
        ],
        out_specs=pl.BlockSpec((tq, D_MODEL), lambda b, j: (b * nb + j, 0)),
        out_shape=jax.ShapeDtypeStruct((m, D_MODEL), BF16),
        compiler_params=_params(2),
        name="attn_seq",
    )(sinks, q, kv, kv, past_kv)


def _attn_step_kernel(q_ref, new_ref, ck_ref, cv_ref, sink_ref, o_ref, nk_ref, nv_ref, *, pos0, n_t):
    bb = q_ref.shape[0]
    rows = N_KV_HEADS * n_t * GROUP
    pad = WINDOW - new_ref.shape[1]
    rho = lax.broadcasted_iota(jnp.int32, (rows, 2 * WINDOW), 0)
    c = lax.broadcasted_iota(jnp.int32, (rows, 2 * WINDOW), 1)
    t = (rho // GROUP) % n_t
    valid = (c > t) & (c <= t + WINDOW) & (pos0 - WINDOW + c >= 0)
    sink = sink_ref[...]
    for i in range(bb):
        new = jnp.concatenate([new_ref[i], jnp.zeros((pad, 2 * HKV), F32)], axis=0)
        kcat = jnp.concatenate([ck_ref[i], new[:, :HKV]], axis=0).astype(BF16)
        vcat = jnp.concatenate([cv_ref[i], new[:, HKV:]], axis=0).astype(BF16)
        s = jnp.where(valid, _dot_nt(q_ref[i], kcat), MASKED)
        m = jnp.maximum(jnp.max(s, axis=-1, keepdims=True), sink)
        p = jnp.exp(s - m)
        denom = jnp.sum(p, axis=-1, keepdims=True) + jnp.exp(sink - m)
        o = (_dot(p.astype(BF16), vcat) / denom).astype(BF16)
        for kvh in range(N_KV_HEADS):
            rs = n_t * GROUP
            o_ref[i, kvh] = o[kvh * rs:(kvh + 1) * rs, kvh * HEAD_DIM:(kvh + 1) * HEAD_DIM]
        nk_ref[i, 0:WINDOW - n_t, :] = ck_ref[i, n_t:WINDOW, :]
        nk_ref[i, WINDOW - n_t:WINDOW, :] = new_ref[i, 0:n_t, 0:HKV]
        nv_ref[i, 0:WINDOW - n_t, :] = cv_ref[i, n_t:WINDOW, :]
        nv_ref[i, WINDOW - n_t:WINDOW, :] = new_ref[i, 0:n_t, HKV:2 * HKV]


def _attn_step(q_bd, new_kv, cache_k, cache_v, sink_rows, pos0, n_t, bb=8):
    b = q_bd.shape[0]
    rows = q_bd.shape[1]
    n_new = new_kv.shape[1]
    blk = lambda *s: pl.BlockSpec((bb,) + s, lambda i: (i,) + (0,) * len(s))
    return pl.pallas_call(
        functools.partial(_attn_step_kernel, pos0=pos0, n_t=n_t),
        grid=(b // bb,),
        in_specs=[
            blk(rows, HKV),
            blk(n_new, 2 * HKV),
            blk(WINDOW, HKV),
            blk(WINDOW, HKV),
            pl.BlockSpec((rows, 1), lambda i: (0, 0)),
        ],
        out_specs=[blk(N_KV_HEADS, n_t * GROUP, HEAD_DIM), blk(WINDOW, HKV), blk(WINDOW, HKV)],
        out_shape=[
            jax.ShapeDtypeStruct((b, N_KV_HEADS, n_t * GROUP, HEAD_DIM), BF16),
            jax.ShapeDtypeStruct((b, WINDOW, HKV), F32),
            jax.ShapeDtypeStruct((b, WINDOW, HKV), F32),
        ],
        compiler_params=_params(1),
        name="attn_step",
    )(q_bd, new_kv, cache_k, cache_v, sink_rows)


def _oproj_kernel(o_ref, h_ref, wo_ref, bo_ref, g_ref, wr_ref, h3_ref, u_ref, comb_ref, sel_ref):
    h3 = h_ref[...] + (_dot(o_ref[...], wo_ref[...]) + bo_ref[...])
    h3_ref[...] = h3
    u = _rmsnorm(h3, g_ref[...])
    u_ref[...] = u
    logits = _dot(u.astype(BF16), wr_ref[...])
    lane = lax.broadcasted_iota(jnp.int32, logits.shape, 1)
    lg = jnp.where(lane < N_EXPERTS, logits, -jnp.inf)
    m1 = jnp.max(lg, axis=-1, keepdims=True)
    i1 = jnp.min(jnp.where(lg == m1, lane, LANES), axis=-1, keepdims=True)
    top1 = lane == i1
    lg2 = jnp.where(top1, -jnp.inf, lg)
    m2 = jnp.max(lg2, axis=-1, keepdims=True)
    i2 = jnp.min(jnp.where(lg2 == m2, lane, LANES), axis=-1, keepdims=True)
    top2 = lane == i2
    e2 = jnp.exp(m2 - m1)
    denom = 1.0 + e2
    comb = jnp.where(top1, 1.0 / denom, 0.0) + jnp.where(top2, e2 / denom, 0.0)
    comb_ref[...] = comb[:, :N_EXPERTS]
    sel_ref[...] = jnp.where(top1 | top2, 1, 0).astype(jnp.int32)[:, :N_EXPERTS]


def _oproj(o, h, wo, bo, g, wr, tm):
    m = o.shape[0]
    row = lambda i: (i, 0)
    fixed = lambda i: (0, 0)
    return pl.pallas_call(
        _oproj_kernel,
        grid=(m // tm,),
        in_specs=[
            pl.BlockSpec((tm, D_MODEL), row),
            pl.BlockSpec((tm, D_MODEL), row),
            pl.BlockSpec((D_MODEL, D_MODEL), fixed),
            pl.BlockSpec((1, D_MODEL), fixed),
            pl.BlockSpec((1, D_MODEL), fixed),
            pl.BlockSpec((D_MODEL, LANES), fixed),
        ],
        out_specs=[
            pl.BlockSpec((tm, D_MODEL), row),
            pl.BlockSpec((tm, D_MODEL), row),
            pl.BlockSpec((tm, N_EXPERTS), row),
            pl.BlockSpec((tm, N_EXPERTS), row),
        ],
        out_shape=[
            jax.ShapeDtypeStruct((m, D_MODEL), F32),
            jax.ShapeDtypeStruct((m, D_MODEL), F32),
            jax.ShapeDtypeStruct((m, N_EXPERTS), F32),
            jax.ShapeDtypeStruct((m, N_EXPERTS), jnp.int32),
        ],
        compiler_params=_params(1),
        name="oproj_router",
    )(o, h, wo, bo, g, wr)


def _row_copy(src, r, dst, p, sem):
    return pltpu.make_async_copy(src.at[pl.ds(r, 1)], dst.at[pl.ds(p, 1)], sem)


def _zero_fill_copies(zeros_ref, xs_ref, last_tile_ref, has_rows_ref, nvalid_ref, sem, n_tiles, min_tiles):
    tm = MOE_TILE
    pairs = []
    for e in range(N_EXPERTS):
        start = pl.multiple_of(last_tile_ref[e], tm)
        cp = pltpu.make_async_copy(zeros_ref, xs_ref.at[pl.ds(start, tm)], sem)
        pairs.append((has_rows_ref[e] != 0, cp))
    for tile in range(min_tiles, n_tiles):
        cp = pltpu.make_async_copy(zeros_ref, xs_ref.at[pl.ds(tile * tm, tm)], sem)
        pairs.append((tile >= nvalid_ref[0], cp))
    return pairs


def _dispatch_kernel(pos_a_ref, pos_b_ref, last_tile_ref, has_rows_ref, nvalid_ref,
                     up_ref, us_ref, xs_ref, zeros_ref, sem, zsem, *, n_blocks_p, n_tiles, min_tiles):
    i = pl.program_id(0)
    rt = ROW_DMA_TILE

    @pl.when(i == 0)
    def _():
        zeros_ref[...] = jnp.zeros(zeros_ref.shape, F32)
        pairs = _zero_fill_copies(zeros_ref, xs_ref, last_tile_ref, has_rows_ref, nvalid_ref,
                                  zsem, n_tiles, min_tiles)
        for cond, cp in pairs:
            @pl.when(cond)
            def _():
                cp.start()
        for cond, cp in pairs:
            @pl.when(cond)
            def _():
                cp.wait()

    def scatter(src_ref):
        def body(r, carry):
            tok = i * rt + r
            _row_copy(src_ref, r, xs_ref, pos_a_ref[tok], sem).start()
            _row_copy(src_ref, r, xs_ref, pos_b_ref[tok], sem).start()
            return carry
        lax.fori_loop(0, rt, body, 0, unroll=8)
        for _ in range(2):
            pltpu.make_async_copy(src_ref, xs_ref.at[pl.ds(0, rt)], sem).wait()

    @pl.when(i < n_blocks_p)
    def _():
        scatter(up_ref)

    @pl.when(i >= n_blocks_p)
    def _():
        scatter(us_ref)


def _dispatch(pos_a, pos_b, last_tile, has_rows, nvalid, u_p, u_s, n_tiles, min_tiles):
    rt = ROW_DMA_TILE
    nbp = u_p.shape[0] // rt
    nbs = u_s.shape[0] // rt
    grid_spec = pltpu.PrefetchScalarGridSpec(
        num_scalar_prefetch=5,
        grid=(nbp + nbs,),
        in_specs=[
            pl.BlockSpec((rt, D_MODEL), lambda i, *_: (jnp.minimum(i, nbp - 1), 0)),
            pl.BlockSpec((rt, D_MODEL), lambda i, *_: (jnp.maximum(i - nbp, 0), 0)),
        ],
        out_specs=pl.BlockSpec(memory_space=pl.ANY),
        scratch_shapes=[
            pltpu.VMEM((MOE_TILE, D_MODEL), F32),
            pltpu.SemaphoreType.DMA,
            pltpu.SemaphoreType.DMA,
        ],
    )
    return pl.pallas_call(
        functools.partial(_dispatch_kernel, n_blocks_p=nbp, n_tiles=n_tiles, min_tiles=min_tiles),
        grid_spec=grid_spec,
        out_shape=jax.ShapeDtypeStruct((n_tiles * MOE_TILE, D_MODEL), F32),
        compiler_params=_params(1),
        name="moe_dispatch",
    )(pos_a, pos_b, last_tile, has_rows, nvalid, u_p, u_s)


def _moe_kernel(te_ref, full_ref, nv_ref, x_ref, w1_ref, w3_ref, w2_ref, y_ref, ha_ref, hb_ref):
    i = pl.program_id(0)
    k = pl.program_id(1)
    half = MOE_TILE // 2
    valid = i < nv_ref[0]
    full = full_ref[i] != 0
    n_chunks = pl.num_programs(1) - 1
    weights = (lambda: w1_ref[0], lambda: w3_ref[0], lambda: w2_ref[0])
    part_a = (x_ref.at[pl.ds(0, half)], ha_ref, y_ref.at[pl.ds(0, half)])
    part_b = (x_ref.at[pl.ds(half, half)], hb_ref, y_ref.at[pl.ds(half, half)])

    def zero_tile():
        y_ref[...] = jnp.zeros(y_ref.shape, F32)

    @pl.when(valid & full)
    def _():
        _swiglu_skewed(k, n_chunks, [part_a, part_b], *weights, zero_tile)

    @pl.when(valid & jnp.logical_not(full))
    def _():
        _swiglu_skewed(k, n_chunks, [part_a], *weights, zero_tile)

    @pl.when(jnp.logical_not(valid) & (k == 0))
    def _():
        zero_tile()


def _moe(tile_expert, tile_full, nvalid, xs, w1, w3, w2):
    tm, tf = MOE_TILE, MOE_FF_TILE
    n_tiles = xs.shape[0] // tm
    kf = w1.shape[2] // tf

    def row(i, k, te, full, nv):
        return (jnp.minimum(i, nv[0] - 1), 0)

    def out_row(i, k, te, full, nv):
        return (i, 0)

    def up(i, k, te, full, nv):
        return (te[jnp.minimum(i, nv[0] - 1)], 0, jnp.where(i < nv[0], jnp.minimum(k, kf - 1), kf - 1))

    def down(i, k, te, full, nv):
        return (te[jnp.minimum(i, nv[0] - 1)], jnp.where(i < nv[0], jnp.maximum(k - 1, 0), kf - 1), 0)

    grid_spec = pltpu.PrefetchScalarGridSpec(
        num_scalar_prefetch=3,
        grid=(n_tiles, kf + 1),
        in_specs=[
            pl.BlockSpec((tm, D_MODEL), row, pipeline_mode=pl.Buffered(1)),
            pl.BlockSpec((1, D_MODEL, tf), up),
            pl.BlockSpec((1, D_MODEL, tf), up),
            pl.BlockSpec((1, tf, D_MODEL), down),
        ],
        out_specs=pl.BlockSpec((tm, D_MODEL), out_row, pipeline_mode=pl.Buffered(1)),
        scratch_shapes=[pltpu.VMEM((tm // 2, tf), F32), pltpu.VMEM((tm // 2, tf), F32)],
    )
    return pl.pallas_call(
        _moe_kernel,
        grid_spec=grid_spec,
        out_shape=jax.ShapeDtypeStruct(xs.shape, F32),
        compiler_params=_params(2),
        name="moe_ffn",
    )(tile_expert, tile_full, nvalid, xs, w1, w3, w2)


def _combine_kernel(pos_a_ref, pos_b_ref, h_ref, ga_ref, gb_ref, g_ref, y_ref, o_ref, ya_ref, yb_ref, sem, *, base):
    i = pl.program_id(0)
    rt = ROW_DMA_TILE

    def fetch(step, slot):
        def body(r, carry):
            tok = base + step * rt + r
            _row_copy(y_ref, pos_a_ref[tok], ya_ref.at[slot], r, sem.at[slot]).start()
            _row_copy(y_ref, pos_b_ref[tok], yb_ref.at[slot], r, sem.at[slot]).start()
            return carry
        lax.fori_loop(0, rt, body, 0, unroll=8)

    @pl.when(i == 0)
    def _():
        fetch(0, 0)

    @pl.when(i + 1 < pl.num_programs(0))
    def _():
        fetch(i + 1, (i + 1) % 2)

    slot = i % 2
    pltpu.make_async_copy(y_ref.at[pl.ds(0, rt)], ya_ref.at[slot], sem.at[slot]).wait()
    pltpu.make_async_copy(y_ref.at[pl.ds(0, rt)], yb_ref.at[slot], sem.at[slot]).wait()
    moe = ga_ref[...] * ya_ref[slot] + gb_ref[...] * yb_ref[slot]
    o_ref[...] = _rmsnorm(h_ref[...] + moe, g_ref[...])


def _combine(pos_a, pos_b, h, gate_a, gate_b, g, y, base):
    rt = ROW_DMA_TILE
    m = h.shape[0]
    row = lambda i, *_: (i, 0)
    grid_spec = pltpu.PrefetchScalarGridSpec(
        num_scalar_prefetch=2,
        grid=(m // rt,),
        in_specs=[
            pl.BlockSpec((rt, D_MODEL), row),
            pl.BlockSpec((rt, 1), row),
            pl.BlockSpec((rt, 1), row),
            pl.BlockSpec((1, D_MODEL), lambda i, *_: (0, 0)),
            pl.BlockSpec(memory_space=pl.ANY),
        ],
        out_specs=pl.BlockSpec((rt, D_MODEL), row),
        scratch_shapes=[
            pltpu.VMEM((2, rt, D_MODEL), F32),
            pltpu.VMEM((2, rt, D_MODEL), F32),
            pltpu.SemaphoreType.DMA((2,)),
        ],
    )
    return pl.pallas_call(
        functools.partial(_combine_kernel, base=base),
        grid_spec=grid_spec,
        out_shape=jax.ShapeDtypeStruct((m, D_MODEL), F32),
        compiler_params=_params(1),
        name="moe_combine",
    )(pos_a, pos_b, h, gate_a, gate_b, g, y)


def _route_tables(sel, comb, n_tiles):
    tm = MOE_TILE
    sel = sel.T
    comb = comb.T
    cum = jnp.cumsum(sel, axis=1)
    counts = cum[:, -1]
    padded = ((counts + tm - 1) // tm) * tm
    ends = jnp.cumsum(padded)
    starts = ends - padded
    slot = starts[:, None] + cum - sel
    order = jnp.cumsum(sel, axis=0)
    first = (sel == 1) & (order == 1)
    second = (sel == 1) & (order == 2)
    pos_a = jnp.sum(jnp.where(first, slot, 0), axis=0).astype(jnp.int32)
    pos_b = jnp.sum(jnp.where(second, slot, 0), axis=0).astype(jnp.int32)
    gate_a = jnp.sum(jnp.where(first, comb, 0.0), axis=0)[:, None]
    gate_b = jnp.sum(jnp.where(second, comb, 0.0), axis=0)[:, None]
    tile_start = jnp.arange(n_tiles, dtype=jnp.int32) * tm
    tile_expert = jnp.minimum(jnp.sum(tile_start[:, None] >= ends[None, :], axis=1), N_EXPERTS - 1)
    rows_in_tile = (starts + counts)[tile_expert] - tile_start
    tile_full = (rows_in_tile > tm // 2).astype(jnp.int32)
    nvalid = (ends[-1:] // tm).astype(jnp.int32)
    last_tile = jnp.maximum(ends - tm, 0).astype(jnp.int32)
    has_rows = (counts > 0).astype(jnp.int32)
    return (pos_a, pos_b, gate_a, gate_b, tile_expert.astype(jnp.int32), tile_full, nvalid,
            last_tile, has_rows)


def kernel(x_prompt, x_sample, state_pool, cache_k, cache_v, meta_tokens, g_pool, w_pool, ls_pool, g_ffn0, w_ff1, w_ff3, w_ff2, g_kv, w_kv, b_kv, g_attn, w_q, b_q, sinks, w_o, b_o, g_ffn1, w_router, w_e1, w_e3, w_e2, g_final):
    n_seq, seq, _ = x_prompt.shape
    n_dec, n_t, _ = x_sample.shape
    past_len = PAST_LEN
    window = cache_k.shape[1]
    assert window == WINDOW and seq % POOL_TILE == 0 and N_META <= POOL_TILE

    vec = lambda a: a.reshape(1, -1).astype(F32)
    g_pool, ls_pool, g_ffn0, g_kv, b_kv, g_attn, b_q, b_o, g_ffn1, g_final = map(
        vec, (g_pool, ls_pool, g_ffn0, g_kv, b_kv, g_attn, b_q, b_o, g_ffn1, g_final))
    w_pool, w_ff1, w_ff3, w_ff2, w_kv, w_q, w_o = (
        w.astype(BF16) for w in (w_pool, w_ff1, w_ff3, w_ff2, w_kv, w_q, w_o))
    w_router = jnp.pad(w_router, ((0, 0), (0, LANES - N_EXPERTS))).astype(BF16)
    meta = meta_tokens.astype(F32)

    def layer0_tail(h1, tm_ffn, tm_proj):
        h2 = _ffn(h1, g_ffn0, w_ff1, w_ff3, w_ff2, min(tm_ffn, h1.shape[0]))
        kv, q = _qkv(h2, g_kv, w_kv, b_kv, g_attn, w_q, b_q, tm_proj)
        return h2, kv, q

    x_meta = jnp.pad(meta, ((0, POOL_TILE - N_META), (0, 0)))[None]
    h1_m, _ = _pool_seq(x_meta, jnp.zeros_like(meta), g_pool, w_pool, ls_pool, 0)
    _, kv_m, _ = layer0_tail(h1_m[0], POOL_TILE, POOL_TILE)
    past_kv = jnp.pad(kv_m[:N_META], ((WINDOW - N_META, 0), (0, 0)))

    h1_p, pool_p = _pool_seq(x_prompt, meta, g_pool, w_pool, ls_pool, N_META)
    h2_p, kv_p, q_p = layer0_tail(h1_p.reshape(n_seq * seq, D_MODEL), FFN_TILE, PROJ_TILE)
    o_p = _attn_seq(q_p, kv_p, past_kv, sinks.astype(F32), n_seq, N_META)
    h3_p, u_p, comb_p, sel_p = _oproj(o_p, h2_p, w_o, b_o, g_ffn1, w_router, PROJ_TILE)
    kv_tail = kv_p.reshape(n_seq, seq, 2 * HKV)[:, seq - WINDOW:]
    k_p = kv_tail[..., :HKV].reshape(n_seq, WINDOW, N_KV_HEADS, HEAD_DIM)
    v_p = kv_tail[..., HKV:].reshape(n_seq, WINDOW, N_KV_HEADS, HEAD_DIM)

    h1_s, pool_s = _pool_step(x_sample.transpose(1, 0, 2), state_pool.transpose(1, 0, 2),
                              g_pool, w_pool, ls_pool, past_len)
    n_s = n_t * n_dec
    h2_s, kv_s, q_s = layer0_tail(h1_s.reshape(n_s, D_MODEL), FFN_TILE, PROJ_TILE)
    q5 = q_s.reshape(n_t, n_dec, N_KV_HEADS, GROUP, HEAD_DIM).transpose(1, 2, 0, 3, 4)
    eye = jnp.eye(N_KV_HEADS, dtype=BF16)
    q_bd = (q5[:, :, :, :, None, :] * eye[None, :, None, None, :, None]).reshape(
        n_dec, N_KV_HEADS * n_t * GROUP, HKV)
    new_kv = jnp.pad(kv_s.reshape(n_t, n_dec, 2 * HKV).transpose(1, 0, 2), ((0, 0), (0, 16 - n_t), (0, 0)))
    sink_rows = jnp.broadcast_to(sinks.astype(F32).reshape(N_KV_HEADS, 1, GROUP),
                                 (N_KV_HEADS, n_t, GROUP)).reshape(-1, 1)
    o_s4, k_s, v_s = _attn_step(q_bd, new_kv, cache_k.reshape(n_dec, WINDOW, HKV),
                                cache_v.reshape(n_dec, WINDOW, HKV), sink_rows, past_len, n_t)
    o_s = o_s4.reshape(n_dec, N_KV_HEADS, n_t, GROUP, HEAD_DIM).transpose(2, 0, 1, 3, 4).reshape(n_s, D_MODEL)
    h3_s, u_s, comb_s, sel_s = _oproj(o_s, h2_s, w_o, b_o, g_ffn1, w_router, PROJ_TILE)

    n_tok = n_seq * seq + n_s
    min_tiles = 2 * n_tok // MOE_TILE
    n_tiles = min_tiles + N_EXPERTS
    pos_a, pos_b, gate_a, gate_b, tile_expert, tile_full, nvalid, last_tile, has_rows = _route_tables(
        jnp.concatenate([sel_p, sel_s]), jnp.concatenate([comb_p, comb_s]), n_tiles)
    xs = _dispatch(pos_a, pos_b, last_tile, has_rows, nvalid, u_p, u_s, n_tiles, min_tiles)
    y = _moe(tile_expert, tile_full, nvalid, xs, w_e1.astype(F32), w_e3.astype(F32), w_e2.astype(F32))
    n_p = n_seq * seq
    y_p = _combine(pos_a, pos_b, h3_p, gate_a[:n_p], gate_b[:n_p], g_final, y, 0)
    y_s = _combine(pos_a, pos_b, h3_s, gate_a[n_p:], gate_b[n_p:], g_final, y, n_p)

    return (y_p.reshape(n_seq, seq, D_MODEL),
            y_s.reshape(n_t, n_dec, D_MODEL).transpose(1, 0, 2),
            pool_p,
            pool_s.transpose(1, 0, 2),
            k_p, v_p,
            k_s.reshape(n_dec, WINDOW, N_KV_HEADS, HEAD_DIM),
            v_s.reshape(n_dec, WINDOW, N_KV_HEADS, HEAD_DIM))
```

```python
import functools

import jax
import jax.numpy as jnp
from jax import lax
from jax.experimental import pallas as pl
from jax.experimental.pallas import tpu as pltpu

F32 = jnp.float32
BF16 = jnp.bfloat16

D_MODEL = 2048
N_META = 16
POOL_WINDOWS = (2, 4, 8, 16)
POOL_GROUP_DIM = D_MODEL // len(POOL_WINDOWS)
POOL_STATE = max(POOL_WINDOWS) - 1
HEAD_DIM = 64
N_HEADS = D_MODEL // HEAD_DIM
N_KV_HEADS = 4
GROUP = N_HEADS // N_KV_HEADS
HKV = N_KV_HEADS * HEAD_DIM
WINDOW = 128
N_EXPERTS = 8
EPS = 1e-5
PAST_LEN = 8192
MASKED = -1e30

LANES = 128
V7X_VMEM_BYTES = 64 * 2 ** 20
VMEM_LIMIT = V7X_VMEM_BYTES - 8 * 2 ** 20

POOL_TILE = 128
ATTN_TILE = WINDOW
MOE_TILE = 1024
MOE_FF_TILE = 512
FFN_FF_TILE = 512
FFN_TILE = 1024
PROJ_TILE = 256
ROW_DMA_TILE = 256


def _dot(a, b):
    return jnp.dot(a, b, preferred_element_type=F32)


def _dot_nt(a, b):
    return lax.dot_general(a, b, (((1,), (1,)), ((), ())), preferred_element_type=F32)


def _rmsnorm(x, g):
    return x * lax.rsqrt(jnp.mean(x * x, axis=-1, keepdims=True) + EPS) * g


def _params(n_axes):
    return pltpu.CompilerParams(dimension_semantics=("arbitrary",) * n_axes,
                                vmem_limit_bytes=VMEM_LIMIT)


def _pool_seq_kernel(x_ref, meta_ref, g_ref, w_ref, ls_ref, h_ref, st_ref, e_ref, *, pos0):
    t = pl.program_id(1)
    tt = POOL_TILE
    g = g_ref[...]

    @pl.when(t == 0)
    def _():
        e_ref[0:tt - N_META, :] = jnp.zeros((tt - N_META, D_MODEL), F32)
        e_ref[tt - N_META:tt, :] = _rmsnorm(meta_ref[...], g)

    @pl.when(t > 0)
    def _():
        e_ref[0:tt, :] = e_ref[tt:2 * tt, :]

    x = x_ref[0]
    u = _rmsnorm(x, g)
    e_ref[tt:2 * tt, :] = u

    e = e_ref[...]
    hi = e.astype(BF16)
    lo = (e - hi.astype(F32)).astype(BF16)
    r = lax.broadcasted_iota(jnp.int32, (tt, 2 * tt), 0)
    c = lax.broadcasted_iota(jnp.int32, (tt, 2 * tt), 1)
    pos = pos0 + t * tt + lax.broadcasted_iota(jnp.int32, (tt, 1), 0)
    for gi, w in enumerate(POOL_WINDOWS):
        sl = slice(gi * POOL_GROUP_DIM, (gi + 1) * POOL_GROUP_DIM)
        band = jnp.where((c > r + tt - w) & (c <= r + tt), 1.0, 0.0).astype(BF16)
        win_sum = _dot(band, hi[:, sl]) + _dot(band, lo[:, sl])
        cnt = jnp.minimum(w, pos + 1).astype(F32)
        mix = win_sum / cnt - u[:, sl]
        o = _dot(mix.astype(BF16), w_ref[gi]) * ls_ref[:, sl]
        h_ref[0, :, sl] = x[:, sl] + o

    @pl.when(t == pl.num_programs(1) - 1)
    def _():
        st_ref[0] = e_ref[2 * tt - POOL_STATE:2 * tt, :]


def _pool_seq(x, meta, g, w, ls, pos0):
    b, t, _ = x.shape
    tt = POOL_TILE
    return pl.pallas_call(
        functools.partial(_pool_seq_kernel, pos0=pos0),
        grid=(b, t // tt),
        in_specs=[
            pl.BlockSpec((1, tt, D_MODEL), lambda i, j: (i, j, 0)),
            pl.BlockSpec((N_META, D_MODEL), lambda i, j: (0, 0)),
            pl.BlockSpec((1, D_MODEL), lambda i, j: (0, 0)),
            pl.BlockSpec((len(POOL_WINDOWS), POOL_GROUP_DIM, POOL_GROUP_DIM), lambda i, j: (0, 0, 0)),
            pl.BlockSpec((1, D_MODEL), lambda i, j: (0, 0)),
        ],
        out_specs=[
            pl.BlockSpec((1, tt, D_MODEL), lambda i, j: (i, j, 0)),
            pl.BlockSpec((1, POOL_STATE, D_MODEL), lambda i, j: (i, 0, 0)),
        ],
        out_shape=[
            jax.ShapeDtypeStruct((b, t, D_MODEL), F32),
            jax.ShapeDtypeStruct((b, POOL_STATE, D_MODEL), F32),
        ],
        scratch_shapes=[pltpu.VMEM((2 * tt, D_MODEL), F32)],
        compiler_params=_params(2),
        name="pool_seq",
    )(x, meta, g, w, ls)


def _pool_step_kernel(x_ref, st_ref, g_ref, w_ref, ls_ref, h_ref, nst_ref, e_ref, *, pos0):
    n_t, bb, _ = x_ref.shape
    g = g_ref[...]
    for j in range(POOL_STATE):
        e_ref[j] = st_ref[j]
    for t in range(n_t):
        e_ref[POOL_STATE + t] = _rmsnorm(x_ref[t], g)
    for j in range(POOL_STATE):
        nst_ref[j] = e_ref[j + n_t]
    for gi, w in enumerate(POOL_WINDOWS):
        sl = slice(gi * POOL_GROUP_DIM, (gi + 1) * POOL_GROUP_DIM)
        mixes = []
        for t in range(n_t):
            win_sum = e_ref[POOL_STATE + t, :, sl]
            for i in range(1, w):
                win_sum = win_sum + e_ref[POOL_STATE + t - i, :, sl]
            cnt = float(min(w, pos0 + t + 1))
            mixes.append(win_sum / cnt - e_ref[POOL_STATE + t, :, sl])
        mix = jnp.concatenate(mixes, axis=0)
        o = _dot(mix.astype(BF16), w_ref[gi]) * ls_ref[:, sl]
        for t in range(n_t):
            h_ref[t, :, sl] = x_ref[t, :, sl] + o[t * bb:(t + 1) * bb]


def _pool_step(x_t, st_t, g, w, ls, pos0, bb=32):
    n_t, b, _ = x_t.shape
    return pl.pallas_call(
        functools.partial(_pool_step_kernel, pos0=pos0),
        grid=(b // bb,),
        in_specs=[
            pl.BlockSpec((n_t, bb, D_MODEL), lambda i: (0, i, 0)),
            pl.BlockSpec((POOL_STATE, bb, D_MODEL), lambda i: (0, i, 0)),
            pl.BlockSpec((1, D_MODEL), lambda i: (0, 0)),
            pl.BlockSpec((len(POOL_WINDOWS), POOL_GROUP_DIM, POOL_GROUP_DIM), lambda i: (0, 0, 0)),
            pl.BlockSpec((1, D_MODEL), lambda i: (0, 0)),
        ],
        out_specs=[
            pl.BlockSpec((n_t, bb, D_MODEL), lambda i: (0, i, 0)),
            pl.BlockSpec((POOL_STATE, bb, D_MODEL), lambda i: (0, i, 0)),
        ],
        out_shape=[
            jax.ShapeDtypeStruct((n_t, b, D_MODEL), F32),
            jax.ShapeDtypeStruct((POOL_STATE, b, D_MODEL), F32),
        ],
        scratch_shapes=[pltpu.VMEM((POOL_STATE + n_t, bb, D_MODEL), F32)],
        compiler_params=_params(1),
        name="pool_step",
    )(x_t, st_t, g, w, ls)


def _swiglu_skewed(k, n_chunks, parts, w1, w3, w2, first, last=lambda: None):
    def up():
        for u_ref, h_ref, _ in parts:
            u = u_ref[...]
            h_ref[...] = (jax.nn.silu(_dot(u, w1())) * _dot(u, w3())).astype(h_ref.dtype)

    def down():
        for _, h_ref, acc_ref in parts:
            acc_ref[...] += _dot(h_ref[...], w2())

    @pl.when(k == 0)
    def _():
        first()
        up()

    @pl.when((k > 0) & (k < n_chunks))
    def _():
        down()
        up()

    @pl.when(k == n_chunks)
    def _():
        last()
        down()


def _ffn_kernel(x_ref, g_ref, w1_ref, w3_ref, w2_ref, o_ref, u_ref, h_ref):
    def first():
        x = x_ref[...]
        u_ref[...] = _rmsnorm(x, g_ref[...]).astype(BF16)
        o_ref[...] = x

    _swiglu_skewed(pl.program_id(1), pl.num_programs(1) - 1, [(u_ref, h_ref, o_ref)],
                   lambda: w1_ref[...], lambda: w3_ref[...], lambda: w2_ref[...], first)


def _ffn(x, g, w1, w3, w2, tm):
    m = x.shape[0]
    f = w1.shape[1]
    tf = FFN_FF_TILE
    kf = f // tf
    return pl.pallas_call(
        _ffn_kernel,
        grid=(m // tm, kf + 1),
        in_specs=[
            pl.BlockSpec((tm, D_MODEL), lambda i, k: (i, 0)),
            pl.BlockSpec((1, D_MODEL), lambda i, k: (0, 0)),
            pl.BlockSpec((D_MODEL, tf), lambda i, k: (0, jnp.minimum(k, kf - 1))),
            pl.BlockSpec((D_MODEL, tf), lambda i, k: (0, jnp.minimum(k, kf - 1))),
            pl.BlockSpec((tf, D_MODEL), lambda i, k: (jnp.maximum(k - 1, 0), 0)),
        ],
        out_specs=pl.BlockSpec((tm, D_MODEL), lambda i, k: (i, 0)),
        out_shape=jax.ShapeDtypeStruct((m, D_MODEL), F32),
        scratch_shapes=[pltpu.VMEM((tm, D_MODEL), BF16), pltpu.VMEM((tm, tf), BF16)],
        compiler_params=_params(2),
        name="ffn0",
    )(x, g, w1, w3, w2)


def _qkv_kernel(x_ref, gkv_ref, wkv_ref, bkv_ref, gq_ref, wq_ref, bq_ref, kv_ref, q_ref):
    x = x_ref[...]
    xn = x * lax.rsqrt(jnp.mean(x * x, axis=-1, keepdims=True) + EPS)
    kv_ref[...] = _dot((xn * gkv_ref[...]).astype(BF16), wkv_ref[...]) + bkv_ref[...]
    q = _dot((xn * gq_ref[...]).astype(BF16), wq_ref[...]) + bq_ref[...]
    q_ref[...] = (q * HEAD_DIM ** -0.5).astype(BF16)


def _qkv(x, gkv, wkv, bkv, gq, wq, bq, tm):
    m = x.shape[0]
    row = lambda i: (i, 0)
    fixed = lambda i: (0, 0)
    return pl.pallas_call(
        _qkv_kernel,
        grid=(m // tm,),
        in_specs=[
            pl.BlockSpec((tm, D_MODEL), row),
            pl.BlockSpec((1, D_MODEL), fixed),
            pl.BlockSpec((D_MODEL, 2 * HKV), fixed),
            pl.BlockSpec((1, 2 * HKV), fixed),
            pl.BlockSpec((1, D_MODEL), fixed),
            pl.BlockSpec((D_MODEL, D_MODEL), fixed),
            pl.BlockSpec((1, D_MODEL), fixed),
        ],
        out_specs=[pl.BlockSpec((tm, 2 * HKV), row), pl.BlockSpec((tm, D_MODEL), row)],
        out_shape=[jax.ShapeDtypeStruct((m, 2 * HKV), F32), jax.ShapeDtypeStruct((m, D_MODEL), BF16)],
        compiler_params=_params(1),
        name="qkv",
    )(x, gkv, wkv, bkv, gq, wq, bq)


def _attn_seq_kernel(sinks_ref, q_ref, cur_ref, prev_ref, past_ref, o_ref, *, pos0):
    j = pl.program_id(1)
    tq = ATTN_TILE
    prev = jnp.where(j == 0, past_ref[...], prev_ref[...])
    band = jnp.concatenate([prev, cur_ref[...]], axis=0)
    c = lax.broadcasted_iota(jnp.int32, (2 * tq, tq), 0)
    r = lax.broadcasted_iota(jnp.int32, (2 * tq, tq), 1)
    key_pos = pos0 + (j - 1) * tq + c
    valid = (c > r) & (c <= r + tq) & (key_pos >= 0)
    low = lax.broadcasted_iota(jnp.int32, (2 * tq, LANES), 1) < HEAD_DIM
    zeros = jnp.zeros((2 * tq, LANES), F32)
    k_bd, v_bd_t = [], []
    v_t = band[:, HKV:].T
    zv = jnp.zeros((HEAD_DIM, 2 * tq), F32)
    for kvh in range(N_KV_HEADS):
        tile = band[:, (kvh // 2) * LANES:(kvh // 2 + 1) * LANES]
        swapped = pltpu.roll(tile, HEAD_DIM, axis=1)
        in_low, in_high = (tile, swapped) if kvh % 2 == 0 else (swapped, tile)
        k_bd.append(jnp.concatenate([jnp.where(low, in_low, zeros), jnp.where(low, zeros, in_high)],
                                    axis=0).astype(BF16))
        vt = v_t[kvh * HEAD_DIM:(kvh + 1) * HEAD_DIM]
        v_bd_t.append(jnp.concatenate([jnp.concatenate([vt, zv], axis=1),
                                       jnp.concatenate([zv, vt], axis=1)], axis=0).astype(BF16))
    for pair in range(N_HEADS // 2):
        kvh = (2 * pair) // GROUP
        ps = slice(pair * LANES, (pair + 1) * LANES)
        s_t = _dot_nt(k_bd[kvh], q_ref[:, ps])
        probs, inv = [], []
        for half in range(2):
            s = jnp.where(valid, s_t[half * 2 * tq:(half + 1) * 2 * tq], MASKED)
            sink = sinks_ref[2 * pair + half]
            m = jnp.maximum(jnp.max(s, axis=0, keepdims=True), sink)
            p = jnp.exp(s - m)
            denom = jnp.sum(p, axis=0, keepdims=True) + jnp.exp(sink - m)
            probs.append(p.astype(BF16))
            inv.append(jnp.broadcast_to(1.0 / denom, (HEAD_DIM, tq)))
        o_t = _dot(v_bd_t[kvh], jnp.concatenate(probs, axis=0)) * jnp.concatenate(inv, axis=0)
        o_ref[:, ps] = o_t.T.astype(BF16)


def _attn_seq(q, kv, past_kv, sinks, n_seq, pos0):
    m = q.shape[0]
    tq = ATTN_TILE
    nb = m // n_seq // tq
    return pl.pallas_call(
        functools.partial(_attn_seq_kernel, pos0=pos0),
        grid=(n_seq, nb),
        in_specs=[
            pl.BlockSpec(memory_space=pltpu.SMEM),
            pl.BlockSpec((tq, D_MODEL), lambda b, j: (b * nb + j, 0)),
            pl.BlockSpec((tq, 2 * HKV), lambda b, j: (b * nb + j, 0)),
            pl.BlockSpec((tq, 2 * HKV), lambda b, j: (b * nb + jnp.maximum(j - 1, 0), 0)),
            pl.BlockSpec((tq, 2 * HKV), lambda b, j: (0, 0)),
        ],
        out_specs=pl.BlockSpec((tq, D_MODEL), lambda b, j: (b * nb + j, 0)),
        out_shape=jax.ShapeDtypeStruct((m, D_MODEL), BF16),
        compiler_params=_params(2),
        name="attn_seq",
    )(sinks, q, kv, kv, past_kv)


def _attn_step_kernel(q_ref, new_ref, ck_ref, cv_ref, sink_ref, o_ref, nk_ref, nv_ref, *, pos0, n_t):
    bb = q_ref.shape[0]
    rows = N_KV_HEADS * n_t * GROUP
    pad = WINDOW - new_ref.shape[1]
    rho = lax.broadcasted_iota(jnp.int32, (rows, 2 * WINDOW), 0)
    c = lax.broadcasted_iota(jnp.int32, (rows, 2 * WINDOW), 1)
    t = (rho // GROUP) % n_t
    valid = (c > t) & (c <= t + WINDOW) & (pos0 - WINDOW + c >= 0)
    sink = sink_ref[...]
    for i in range(bb):
        new = jnp.concatenate([new_ref[i], jnp.zeros((pad, 2 * HKV), F32)], axis=0)
        kcat = jnp.concatenate([ck_ref[i], new[:, :HKV]], axis=0).astype(BF16)
        vcat = jnp.concatenate([cv_ref[i], new[:, HKV:]], axis=0).astype(BF16)
        s = jnp.where(valid, _dot_nt(q_ref[i], kcat), MASKED)
        m = jnp.maximum(jnp.max(s, axis=-1, keepdims=True), sink)
        p = jnp.exp(s - m)
        denom = jnp.sum(p, axis=-1, keepdims=True) + jnp.exp(sink - m)
        o = (_dot(p.astype(BF16), vcat) / denom).astype(BF16)
        for kvh in range(N_KV_HEADS):
            rs = n_t * GROUP
            o_ref[i, kvh] = o[kvh * rs:(kvh + 1) * rs, kvh * HEAD_DIM:(kvh + 1) * HEAD_DIM]
        nk_ref[i, 0:WINDOW - n_t, :] = ck_ref[i, n_t:WINDOW, :]
        nk_ref[i, WINDOW - n_t:WINDOW, :] = new_ref[i, 0:n_t, 0:HKV]
        nv_ref[i, 0:WINDOW - n_t, :] = cv_ref[i, n_t:WINDOW, :]
        nv_ref[i, WINDOW - n_t:WINDOW, :] = new_ref[i, 0:n_t, HKV:2 * HKV]


def _attn_step(q_bd, new_kv, cache_k, cache_v, sink_rows, pos0, n_t, bb=8):
    b = q_bd.shape[0]
    rows = q_bd.shape[1]
    n_new = new_kv.shape[1]
    blk = lambda *s: pl.BlockSpec((bb,) + s, lambda i: (i,) + (0,) * len(s))
    return pl.pallas_call(
        functools.partial(_attn_step_kernel, pos0=pos0, n_t=n_t),
        grid=(b // bb,),
        in_specs=[
            blk(rows, HKV),
            blk(n_new, 2 * HKV),
            blk(WINDOW, HKV),
            blk(WINDOW, HKV),
            pl.BlockSpec((rows, 1), lambda i: (0, 0)),
        ],
        out_specs=[blk(N_KV_HEADS, n_t * GROUP, HEAD_DIM), blk(WINDOW, HKV), blk(WINDOW, HKV)],
        out_shape=[
            jax.ShapeDtypeStruct((b, N_KV_HEADS, n_t * GROUP, HEAD_DIM), BF16),
            jax.ShapeDtypeStruct((b, WINDOW, HKV), F32),
            jax.ShapeDtypeStruct((b, WINDOW, HKV), F32),
        ],
        compiler_params=_params(1),
        name="attn_step",
    )(q_bd, new_kv, cache_k, cache_v, sink_rows)


def _oproj_kernel(o_ref, h_ref, wo_ref, bo_ref, g_ref, wr_ref, h3_ref, u_ref, comb_ref, sel_ref):
    h3 = h_ref[...] + (_dot(o_ref[...], wo_ref[...]) + bo_ref[...])
    h3_ref[...] = h3
    u = _rmsnorm(h3, g_ref[...])
    u_ref[...] = u
    logits = _dot(u.astype(BF16), wr_ref[...])
    lane = lax.broadcasted_iota(jnp.int32, logits.shape, 1)
    lg = jnp.where(lane < N_EXPERTS, logits, -jnp.inf)
    m1 = jnp.max(lg, axis=-1, keepdims=True)
    i1 = jnp.min(jnp.where(lg == m1, lane, LANES), axis=-1, keepdims=True)
    top1 = lane == i1
    lg2 = jnp.where(top1, -jnp.inf, lg)
    m2 = jnp.max(lg2, axis=-1, keepdims=True)
    i2 = jnp.min(jnp.where(lg2 == m2, lane, LANES), axis=-1, keepdims=True)
    top2 = lane == i2
    e2 = jnp.exp(m2 - m1)
    denom = 1.0 + e2
    comb = jnp.where(top1, 1.0 / denom, 0.0) + jnp.where(top2, e2 / denom, 0.0)
    comb_ref[...] = comb[:, :N_EXPERTS]
    sel_ref[...] = jnp.where(top1 | top2, 1, 0).astype(jnp.int32)[:, :N_EXPERTS]


def _oproj(o, h, wo, bo, g, wr, tm):
    m = o.shape[0]
    row = lambda i: (i, 0)
    fixed = lambda i: (0, 0)
    return pl.pallas_call(
        _oproj_kernel,
        grid=(m // tm,),
        in_specs=[
            pl.BlockSpec((tm, D_MODEL), row),
            pl.BlockSpec((tm, D_MODEL), row),
            pl.BlockSpec((D_MODEL, D_MODEL), fixed),
            pl.BlockSpec((1, D_MODEL), fixed),
            pl.BlockSpec((1, D_MODEL), fixed),
            pl.BlockSpec((D_MODEL, LANES), fixed),
        ],
        out_specs=[
            pl.BlockSpec((tm, D_MODEL), row),
            pl.BlockSpec((tm, D_MODEL), row),
            pl.BlockSpec((tm, N_EXPERTS), row),
            pl.BlockSpec((tm, N_EXPERTS), row),
        ],
        out_shape=[
            jax.ShapeDtypeStruct((m, D_MODEL), F32),
            jax.ShapeDtypeStruct((m, D_MODEL), F32),
            jax.ShapeDtypeStruct((m, N_EXPERTS), F32),
            jax.ShapeDtypeStruct((m, N_EXPERTS), jnp.int32),
        ],
        compiler_params=_params(1),
        name="oproj_router",
    )(o, h, wo, bo, g, wr)


def _row_copy(src, r, dst, p, sem):
    return pltpu.make_async_copy(src.at[pl.ds(r, 1)], dst.at[pl.ds(p, 1)], sem)


def _zero_fill_copies(zeros_ref, xs_ref, last_tile_ref, has_rows_ref, nvalid_ref, sem, n_tiles, min_tiles):
    tm = MOE_TILE
    pairs = []
    for e in range(N_EXPERTS):
        start = pl.multiple_of(last_tile_ref[e], tm)
        cp = pltpu.make_async_copy(zeros_ref, xs_ref.at[pl.ds(start, tm)], sem)
        pairs.append((has_rows_ref[e] != 0, cp))
    for tile in range(min_tiles, n_tiles):
        cp = pltpu.make_async_copy(zeros_ref, xs_ref.at[pl.ds(tile * tm, tm)], sem)
        pairs.append((tile >= nvalid_ref[0], cp))
    return pairs


def _dispatch_kernel(pos_a_ref, pos_b_ref, last_tile_ref, has_rows_ref, nvalid_ref,
                     up_ref, us_ref, xs_ref, zeros_ref, sem, zsem, *, n_blocks_p, n_tiles, min_tiles):
    i = pl.program_id(0)
    rt = ROW_DMA_TILE

    @pl.when(i == 0)
    def _():
        zeros_ref[...] = jnp.zeros(zeros_ref.shape, F32)
        pairs = _zero_fill_copies(zeros_ref, xs_ref, last_tile_ref, has_rows_ref, nvalid_ref,
                                  zsem, n_tiles, min_tiles)
        for cond, cp in pairs:
            @pl.when(cond)
            def _():
                cp.start()
        for cond, cp in pairs:
            @pl.when(cond)
            def _():
                cp.wait()

    def scatter(src_ref):
        def body(r, carry):
            tok = i * rt + r
            _row_copy(src_ref, r, xs_ref, pos_a_ref[tok], sem).start()
            _row_copy(src_ref, r, xs_ref, pos_b_ref[tok], sem).start()
            return carry
        lax.fori_loop(0, rt, body, 0, unroll=8)
        for _ in range(2):
            pltpu.make_async_copy(src_ref, xs_ref.at[pl.ds(0, rt)], sem).wait()

    @pl.when(i < n_blocks_p)
    def _():
        scatter(up_ref)

    @pl.when(i >= n_blocks_p)
    def _():
        scatter(us_ref)


def _dispatch(pos_a, pos_b, last_tile, has_rows, nvalid, u_p, u_s, n_tiles, min_tiles):
    rt = ROW_DMA_TILE
    nbp = u_p.shape[0] // rt
    nbs = u_s.shape[0] // rt
    grid_spec = pltpu.PrefetchScalarGridSpec(
        num_scalar_prefetch=5,
        grid=(nbp + nbs,),
        in_specs=[
            pl.BlockSpec((rt, D_MODEL), lambda i, *_: (jnp.minimum(i, nbp - 1), 0)),
            pl.BlockSpec((rt, D_MODEL), lambda i, *_: (jnp.maximum(i - nbp, 0), 0)),
        ],
        out_specs=pl.BlockSpec(memory_space=pl.ANY),
        scratch_shapes=[
            pltpu.VMEM((MOE_TILE, D_MODEL), F32),
            pltpu.SemaphoreType.DMA,
            pltpu.SemaphoreType.DMA,
        ],
    )
    return pl.pallas_call(
        functools.partial(_dispatch_kernel, n_blocks_p=nbp, n_tiles=n_tiles, min_tiles=min_tiles),
        grid_spec=grid_spec,
        out_shape=jax.ShapeDtypeStruct((n_tiles * MOE_TILE, D_MODEL), F32),
        compiler_params=_params(1),
        name="moe_dispatch",
    )(pos_a, pos_b, last_tile, has_rows, nvalid, u_p, u_s)


def _moe_kernel(te_ref, full_ref, nv_ref, xs_ref, w1_ref, w3_ref, w2_ref, y_ref, x_ref, ha_ref, hb_ref, sem):
    i = pl.program_id(0)
    k = pl.program_id(1)
    tm = MOE_TILE
    half = tm // 2
    valid = i < nv_ref[0]
    full = full_ref[i] != 0
    n_chunks = pl.num_programs(1) - 1
    weights = (lambda: w1_ref[0], lambda: w3_ref[0], lambda: w2_ref[0])
    part_a = (x_ref.at[pl.ds(0, half)], ha_ref, y_ref.at[pl.ds(0, half)])
    part_b = (x_ref.at[pl.ds(half, half)], hb_ref, y_ref.at[pl.ds(half, half)])

    def fetch(tile):
        return pltpu.make_async_copy(xs_ref.at[pl.ds(pl.multiple_of(tile * tm, tm), tm)], x_ref, sem)

    def zero_tile():
        y_ref[...] = jnp.zeros(y_ref.shape, F32)

    def first():
        zero_tile()

        @pl.when(i == 0)
        def _():
            fetch(0).start()

        fetch(i).wait()

    def last():
        @pl.when(i + 1 < nv_ref[0])
        def _():
            fetch(i + 1).start()

    @pl.when(valid & full)
    def _():
        _swiglu_skewed(k, n_chunks, [part_a, part_b], *weights, first, last)

    @pl.when(valid & jnp.logical_not(full))
    def _():
        _swiglu_skewed(k, n_chunks, [part_a], *weights, first, last)

    @pl.when(jnp.logical_not(valid) & (k == 0))
    def _():
        zero_tile()


def _moe(tile_expert, tile_full, nvalid, xs, w1, w3, w2):
    tm, tf = MOE_TILE, MOE_FF_TILE
    n_tiles = xs.shape[0] // tm
    kf = w1.shape[2] // tf

    def out_row(i, k, te, full, nv):
        return (i, 0)

    def up(i, k, te, full, nv):
        return (te[jnp.minimum(i, nv[0] - 1)], 0, jnp.where(i < nv[0], jnp.minimum(k, kf - 1), kf - 1))

    def down(i, k, te, full, nv):
        return (te[jnp.minimum(i, nv[0] - 1)], jnp.where(i < nv[0], jnp.maximum(k - 1, 0), kf - 1), 0)

    grid_spec = pltpu.PrefetchScalarGridSpec(
        num_scalar_prefetch=3,
        grid=(n_tiles, kf + 1),
        in_specs=[
            pl.BlockSpec(memory_space=pl.ANY),
            pl.BlockSpec((1, D_MODEL, tf), up),
            pl.BlockSpec((1, D_MODEL, tf), up),
            pl.BlockSpec((1, tf, D_MODEL), down),
        ],
        out_specs=pl.BlockSpec((tm, D_MODEL), out_row),
        scratch_shapes=[
            pltpu.VMEM((tm, D_MODEL), F32),
            pltpu.VMEM((tm // 2, tf), F32),
            pltpu.VMEM((tm // 2, tf), F32),
            pltpu.SemaphoreType.DMA,
        ],
    )
    return pl.pallas_call(
        _moe_kernel,
        grid_spec=grid_spec,
        out_shape=jax.ShapeDtypeStruct(xs.shape, F32),
        compiler_params=_params(2),
        name="moe_ffn",
    )(tile_expert, tile_full, nvalid, xs, w1, w3, w2)


def _combine_kernel(pos_a_ref, pos_b_ref, h_ref, ga_ref, gb_ref, g_ref, y_ref, o_ref, ya_ref, yb_ref, sem, *, base):
    i = pl.program_id(0)
    rt = ROW_DMA_TILE

    def fetch(step, slot):
        def body(r, carry):
            tok = base + step * rt + r
            _row_copy(y_ref, pos_a_ref[tok], ya_ref.at[slot], r, sem.at[slot]).start()
            _row_copy(y_ref, pos_b_ref[tok], yb_ref.at[slot], r, sem.at[slot]).start()
            return carry
        lax.fori_loop(0, rt, body, 0, unroll=8)

    @pl.when(i == 0)
    def _():
        fetch(0, 0)

    @pl.when(i + 1 < pl.num_programs(0))
    def _():
        fetch(i + 1, (i + 1) % 2)

    slot = i % 2
    pltpu.make_async_copy(y_ref.at[pl.ds(0, rt)], ya_ref.at[slot], sem.at[slot]).wait()
    pltpu.make_async_copy(y_ref.at[pl.ds(0, rt)], yb_ref.at[slot], sem.at[slot]).wait()
    moe = ga_ref[...] * ya_ref[slot] + gb_ref[...] * yb_ref[slot]
    o_ref[...] = _rmsnorm(h_ref[...] + moe, g_ref[...])


def _combine(pos_a, pos_b, h, gate_a, gate_b, g, y, base):
    rt = ROW_DMA_TILE
    m = h.shape[0]
    row = lambda i, *_: (i, 0)
    grid_spec = pltpu.PrefetchScalarGridSpec(
        num_scalar_prefetch=2,
        grid=(m // rt,),
        in_specs=[
            pl.BlockSpec((rt, D_MODEL), row),
            pl.BlockSpec((rt, 1), row),
            pl.BlockSpec((rt, 1), row),
            pl.BlockSpec((1, D_MODEL), lambda i, *_: (0, 0)),
            pl.BlockSpec(memory_space=pl.ANY),
        ],
        out_specs=pl.BlockSpec((rt, D_MODEL), row),
        scratch_shapes=[
            pltpu.VMEM((2, rt, D_MODEL), F32),
            pltpu.VMEM((2, rt, D_MODEL), F32),
            pltpu.SemaphoreType.DMA((2,)),
        ],
    )
    return pl.pallas_call(
        functools.partial(_combine_kernel, base=base),
        grid_spec=grid_spec,
        out_shape=jax.ShapeDtypeStruct((m, D_MODEL), F32),
        compiler_params=_params(1),
        name="moe_combine",
    )(pos_a, pos_b, h, gate_a, gate_b, g, y)


def _token_cumsum(sel):
    n_e, n_tok = sel.shape
    blocks = n_tok // LANES
    s = sel.astype(F32).reshape(n_e, blocks, LANES)
    idx = jnp.arange(LANES)
    within = jnp.einsum("ebl,lm->ebm", s, (idx[:, None] <= idx[None, :]).astype(F32),
                        precision=lax.Precision.HIGHEST)
    totals = within[:, :, -1]
    bidx = jnp.arange(blocks)
    offsets = jnp.einsum("eb,bc->ec", totals, (bidx[:, None] < bidx[None, :]).astype(F32),
                         precision=lax.Precision.HIGHEST)
    return (within + offsets[:, :, None]).astype(jnp.int32).reshape(n_e, n_tok)


def _route_tables(sel, comb, n_tiles):
    tm = MOE_TILE
    sel = sel.T
    comb = comb.T
    cum = _token_cumsum(sel)
    counts = cum[:, -1]
    padded = ((counts + tm - 1) // tm) * tm
    ends = jnp.cumsum(padded)
    starts = ends - padded
    slot = starts[:, None] + cum - sel
    order = jnp.cumsum(sel, axis=0)
    first = (sel == 1) & (order == 1)
    second = (sel == 1) & (order == 2)
    pos_a = jnp.sum(jnp.where(first, slot, 0), axis=0).astype(jnp.int32)
    pos_b = jnp.sum(jnp.where(second, slot, 0), axis=0).astype(jnp.int32)
    gate_a = jnp.sum(jnp.where(first, comb, 0.0), axis=0)[:, None]
    gate_b = jnp.sum(jnp.where(second, comb, 0.0), axis=0)[:, None]
    tile_start = jnp.arange(n_tiles, dtype=jnp.int32) * tm
    tile_expert = jnp.minimum(jnp.sum(tile_start[:, None] >= ends[None, :], axis=1), N_EXPERTS - 1)
    rows_in_tile = (starts + counts)[tile_expert] - tile_start
    tile_full = (rows_in_tile > tm // 2).astype(jnp.int32)
    nvalid = (ends[-1:] // tm).astype(jnp.int32)
    last_tile = jnp.maximum(ends - tm, 0).astype(jnp.int32)
    has_rows = (counts > 0).astype(jnp.int32)
    return (pos_a, pos_b, gate_a, gate_b, tile_expert.astype(jnp.int32), tile_full, nvalid,
            last_tile, has_rows)


def kernel(x_prompt, x_sample, state_pool, cache_k, cache_v, meta_tokens, g_pool, w_pool, ls_pool, g_ffn0, w_ff1, w_ff3, w_ff2, g_kv, w_kv, b_kv, g_attn, w_q, b_q, sinks, w_o, b_o, g_ffn1, w_router, w_e1, w_e3, w_e2, g_final):
    n_seq, seq, _ = x_prompt.shape
    n_dec, n_t, _ = x_sample.shape
    past_len = PAST_LEN
    window = cache_k.shape[1]
    assert window == WINDOW and seq % POOL_TILE == 0 and N_META <= POOL_TILE

    vec = lambda a: a.reshape(1, -1).astype(F32)
    g_pool, ls_pool, g_ffn0, g_kv, b_kv, g_attn, b_q, b_o, g_ffn1, g_final = map(
        vec, (g_pool, ls_pool, g_ffn0, g_kv, b_kv, g_attn, b_q, b_o, g_ffn1, g_final))
    w_pool, w_ff1, w_ff3, w_ff2, w_kv, w_q, w_o = (
        w.astype(BF16) for w in (w_pool, w_ff1, w_ff3, w_ff2, w_kv, w_q, w_o))
    w_router = jnp.pad(w_router, ((0, 0), (0, LANES - N_EXPERTS))).astype(BF16)
    meta = meta_tokens.astype(F32)

    def layer0_tail(h1, tm_ffn, tm_proj):
        h2 = _ffn(h1, g_ffn0, w_ff1, w_ff3, w_ff2, min(tm_ffn, h1.shape[0]))
        kv, q = _qkv(h2, g_kv, w_kv, b_kv, g_attn, w_q, b_q, tm_proj)
        return h2, kv, q

    x_meta = jnp.pad(meta, ((0, POOL_TILE - N_META), (0, 0)))[None]
    h1_m, _ = _pool_seq(x_meta, jnp.zeros_like(meta), g_pool, w_pool, ls_pool, 0)
    _, kv_m, _ = layer0_tail(h1_m[0], POOL_TILE, POOL_TILE)
    past_kv = jnp.pad(kv_m[:N_META], ((WINDOW - N_META, 0), (0, 0)))

    h1_p, pool_p = _pool_seq(x_prompt, meta, g_pool, w_pool, ls_pool, N_META)
    h2_p, kv_p, q_p = layer0_tail(h1_p.reshape(n_seq * seq, D_MODEL), FFN_TILE, PROJ_TILE)
    o_p = _attn_seq(q_p, kv_p, past_kv, sinks.astype(F32), n_seq, N_META)
    h3_p, u_p, comb_p, sel_p = _oproj(o_p, h2_p, w_o, b_o, g_ffn1, w_router, PROJ_TILE)
    kv_tail = kv_p.reshape(n_seq, seq, 2 * HKV)[:, seq - WINDOW:]
    k_p = kv_tail[..., :HKV].reshape(n_seq, WINDOW, N_KV_HEADS, HEAD_DIM)
    v_p = kv_tail[..., HKV:].reshape(n_seq, WINDOW, N_KV_HEADS, HEAD_DIM)

    h1_s, pool_s = _pool_step(x_sample.transpose(1, 0, 2), state_pool.transpose(1, 0, 2),
                              g_pool, w_pool, ls_pool, past_len)
    n_s = n_t * n_dec
    h2_s, kv_s, q_s = layer0_tail(h1_s.reshape(n_s, D_MODEL), FFN_TILE, PROJ_TILE)
    q5 = q_s.reshape(n_t, n_dec, N_KV_HEADS, GROUP, HEAD_DIM).transpose(1, 2, 0, 3, 4)
    eye = jnp.eye(N_KV_HEADS, dtype=BF16)
    q_bd = (q5[:, :, :, :, None, :] * eye[None, :, None, None, :, None]).reshape(
        n_dec, N_KV_HEADS * n_t * GROUP, HKV)
    new_kv = jnp.pad(kv_s.reshape(n_t, n_dec, 2 * HKV).transpose(1, 0, 2), ((0, 0), (0, 16 - n_t), (0, 0)))
    sink_rows = jnp.broadcast_to(sinks.astype(F32).reshape(N_KV_HEADS, 1, GROUP),
                                 (N_KV_HEADS, n_t, GROUP)).reshape(-1, 1)
    o_s4, k_s, v_s = _attn_step(q_bd, new_kv, cache_k.reshape(n_dec, WINDOW, HKV),
                                cache_v.reshape(n_dec, WINDOW, HKV), sink_rows, past_len, n_t)
    o_s = o_s4.reshape(n_dec, N_KV_HEADS, n_t, GROUP, HEAD_DIM).transpose(2, 0, 1, 3, 4).reshape(n_s, D_MODEL)
    h3_s, u_s, comb_s, sel_s = _oproj(o_s, h2_s, w_o, b_o, g_ffn1, w_router, PROJ_TILE)

    n_tok = n_seq * seq + n_s
    min_tiles = 2 * n_tok // MOE_TILE
    n_tiles = min_tiles + N_EXPERTS
    pos_a, pos_b, gate_a, gate_b, tile_expert, tile_full, nvalid, last_tile, has_rows = _route_tables(
        jnp.concatenate([sel_p, sel_s]), jnp.concatenate([comb_p, comb_s]), n_tiles)
    xs = _dispatch(pos_a, pos_b, last_tile, has_rows, nvalid, u_p, u_s, n_tiles, min_tiles)
    y = _moe(tile_expert, tile_full, nvalid, xs, w_e1.astype(F32), w_e3.astype(F32), w_e2.astype(F32))
    n_p = n_seq * seq
    y_p = _combine(pos_a, pos_b, h3_p, gate_a[:n_p], gate_b[:n_p], g_final, y, 0)
    y_s = _combine(pos_a, pos_b, h3_s, gate_a[n_p:], gate_b[n_p:], g_final, y, n_p)

    return (y_p.reshape(n_seq, seq, D_MODEL),
            y_s.reshape(n_t, n_dec, D_MODEL).transpose(1, 0, 2),
            pool_p,
            pool_s.transpose(1, 0, 2),
            k_p, v_p,
            k_s.reshape(n_dec, WINDOW, N_KV_HEADS, HEAD_DIM),
            v_s.reshape(n_dec, WINDOW, N_KV_HEADS, HEAD_DIM))
```

```python
import functools

import jax
import jax.numpy as jnp
from jax import lax
from jax.experimental import pallas as pl
from jax.experimental.pallas import tpu as pltpu

F32 = jnp.float32
BF16 = jnp.bfloat16

D_MODEL = 2048
N_META = 16
POOL_WINDOWS = (2, 4, 8, 16)
POOL_GROUP_DIM = D_MODEL // len(POOL_WINDOWS)
POOL_STATE = max(POOL_WINDOWS) - 1
HEAD_DIM = 64
N_HEADS = D_MODEL // HEAD_DIM
N_KV_HEADS = 4
GROUP = N_HEADS // N_KV_HEADS
HKV = N_KV_HEADS * HEAD_DIM
WINDOW = 128
N_EXPERTS = 8
EPS = 1e-5
PAST_LEN = 8192
MASKED = -1e30

LANES = 128
V7X_VMEM_BYTES = 64 * 2 ** 20
VMEM_LIMIT = V7X_VMEM_BYTES - 8 * 2 ** 20

POOL_TILE = 128
ATTN_TILE = WINDOW
MOE_TILE = 1024
MOE_FF_TILE = 256
FFN_FF_TILE = 512
FFN_TILE = 1024
PROJ_TILE = 256
ROW_DMA_TILE = 256


def _dot(a, b):
    return jnp.dot(a, b, preferred_element_type=F32)


def _dot_nt(a, b):
    return lax.dot_general(a, b, (((1,), (1,)), ((), ())), preferred_element_type=F32)


def _rmsnorm(x, g):
    return x * lax.rsqrt(jnp.mean(x * x, axis=-1, keepdims=True) + EPS) * g


def _params(n_axes):
    return pltpu.CompilerParams(dimension_semantics=("arbitrary",) * n_axes,
                                vmem_limit_bytes=VMEM_LIMIT)


def _pool_seq_kernel(x_ref, meta_ref, g_ref, w_ref, ls_ref, h_ref, st_ref, e_ref, *, pos0):
    t = pl.program_id(1)
    tt = POOL_TILE
    g = g_ref[...]

    @pl.when(t == 0)
    def _():
        e_ref[0:tt - N_META, :] = jnp.zeros((tt - N_META, D_MODEL), F32)
        e_ref[tt - N_META:tt, :] = _rmsnorm(meta_ref[...], g)

    @pl.when(t > 0)
    def _():
        e_ref[0:tt, :] = e_ref[tt:2 * tt, :]

    x = x_ref[0]
    u = _rmsnorm(x, g)
    e_ref[tt:2 * tt, :] = u

    e = e_ref[...]
    hi = e.astype(BF16)
    lo = (e - hi.astype(F32)).astype(BF16)
    r = lax.broadcasted_iota(jnp.int32, (tt, 2 * tt), 0)
    c = lax.broadcasted_iota(jnp.int32, (tt, 2 * tt), 1)
    pos = pos0 + t * tt + lax.broadcasted_iota(jnp.int32, (tt, 1), 0)
    for gi, w in enumerate(POOL_WINDOWS):
        sl = slice(gi * POOL_GROUP_DIM, (gi + 1) * POOL_GROUP_DIM)
        band = jnp.where((c > r + tt - w) & (c <= r + tt), 1.0, 0.0).astype(BF16)
        win_sum = _dot(band, hi[:, sl]) + _dot(band, lo[:, sl])
        cnt = jnp.minimum(w, pos + 1).astype(F32)
        mix = win_sum / cnt - u[:, sl]
        o = _dot(mix.astype(BF16), w_ref[gi]) * ls_ref[:, sl]
        h_ref[0, :, sl] = x[:, sl] + o

    @pl.when(t == pl.num_programs(1) - 1)
    def _():
        st_ref[0] = e_ref[2 * tt - POOL_STATE:2 * tt, :]


def _pool_seq(x, meta, g, w, ls, pos0):
    b, t, _ = x.shape
    tt = POOL_TILE
    return pl.pallas_call(
        functools.partial(_pool_seq_kernel, pos0=pos0),
        grid=(b, t // tt),
        in_specs=[
            pl.BlockSpec((1, tt, D_MODEL), lambda i, j: (i, j, 0)),
            pl.BlockSpec((N_META, D_MODEL), lambda i, j: (0, 0)),
            pl.BlockSpec((1, D_MODEL), lambda i, j: (0, 0)),
            pl.BlockSpec((len(POOL_WINDOWS), POOL_GROUP_DIM, POOL_GROUP_DIM), lambda i, j: (0, 0, 0)),
            pl.BlockSpec((1, D_MODEL), lambda i, j: (0, 0)),
        ],
        out_specs=[
            pl.BlockSpec((1, tt, D_MODEL), lambda i, j: (i, j, 0)),
            pl.BlockSpec((1, POOL_STATE, D_MODEL), lambda i, j: (i, 0, 0)),
        ],
        out_shape=[
            jax.ShapeDtypeStruct((b, t, D_MODEL), F32),
            jax.ShapeDtypeStruct((b, POOL_STATE, D_MODEL), F32),
        ],
        scratch_shapes=[pltpu.VMEM((2 * tt, D_MODEL), F32)],
        compiler_params=_params(2),
        name="pool_seq",
    )(x, meta, g, w, ls)


def _pool_step_kernel(x_ref, st_ref, g_ref, w_ref, ls_ref, h_ref, nst_ref, e_ref, *, pos0):
    n_t, bb, _ = x_ref.shape
    g = g_ref[...]
    for j in range(POOL_STATE):
        e_ref[j] = st_ref[j]
    for t in range(n_t):
        e_ref[POOL_STATE + t] = _rmsnorm(x_ref[t], g)
    for j in range(POOL_STATE):
        nst_ref[j] = e_ref[j + n_t]
    for gi, w in enumerate(POOL_WINDOWS):
        sl = slice(gi * POOL_GROUP_DIM, (gi + 1) * POOL_GROUP_DIM)
        mixes = []
        for t in range(n_t):
            win_sum = e_ref[POOL_STATE + t, :, sl]
            for i in range(1, w):
                win_sum = win_sum + e_ref[POOL_STATE + t - i, :, sl]
            cnt = float(min(w, pos0 + t + 1))
            mixes.append(win_sum / cnt - e_ref[POOL_STATE + t, :, sl])
        mix = jnp.concatenate(mixes, axis=0)
        o = _dot(mix.astype(BF16), w_ref[gi]) * ls_ref[:, sl]
        for t in range(n_t):
            h_ref[t, :, sl] = x_ref[t, :, sl] + o[t * bb:(t + 1) * bb]


def _pool_step(x_t, st_t, g, w, ls, pos0, bb=32):
    n_t, b, _ = x_t.shape
    return pl.pallas_call(
        functools.partial(_pool_step_kernel, pos0=pos0),
        grid=(b // bb,),
        in_specs=[
            pl.BlockSpec((n_t, bb, D_MODEL), lambda i: (0, i, 0)),
            pl.BlockSpec((POOL_STATE, bb, D_MODEL), lambda i: (0, i, 0)),
            pl.BlockSpec((1, D_MODEL), lambda i: (0, 0)),
            pl.BlockSpec((len(POOL_WINDOWS), POOL_GROUP_DIM, POOL_GROUP_DIM), lambda i: (0, 0, 0)),
            pl.BlockSpec((1, D_MODEL), lambda i: (0, 0)),
        ],
        out_specs=[
            pl.BlockSpec((n_t, bb, D_MODEL), lambda i: (0, i, 0)),
            pl.BlockSpec((POOL_STATE, bb, D_MODEL), lambda i: (0, i, 0)),
        ],
        out_shape=[
            jax.ShapeDtypeStruct((n_t, b, D_MODEL), F32),
            jax.ShapeDtypeStruct((POOL_STATE, b, D_MODEL), F32),
        ],
        scratch_shapes=[pltpu.VMEM((POOL_STATE + n_t, bb, D_MODEL), F32)],
        compiler_params=_params(1),
        name="pool_step",
    )(x_t, st_t, g, w, ls)


def _swiglu_skewed(k, n_chunks, parts, w1, w3, w2, first, last=lambda: None):
    def up():
        w_gate, w_lin = w1(), w3()
        for u_ref, h_ref, _ in parts:
            u = u_ref[...]
            h_ref[...] = (jax.nn.silu(_dot(u, w_gate)) * _dot(u, w_lin)).astype(h_ref.dtype)

    def down():
        w_out = w2()
        for _, h_ref, acc_ref in parts:
            acc_ref[...] += _dot(h_ref[...], w_out)

    @pl.when(k == 0)
    def _():
        first()
        up()

    @pl.when((k > 0) & (k < n_chunks))
    def _():
        down()
        up()

    @pl.when(k == n_chunks)
    def _():
        last()
        down()


def _ffn_kernel(x_ref, g_ref, w1_ref, w3_ref, w2_ref, o_ref, u_ref, h_ref):
    def first():
        x = x_ref[...]
        u_ref[...] = _rmsnorm(x, g_ref[...]).astype(BF16)
        o_ref[...] = x

    _swiglu_skewed(pl.program_id(1), pl.num_programs(1) - 1, [(u_ref, h_ref, o_ref)],
                   lambda: w1_ref[...], lambda: w3_ref[...], lambda: w2_ref[...], first)


def _ffn(x, g, w1, w3, w2, tm):
    m = x.shape[0]
    f = w1.shape[1]
    tf = FFN_FF_TILE
    kf = f // tf
    return pl.pallas_call(
        _ffn_kernel,
        grid=(m // tm, kf + 1),
        in_specs=[
            pl.BlockSpec((tm, D_MODEL), lambda i, k: (i, 0)),
            pl.BlockSpec((1, D_MODEL), lambda i, k: (0, 0)),
            pl.BlockSpec((D_MODEL, tf), lambda i, k: (0, jnp.minimum(k, kf - 1))),
            pl.BlockSpec((D_MODEL, tf), lambda i, k: (0, jnp.minimum(k, kf - 1))),
            pl.BlockSpec((tf, D_MODEL), lambda i, k: (jnp.maximum(k - 1, 0), 0)),
        ],
        out_specs=pl.BlockSpec((tm, D_MODEL), lambda i, k: (i, 0)),
        out_shape=jax.ShapeDtypeStruct((m, D_MODEL), F32),
        scratch_shapes=[pltpu.VMEM((tm, D_MODEL), BF16), pltpu.VMEM((tm, tf), BF16)],
        compiler_params=_params(2),
        name="ffn0",
    )(x, g, w1, w3, w2)


def _qkv_kernel(x_ref, gkv_ref, wkv_ref, bkv_ref, gq_ref, wq_ref, bq_ref, kv_ref, q_ref):
    x = x_ref[...]
    xn = x * lax.rsqrt(jnp.mean(x * x, axis=-1, keepdims=True) + EPS)
    kv_ref[...] = _dot((xn * gkv_ref[...]).astype(BF16), wkv_ref[...]) + bkv_ref[...]
    q = _dot((xn * gq_ref[...]).astype(BF16), wq_ref[...]) + bq_ref[...]
    q_ref[...] = (q * HEAD_DIM ** -0.5).astype(BF16)


def _qkv(x, gkv, wkv, bkv, gq, wq, bq, tm):
    m = x.shape[0]
    row = lambda i: (i, 0)
    fixed = lambda i: (0, 0)
    return pl.pallas_call(
        _qkv_kernel,
        grid=(m // tm,),
        in_specs=[
            pl.BlockSpec((tm, D_MODEL), row),
            pl.BlockSpec((1, D_MODEL), fixed),
            pl.BlockSpec((D_MODEL, 2 * HKV), fixed),
            pl.BlockSpec((1, 2 * HKV), fixed),
            pl.BlockSpec((1, D_MODEL), fixed),
            pl.BlockSpec((D_MODEL, D_MODEL), fixed),
            pl.BlockSpec((1, D_MODEL), fixed),
        ],
        out_specs=[pl.BlockSpec((tm, 2 * HKV), row), pl.BlockSpec((tm, D_MODEL), row)],
        out_shape=[jax.ShapeDtypeStruct((m, 2 * HKV), F32), jax.ShapeDtypeStruct((m, D_MODEL), BF16)],
        compiler_params=_params(1),
        name="qkv",
    )(x, gkv, wkv, bkv, gq, wq, bq)


def _attn_seq_kernel(sinks_ref, q_ref, cur_ref, prev_ref, past_ref, o_ref, *, pos0):
    j = pl.program_id(1)
    tq = ATTN_TILE
    prev = jnp.where(j == 0, past_ref[...], prev_ref[...])
    band = jnp.concatenate([prev, cur_ref[...]], axis=0)
    c = lax.broadcasted_iota(jnp.int32, (2 * tq, tq), 0)
    r = lax.broadcasted_iota(jnp.int32, (2 * tq, tq), 1)
    key_pos = pos0 + (j - 1) * tq + c
    valid = (c > r) & (c <= r + tq) & (key_pos >= 0)
    low = lax.broadcasted_iota(jnp.int32, (2 * tq, LANES), 1) < HEAD_DIM
    zeros = jnp.zeros((2 * tq, LANES), F32)
    k_bd, v_bd_t = [], []
    v_t = band[:, HKV:].T
    zv = jnp.zeros((HEAD_DIM, 2 * tq), F32)
    for kvh in range(N_KV_HEADS):
        tile = band[:, (kvh // 2) * LANES:(kvh // 2 + 1) * LANES]
        swapped = pltpu.roll(tile, HEAD_DIM, axis=1)
        in_low, in_high = (tile, swapped) if kvh % 2 == 0 else (swapped, tile)
        k_bd.append(jnp.concatenate([jnp.where(low, in_low, zeros), jnp.where(low, zeros, in_high)],
                                    axis=0).astype(BF16))
        vt = v_t[kvh * HEAD_DIM:(kvh + 1) * HEAD_DIM]
        v_bd_t.append(jnp.concatenate([jnp.concatenate([vt, zv], axis=1),
                                       jnp.concatenate([zv, vt], axis=1)], axis=0).astype(BF16))
    for pair in range(N_HEADS // 2):
        kvh = (2 * pair) // GROUP
        ps = slice(pair * LANES, (pair + 1) * LANES)
        s_t = _dot_nt(k_bd[kvh], q_ref[:, ps])
        probs, inv = [], []
        for half in range(2):
            s = jnp.where(valid, s_t[half * 2 * tq:(half + 1) * 2 * tq], MASKED)
            sink = sinks_ref[2 * pair + half]
            m = jnp.maximum(jnp.max(s, axis=0, keepdims=True), sink)
            p = jnp.exp(s - m)
            denom = jnp.sum(p, axis=0, keepdims=True) + jnp.exp(sink - m)
            probs.append(p.astype(BF16))
            inv.append(jnp.broadcast_to(1.0 / denom, (HEAD_DIM, tq)))
        o_t = _dot(v_bd_t[kvh], jnp.concatenate(probs, axis=0)) * jnp.concatenate(inv, axis=0)
        o_ref[:, ps] = o_t.T.astype(BF16)


def _attn_seq(q, kv, past_kv, sinks, n_seq, pos0):
    m = q.shape[0]
    tq = ATTN_TILE
    nb = m // n_seq // tq
    return pl.pallas_call(
        functools.partial(_attn_seq_kernel, pos0=pos0),
        grid=(n_seq, nb),
        in_specs=[
            pl.BlockSpec(memory_space=pltpu.SMEM),
            pl.BlockSpec((tq, D_MODEL), lambda b, j: (b * nb + j, 0)),
            pl.BlockSpec((tq, 2 * HKV), lambda b, j: (b * nb + j, 0)),
            pl.BlockSpec((tq, 2 * HKV), lambda b, j: (b * nb + jnp.maximum(j - 1, 0), 0)),
            pl.BlockSpec((tq, 2 * HKV), lambda b, j: (0, 0)),
        ],
        out_specs=pl.BlockSpec((tq, D_MODEL), lambda b, j: (b * nb + j, 0)),
        out_shape=jax.ShapeDtypeStruct((m, D_MODEL), BF16),
        compiler_params=_params(2),
        name="attn_seq",
    )(sinks, q, kv, kv, past_kv)


def _attn_step_kernel(q_ref, new_ref, ck_ref, cv_ref, sink_ref, o_ref, nk_ref, nv_ref, *, pos0, n_t):
    bb = q_ref.shape[0]
    rows = N_KV_HEADS * n_t * GROUP
    pad = WINDOW - new_ref.shape[1]
    rho = lax.broadcasted_iota(jnp.int32, (rows, 2 * WINDOW), 0)
    c = lax.broadcasted_iota(jnp.int32, (rows, 2 * WINDOW), 1)
    t = (rho // GROUP) % n_t
    valid = (c > t) & (c <= t + WINDOW) & (pos0 - WINDOW + c >= 0)
    sink = sink_ref[...]
    for i in range(bb):
        new = jnp.concatenate([new_ref[i], jnp.zeros((pad, 2 * HKV), F32)], axis=0)
        kcat = jnp.concatenate([ck_ref[i], new[:, :HKV]], axis=0).astype(BF16)
        vcat = jnp.concatenate([cv_ref[i], new[:, HKV:]], axis=0).astype(BF16)
        s = jnp.where(valid, _dot_nt(q_ref[i], kcat), MASKED)
        m = jnp.maximum(jnp.max(s, axis=-1, keepdims=True), sink)
        p = jnp.exp(s - m)
        denom = jnp.sum(p, axis=-1, keepdims=True) + jnp.exp(sink - m)
        o = (_dot(p.astype(BF16), vcat) / denom).astype(BF16)
        for kvh in range(N_KV_HEADS):
            rs = n_t * GROUP
            o_ref[i, kvh] = o[kvh * rs:(kvh + 1) * rs, kvh * HEAD_DIM:(kvh + 1) * HEAD_DIM]
        nk_ref[i, 0:WINDOW - n_t, :] = ck_ref[i, n_t:WINDOW, :]
        nk_ref[i, WINDOW - n_t:WINDOW, :] = new_ref[i, 0:n_t, 0:HKV]
        nv_ref[i, 0:WINDOW - n_t, :] = cv_ref[i, n_t:WINDOW, :]
        nv_ref[i, WINDOW - n_t:WINDOW, :] = new_ref[i, 0:n_t, HKV:2 * HKV]


def _attn_step(q_bd, new_kv, cache_k, cache_v, sink_rows, pos0, n_t, bb=8):
    b = q_bd.shape[0]
    rows = q_bd.shape[1]
    n_new = new_kv.shape[1]
    blk = lambda *s: pl.BlockSpec((bb,) + s, lambda i: (i,) + (0,) * len(s))
    return pl.pallas_call(
        functools.partial(_attn_step_kernel, pos0=pos0, n_t=n_t),
        grid=(b // bb,),
        in_specs=[
            blk(rows, HKV),
            blk(n_new, 2 * HKV),
            blk(WINDOW, HKV),
            blk(WINDOW, HKV),
            pl.BlockSpec((rows, 1), lambda i: (0, 0)),
        ],
        out_specs=[blk(N_KV_HEADS, n_t * GROUP, HEAD_DIM), blk(WINDOW, HKV), blk(WINDOW, HKV)],
        out_shape=[
            jax.ShapeDtypeStruct((b, N_KV_HEADS, n_t * GROUP, HEAD_DIM), BF16),
            jax.ShapeDtypeStruct((b, WINDOW, HKV), F32),
            jax.ShapeDtypeStruct((b, WINDOW, HKV), F32),
        ],
        compiler_params=_params(1),
        name="attn_step",
    )(q_bd, new_kv, cache_k, cache_v, sink_rows)


def _oproj_kernel(o_ref, h_ref, wo_ref, bo_ref, g_ref, wr_ref, h3_ref, u_ref, comb_ref, sel_ref):
    h3 = h_ref[...] + (_dot(o_ref[...], wo_ref[...]) + bo_ref[...])
    h3_ref[...] = h3
    u = _rmsnorm(h3, g_ref[...])
    u_ref[...] = u
    logits = _dot(u.astype(BF16), wr_ref[...])
    lane = lax.broadcasted_iota(jnp.int32, logits.shape, 1)
    lg = jnp.where(lane < N_EXPERTS, logits, -jnp.inf)
    m1 = jnp.max(lg, axis=-1, keepdims=True)
    i1 = jnp.min(jnp.where(lg == m1, lane, LANES), axis=-1, keepdims=True)
    top1 = lane == i1
    lg2 = jnp.where(top1, -jnp.inf, lg)
    m2 = jnp.max(lg2, axis=-1, keepdims=True)
    i2 = jnp.min(jnp.where(lg2 == m2, lane, LANES), axis=-1, keepdims=True)
    top2 = lane == i2
    e2 = jnp.exp(m2 - m1)
    denom = 1.0 + e2
    comb = jnp.where(top1, 1.0 / denom, 0.0) + jnp.where(top2, e2 / denom, 0.0)
    comb_ref[...] = comb[:, :N_EXPERTS]
    sel_ref[...] = jnp.where(top1 | top2, 1, 0).astype(jnp.int32)[:, :N_EXPERTS]


def _oproj(o, h, wo, bo, g, wr, tm):
    m = o.shape[0]
    row = lambda i: (i, 0)
    fixed = lambda i: (0, 0)
    return pl.pallas_call(
        _oproj_kernel,
        grid=(m // tm,),
        in_specs=[
            pl.BlockSpec((tm, D_MODEL), row),
            pl.BlockSpec((tm, D_MODEL), row),
            pl.BlockSpec((D_MODEL, D_MODEL), fixed),
            pl.BlockSpec((1, D_MODEL), fixed),
            pl.BlockSpec((1, D_MODEL), fixed),
            pl.BlockSpec((D_MODEL, LANES), fixed),
        ],
        out_specs=[
            pl.BlockSpec((tm, D_MODEL), row),
            pl.BlockSpec((tm, D_MODEL), row),
            pl.BlockSpec((tm, N_EXPERTS), row),
            pl.BlockSpec((tm, N_EXPERTS), row),
        ],
        out_shape=[
            jax.ShapeDtypeStruct((m, D_MODEL), F32),
            jax.ShapeDtypeStruct((m, D_MODEL), F32),
            jax.ShapeDtypeStruct((m, N_EXPERTS), F32),
            jax.ShapeDtypeStruct((m, N_EXPERTS), jnp.int32),
        ],
        compiler_params=_params(1),
        name="oproj_router",
    )(o, h, wo, bo, g, wr)


def _row_copy(src, r, dst, p, sem):
    return pltpu.make_async_copy(src.at[pl.ds(r, 1)], dst.at[pl.ds(p, 1)], sem)


def _zero_fill_copies(zeros_ref, xs_ref, last_tile_ref, has_rows_ref, nvalid_ref, sem, n_tiles, min_tiles):
    tm = MOE_TILE
    pairs = []
    for e in range(N_EXPERTS):
        start = pl.multiple_of(last_tile_ref[e], tm)
        cp = pltpu.make_async_copy(zeros_ref, xs_ref.at[pl.ds(start, tm)], sem)
        pairs.append((has_rows_ref[e] != 0, cp))
    for tile in range(min_tiles, n_tiles):
        cp = pltpu.make_async_copy(zeros_ref, xs_ref.at[pl.ds(tile * tm, tm)], sem)
        pairs.append((tile >= nvalid_ref[0], cp))
    return pairs


def _dispatch_kernel(pos_a_ref, pos_b_ref, last_tile_ref, has_rows_ref, nvalid_ref,
                     up_ref, us_ref, xs_ref, zeros_ref, sem, zsem, *, n_blocks_p, n_tiles, min_tiles):
    i = pl.program_id(0)
    rt = ROW_DMA_TILE

    @pl.when(i == 0)
    def _():
        zeros_ref[...] = jnp.zeros(zeros_ref.shape, F32)
        pairs = _zero_fill_copies(zeros_ref, xs_ref, last_tile_ref, has_rows_ref, nvalid_ref,
                                  zsem, n_tiles, min_tiles)
        for cond, cp in pairs:
            @pl.when(cond)
            def _():
                cp.start()
        for cond, cp in pairs:
            @pl.when(cond)
            def _():
                cp.wait()

    def scatter(src_ref):
        def body(r, carry):
            tok = i * rt + r
            _row_copy(src_ref, r, xs_ref, pos_a_ref[tok], sem).start()
            _row_copy(src_ref, r, xs_ref, pos_b_ref[tok], sem).start()
            return carry
        lax.fori_loop(0, rt, body, 0, unroll=8)
        for _ in range(2):
            pltpu.make_async_copy(src_ref, xs_ref.at[pl.ds(0, rt)], sem).wait()

    @pl.when(i < n_blocks_p)
    def _():
        scatter(up_ref)

    @pl.when(i >= n_blocks_p)
    def _():
        scatter(us_ref)


def _dispatch(pos_a, pos_b, last_tile, has_rows, nvalid, u_p, u_s, n_tiles, min_tiles):
    rt = ROW_DMA_TILE
    nbp = u_p.shape[0] // rt
    nbs = u_s.shape[0] // rt
    grid_spec = pltpu.PrefetchScalarGridSpec(
        num_scalar_prefetch=5,
        grid=(nbp + nbs,),
        in_specs=[
            pl.BlockSpec((rt, D_MODEL), lambda i, *_: (jnp.minimum(i, nbp - 1), 0)),
            pl.BlockSpec((rt, D_MODEL), lambda i, *_: (jnp.maximum(i - nbp, 0), 0)),
        ],
        out_specs=pl.BlockSpec(memory_space=pl.ANY),
        scratch_shapes=[
            pltpu.VMEM((MOE_TILE, D_MODEL), F32),
            pltpu.SemaphoreType.DMA,
            pltpu.SemaphoreType.DMA,
        ],
    )
    return pl.pallas_call(
        functools.partial(_dispatch_kernel, n_blocks_p=nbp, n_tiles=n_tiles, min_tiles=min_tiles),
        grid_spec=grid_spec,
        out_shape=jax.ShapeDtypeStruct((n_tiles * MOE_TILE, D_MODEL), F32),
        compiler_params=_params(1),
        name="moe_dispatch",
    )(pos_a, pos_b, last_tile, has_rows, nvalid, u_p, u_s)


def _moe_kernel(te_ref, full_ref, nv_ref, xs_ref, w1_ref, w3_ref, w2_ref, y_ref,
                x_ref, xb_ref, ha_ref, hb_ref, sem):
    i = pl.program_id(0)
    k = pl.program_id(1)
    tm = MOE_TILE
    half = tm // 2
    valid = i < nv_ref[0]
    full = full_ref[i] != 0
    n_chunks = pl.num_programs(1) - 1
    weights = (lambda: w1_ref[0].astype(BF16), lambda: w3_ref[0].astype(BF16), lambda: w2_ref[0].astype(BF16))
    part_a = (xb_ref.at[pl.ds(0, half)], ha_ref, y_ref.at[pl.ds(0, half)])
    part_b = (xb_ref.at[pl.ds(half, half)], hb_ref, y_ref.at[pl.ds(half, half)])

    def fetch(tile):
        return pltpu.make_async_copy(xs_ref.at[pl.ds(pl.multiple_of(tile * tm, tm), tm)], x_ref, sem)

    def zero_tile():
        y_ref[...] = jnp.zeros(y_ref.shape, F32)

    def first():
        zero_tile()

        @pl.when(i == 0)
        def _():
            fetch(0).start()

        fetch(i).wait()
        xb_ref[...] = x_ref[...].astype(BF16)

        @pl.when(i + 1 < nv_ref[0])
        def _():
            fetch(i + 1).start()

    @pl.when(valid & full)
    def _():
        _swiglu_skewed(k, n_chunks, [part_a, part_b], *weights, first)

    @pl.when(valid & jnp.logical_not(full))
    def _():
        _swiglu_skewed(k, n_chunks, [part_a], *weights, first)

    @pl.when(jnp.logical_not(valid) & (k == 0))
    def _():
        zero_tile()


def _moe(tile_expert, tile_full, nvalid, xs, w1, w3, w2):
    tm, tf = MOE_TILE, MOE_FF_TILE
    n_tiles = xs.shape[0] // tm
    kf = w1.shape[2] // tf

    def out_row(i, k, te, full, nv):
        return (i, 0)

    def up(i, k, te, full, nv):
        return (te[jnp.minimum(i, nv[0] - 1)], 0, jnp.where(i < nv[0], jnp.minimum(k, kf - 1), kf - 1))

    def down(i, k, te, full, nv):
        return (te[jnp.minimum(i, nv[0] - 1)], jnp.where(i < nv[0], jnp.maximum(k - 1, 0), kf - 1), 0)

    grid_spec = pltpu.PrefetchScalarGridSpec(
        num_scalar_prefetch=3,
        grid=(n_tiles, kf + 1),
        in_specs=[
            pl.BlockSpec(memory_space=pl.ANY),
            pl.BlockSpec((1, D_MODEL, tf), up),
            pl.BlockSpec((1, D_MODEL, tf), up),
            pl.BlockSpec((1, tf, D_MODEL), down),
        ],
        out_specs=pl.BlockSpec((tm, D_MODEL), out_row),
        scratch_shapes=[
            pltpu.VMEM((tm, D_MODEL), F32),
            pltpu.VMEM((tm, D_MODEL), BF16),
            pltpu.VMEM((tm // 2, tf), BF16),
            pltpu.VMEM((tm // 2, tf), BF16),
            pltpu.SemaphoreType.DMA,
        ],
    )
    return pl.pallas_call(
        _moe_kernel,
        grid_spec=grid_spec,
        out_shape=jax.ShapeDtypeStruct(xs.shape, F32),
        compiler_params=_params(2),
        name="moe_ffn",
    )(tile_expert, tile_full, nvalid, xs, w1, w3, w2)


def _combine_kernel(pos_a_ref, pos_b_ref, h_ref, ga_ref, gb_ref, g_ref, y_ref, o_ref, ya_ref, yb_ref, sem, *, base):
    i = pl.program_id(0)
    rt = ROW_DMA_TILE

    def fetch(step, slot):
        def body(r, carry):
            tok = base + step * rt + r
            _row_copy(y_ref, pos_a_ref[tok], ya_ref.at[slot], r, sem.at[slot]).start()
            _row_copy(y_ref, pos_b_ref[tok], yb_ref.at[slot], r, sem.at[slot]).start()
            return carry
        lax.fori_loop(0, rt, body, 0, unroll=8)

    @pl.when(i == 0)
    def _():
        fetch(0, 0)

    @pl.when(i + 1 < pl.num_programs(0))
    def _():
        fetch(i + 1, (i + 1) % 2)

    slot = i % 2
    pltpu.make_async_copy(y_ref.at[pl.ds(0, rt)], ya_ref.at[slot], sem.at[slot]).wait()
    pltpu.make_async_copy(y_ref.at[pl.ds(0, rt)], yb_ref.at[slot], sem.at[slot]).wait()
    moe = ga_ref[...] * ya_ref[slot] + gb_ref[...] * yb_ref[slot]
    o_ref[...] = _rmsnorm(h_ref[...] + moe, g_ref[...])


def _combine(pos_a, pos_b, h, gate_a, gate_b, g, y, base):
    rt = ROW_DMA_TILE
    m = h.shape[0]
    row = lambda i, *_: (i, 0)
    grid_spec = pltpu.PrefetchScalarGridSpec(
        num_scalar_prefetch=2,
        grid=(m // rt,),
        in_specs=[
            pl.BlockSpec((rt, D_MODEL), row),
            pl.BlockSpec((rt, 1), row),
            pl.BlockSpec((rt, 1), row),
            pl.BlockSpec((1, D_MODEL), lambda i, *_: (0, 0)),
            pl.BlockSpec(memory_space=pl.ANY),
        ],
        out_specs=pl.BlockSpec((rt, D_MODEL), row),
        scratch_shapes=[
            pltpu.VMEM((2, rt, D_MODEL), F32),
            pltpu.VMEM((2, rt, D_MODEL), F32),
            pltpu.SemaphoreType.DMA((2,)),
        ],
    )
    return pl.pallas_call(
        functools.partial(_combine_kernel, base=base),
        grid_spec=grid_spec,
        out_shape=jax.ShapeDtypeStruct((m, D_MODEL), F32),
        compiler_params=_params(1),
        name="moe_combine",
    )(pos_a, pos_b, h, gate_a, gate_b, g, y)


def _token_cumsum(sel):
    n_e, n_tok = sel.shape
    blocks = n_tok // LANES
    s = sel.astype(F32).reshape(n_e, blocks, LANES)
    idx = jnp.arange(LANES)
    within = jnp.einsum("ebl,lm->ebm", s, (idx[:, None] <= idx[None, :]).astype(F32),
                        precision=lax.Precision.HIGHEST)
    totals = within[:, :, -1]
    bidx = jnp.arange(blocks)
    offsets = jnp.einsum("eb,bc->ec", totals, (bidx[:, None] < bidx[None, :]).astype(F32),
                         precision=lax.Precision.HIGHEST)
    return (within + offsets[:, :, None]).astype(jnp.int32).reshape(n_e, n_tok)


def _route_tables(sel, comb, n_tiles):
    tm = MOE_TILE
    sel = sel.T
    comb = comb.T
    cum = _token_cumsum(sel)
    counts = cum[:, -1]
    padded = ((counts + tm - 1) // tm) * tm
    ends = jnp.cumsum(padded)
    starts = ends - padded
    slot = starts[:, None] + cum - sel
    order = jnp.cumsum(sel, axis=0)
    first = (sel == 1) & (order == 1)
    second = (sel == 1) & (order == 2)
    pos_a = jnp.sum(jnp.where(first, slot, 0), axis=0).astype(jnp.int32)
    pos_b = jnp.sum(jnp.where(second, slot, 0), axis=0).astype(jnp.int32)
    gate_a = jnp.sum(jnp.where(first, comb, 0.0), axis=0)[:, None]
    gate_b = jnp.sum(jnp.where(second, comb, 0.0), axis=0)[:, None]
    tile_start = jnp.arange(n_tiles, dtype=jnp.int32) * tm
    tile_expert = jnp.minimum(jnp.sum(tile_start[:, None] >= ends[None, :], axis=1), N_EXPERTS - 1)
    rows_in_tile = (starts + counts)[tile_expert] - tile_start
    tile_full = (rows_in_tile > tm // 2).astype(jnp.int32)
    nvalid = (ends[-1:] // tm).astype(jnp.int32)
    last_tile = jnp.maximum(ends - tm, 0).astype(jnp.int32)
    has_rows = (counts > 0).astype(jnp.int32)
    return (pos_a, pos_b, gate_a, gate_b, tile_expert.astype(jnp.int32), tile_full, nvalid,
            last_tile, has_rows)


def kernel(x_prompt, x_sample, state_pool, cache_k, cache_v, meta_tokens, g_pool, w_pool, ls_pool, g_ffn0, w_ff1, w_ff3, w_ff2, g_kv, w_kv, b_kv, g_attn, w_q, b_q, sinks, w_o, b_o, g_ffn1, w_router, w_e1, w_e3, w_e2, g_final):
    n_seq, seq, _ = x_prompt.shape
    n_dec, n_t, _ = x_sample.shape
    past_len = PAST_LEN
    window = cache_k.shape[1]
    assert window == WINDOW and seq % POOL_TILE == 0 and N_META <= POOL_TILE

    vec = lambda a: a.reshape(1, -1).astype(F32)
    g_pool, ls_pool, g_ffn0, g_kv, b_kv, g_attn, b_q, b_o, g_ffn1, g_final = map(
        vec, (g_pool, ls_pool, g_ffn0, g_kv, b_kv, g_attn, b_q, b_o, g_ffn1, g_final))
    w_pool, w_ff1, w_ff3, w_ff2, w_kv, w_q, w_o = (
        w.astype(BF16) for w in (w_pool, w_ff1, w_ff3, w_ff2, w_kv, w_q, w_o))
    w_router = jnp.pad(w_router, ((0, 0), (0, LANES - N_EXPERTS))).astype(BF16)
    meta = meta_tokens.astype(F32)

    def layer0_tail(h1, tm_ffn, tm_proj):
        h2 = _ffn(h1, g_ffn0, w_ff1, w_ff3, w_ff2, min(tm_ffn, h1.shape[0]))
        kv, q = _qkv(h2, g_kv, w_kv, b_kv, g_attn, w_q, b_q, tm_proj)
        return h2, kv, q

    x_meta = jnp.pad(meta, ((0, POOL_TILE - N_META), (0, 0)))[None]
    h1_m, _ = _pool_seq(x_meta, jnp.zeros_like(meta), g_pool, w_pool, ls_pool, 0)
    _, kv_m, _ = layer0_tail(h1_m[0], POOL_TILE, POOL_TILE)
    past_kv = jnp.pad(kv_m[:N_META], ((WINDOW - N_META, 0), (0, 0)))

    h1_p, pool_p = _pool_seq(x_prompt, meta, g_pool, w_pool, ls_pool, N_META)
    h2_p, kv_p, q_p = layer0_tail(h1_p.reshape(n_seq * seq, D_MODEL), FFN_TILE, PROJ_TILE)
    o_p = _attn_seq(q_p, kv_p, past_kv, sinks.astype(F32), n_seq, N_META)
    h3_p, u_p, comb_p, sel_p = _oproj(o_p, h2_p, w_o, b_o, g_ffn1, w_router, PROJ_TILE)
    kv_tail = kv_p.reshape(n_seq, seq, 2 * HKV)[:, seq - WINDOW:]
    k_p = kv_tail[..., :HKV].reshape(n_seq, WINDOW, N_KV_HEADS, HEAD_DIM)
    v_p = kv_tail[..., HKV:].reshape(n_seq, WINDOW, N_KV_HEADS, HEAD_DIM)

    h1_s, pool_s = _pool_step(x_sample.transpose(1, 0, 2), state_pool.transpose(1, 0, 2),
                              g_pool, w_pool, ls_pool, past_len)
    n_s = n_t * n_dec
    h2_s, kv_s, q_s = layer0_tail(h1_s.reshape(n_s, D_MODEL), FFN_TILE, PROJ_TILE)
    q5 = q_s.reshape(n_t, n_dec, N_KV_HEADS, GROUP, HEAD_DIM).transpose(1, 2, 0, 3, 4)
    eye = jnp.eye(N_KV_HEADS, dtype=BF16)
    q_bd = (q5[:, :, :, :, None, :] * eye[None, :, None, None, :, None]).reshape(
        n_dec, N_KV_HEADS * n_t * GROUP, HKV)
    new_kv = jnp.pad(kv_s.reshape(n_t, n_dec, 2 * HKV).transpose(1, 0, 2), ((0, 0), (0, 16 - n_t), (0, 0)))
    sink_rows = jnp.broadcast_to(sinks.astype(F32).reshape(N_KV_HEADS, 1, GROUP),
                                 (N_KV_HEADS, n_t, GROUP)).reshape(-1, 1)
    o_s4, k_s, v_s = _attn_step(q_bd, new_kv, cache_k.reshape(n_dec, WINDOW, HKV),
                                cache_v.reshape(n_dec, WINDOW, HKV), sink_rows, past_len, n_t)
    o_s = o_s4.reshape(n_dec, N_KV_HEADS, n_t, GROUP, HEAD_DIM).transpose(2, 0, 1, 3, 4).reshape(n_s, D_MODEL)
    h3_s, u_s, comb_s, sel_s = _oproj(o_s, h2_s, w_o, b_o, g_ffn1, w_router, PROJ_TILE)

    n_tok = n_seq * seq + n_s
    min_tiles = 2 * n_tok // MOE_TILE
    n_tiles = min_tiles + N_EXPERTS
    pos_a, pos_b, gate_a, gate_b, tile_expert, tile_full, nvalid, last_tile, has_rows = _route_tables(
        jnp.concatenate([sel_p, sel_s]), jnp.concatenate([comb_p, comb_s]), n_tiles)
    xs = _dispatch(pos_a, pos_b, last_tile, has_rows, nvalid, u_p, u_s, n_tiles, min_tiles)
    y = _moe(tile_expert, tile_full, nvalid, xs, w_e1.astype(F32), w_e3.astype(F32), w_e2.astype(F32))
    n_p = n_seq * seq
    y_p = _combine(pos_a, pos_b, h3_p, gate_a[:n_p], gate_b[:n_p], g_final, y, 0)
    y_s = _combine(pos_a, pos_b, h3_s, gate_a[n_p:], gate_b[n_p:], g_final, y, n_p)

    return (y_p.reshape(n_seq, seq, D_MODEL),
            y_s.reshape(n_t, n_dec, D_MODEL).transpose(1, 0, 2),
            pool_p,
            pool_s.transpose(1, 0, 2),
            k_p, v_p,
            k_s.reshape(n_dec, WINDOW, N_KV_HEADS, HEAD_DIM),
            v_s.reshape(n_dec, WINDOW, N_KV_HEADS, HEAD_DIM))
```

```python
import functools

import jax
import jax.numpy as jnp
from jax import lax
from jax.experimental import pallas as pl
from jax.experimental.pallas import tpu as pltpu

F32 = jnp.float32
BF16 = jnp.bfloat16

D_MODEL = 2048
N_META = 16
POOL_WINDOWS = (2, 4, 8, 16)
POOL_GROUP_DIM = D_MODEL // len(POOL_WINDOWS)
POOL_STATE = max(POOL_WINDOWS) - 1
HEAD_DIM = 64
N_HEADS = D_MODEL // HEAD_DIM
N_KV_HEADS = 4
GROUP = N_HEADS // N_KV_HEADS
HKV = N_KV_HEADS * HEAD_DIM
WINDOW = 128
N_EXPERTS = 8
EPS = 1e-5
PAST_LEN = 8192
MASKED = -1e30

LANES = 128
V7X_VMEM_BYTES = 64 * 2 ** 20
VMEM_LIMIT = V7X_VMEM_BYTES - 8 * 2 ** 20

POOL_TILE = 128
ATTN_TILE = WINDOW
MOE_TILE = 1024
MOE_QUARTERS = 4
MOE_FF_TILE = 512
FFN_FF_TILE = 512
FFN_TILE = 1024
PROJ_TILE = 256
ROW_DMA_TILE = 256


def _dot(a, b):
    return jnp.dot(a, b, preferred_element_type=F32)


def _dot_nt(a, b):
    return lax.dot_general(a, b, (((1,), (1,)), ((), ())), preferred_element_type=F32)


def _rmsnorm(x, g):
    return x * lax.rsqrt(jnp.mean(x * x, axis=-1, keepdims=True) + EPS) * g


def _params(n_axes):
    return pltpu.CompilerParams(dimension_semantics=("arbitrary",) * n_axes,
                                vmem_limit_bytes=VMEM_LIMIT)


def _pool_seq_kernel(x_ref, meta_ref, g_ref, w_ref, ls_ref, h_ref, st_ref, e_ref, *, pos0):
    t = pl.program_id(1)
    tt = POOL_TILE
    g = g_ref[...]

    @pl.when(t == 0)
    def _():
        e_ref[0:tt - N_META, :] = jnp.zeros((tt - N_META, D_MODEL), F32)
        e_ref[tt - N_META:tt, :] = _rmsnorm(meta_ref[...], g)

    @pl.when(t > 0)
    def _():
        e_ref[0:tt, :] = e_ref[tt:2 * tt, :]

    x = x_ref[0]
    u = _rmsnorm(x, g)
    e_ref[tt:2 * tt, :] = u

    e = e_ref[...]
    hi = e.astype(BF16)
    lo = (e - hi.astype(F32)).astype(BF16)
    r = lax.broadcasted_iota(jnp.int32, (tt, 2 * tt), 0)
    c = lax.broadcasted_iota(jnp.int32, (tt, 2 * tt), 1)
    pos = pos0 + t * tt + lax.broadcasted_iota(jnp.int32, (tt, 1), 0)
    for gi, w in enumerate(POOL_WINDOWS):
        sl = slice(gi * POOL_GROUP_DIM, (gi + 1) * POOL_GROUP_DIM)
        band = jnp.where((c > r + tt - w) & (c <= r + tt), 1.0, 0.0).astype(BF16)
        win_sum = _dot(band, hi[:, sl]) + _dot(band, lo[:, sl])
        cnt = jnp.minimum(w, pos + 1).astype(F32)
        mix = win_sum / cnt - u[:, sl]
        o = _dot(mix.astype(BF16), w_ref[gi]) * ls_ref[:, sl]
        h_ref[0, :, sl] = x[:, sl] + o

    @pl.when(t == pl.num_programs(1) - 1)
    def _():
        st_ref[0] = e_ref[2 * tt - POOL_STATE:2 * tt, :]


def _pool_seq(x, meta, g, w, ls, pos0):
    b, t, _ = x.shape
    tt = POOL_TILE
    return pl.pallas_call(
        functools.partial(_pool_seq_kernel, pos0=pos0),
        grid=(b, t // tt),
        in_specs=[
            pl.BlockSpec((1, tt, D_MODEL), lambda i, j: (i, j, 0)),
            pl.BlockSpec((N_META, D_MODEL), lambda i, j: (0, 0)),
            pl.BlockSpec((1, D_MODEL), lambda i, j: (0, 0)),
            pl.BlockSpec((len(POOL_WINDOWS), POOL_GROUP_DIM, POOL_GROUP_DIM), lambda i, j: (0, 0, 0)),
            pl.BlockSpec((1, D_MODEL), lambda i, j: (0, 0)),
        ],
        out_specs=[
            pl.BlockSpec((1, tt, D_MODEL), lambda i, j: (i, j, 0)),
            pl.BlockSpec((1, POOL_STATE, D_MODEL), lambda i, j: (i, 0, 0)),
        ],
        out_shape=[
            jax.ShapeDtypeStruct((b, t, D_MODEL), F32),
            jax.ShapeDtypeStruct((b, POOL_STATE, D_MODEL), F32),
        ],
        scratch_shapes=[pltpu.VMEM((2 * tt, D_MODEL), F32)],
        compiler_params=_params(2),
        name="pool_seq",
    )(x, meta, g, w, ls)


def _pool_step_kernel(x_ref, st_ref, g_ref, w_ref, ls_ref, h_ref, nst_ref, e_ref, *, pos0):
    n_t, bb, _ = x_ref.shape
    g = g_ref[...]
    for j in range(POOL_STATE):
        e_ref[j] = st_ref[j]
    for t in range(n_t):
        e_ref[POOL_STATE + t] = _rmsnorm(x_ref[t], g)
    for j in range(POOL_STATE):
        nst_ref[j] = e_ref[j + n_t]
    for gi, w in enumerate(POOL_WINDOWS):
        sl = slice(gi * POOL_GROUP_DIM, (gi + 1) * POOL_GROUP_DIM)
        mixes = []
        for t in range(n_t):
            win_sum = e_ref[POOL_STATE + t, :, sl]
            for i in range(1, w):
                win_sum = win_sum + e_ref[POOL_STATE + t - i, :, sl]
            cnt = float(min(w, pos0 + t + 1))
            mixes.append(win_sum / cnt - e_ref[POOL_STATE + t, :, sl])
        mix = jnp.concatenate(mixes, axis=0)
        o = _dot(mix.astype(BF16), w_ref[gi]) * ls_ref[:, sl]
        for t in range(n_t):
            h_ref[t, :, sl] = x_ref[t, :, sl] + o[t * bb:(t + 1) * bb]


def _pool_step(x_t, st_t, g, w, ls, pos0, bb=32):
    n_t, b, _ = x_t.shape
    return pl.pallas_call(
        functools.partial(_pool_step_kernel, pos0=pos0),
        grid=(b // bb,),
        in_specs=[
            pl.BlockSpec((n_t, bb, D_MODEL), lambda i: (0, i, 0)),
            pl.BlockSpec((POOL_STATE, bb, D_MODEL), lambda i: (0, i, 0)),
            pl.BlockSpec((1, D_MODEL), lambda i: (0, 0)),
            pl.BlockSpec((len(POOL_WINDOWS), POOL_GROUP_DIM, POOL_GROUP_DIM), lambda i: (0, 0, 0)),
            pl.BlockSpec((1, D_MODEL), lambda i: (0, 0)),
        ],
        out_specs=[
            pl.BlockSpec((n_t, bb, D_MODEL), lambda i: (0, i, 0)),
            pl.BlockSpec((POOL_STATE, bb, D_MODEL), lambda i: (0, i, 0)),
        ],
        out_shape=[
            jax.ShapeDtypeStruct((n_t, b, D_MODEL), F32),
            jax.ShapeDtypeStruct((POOL_STATE, b, D_MODEL), F32),
        ],
        scratch_shapes=[pltpu.VMEM((POOL_STATE + n_t, bb, D_MODEL), F32)],
        compiler_params=_params(1),
        name="pool_step",
    )(x_t, st_t, g, w, ls)


def _swiglu_skewed(k, n_chunks, parts, w1, w3, w2, first, last=lambda: None):
    def up():
        w_gate, w_lin = w1(), w3()
        for u_ref, h_ref, _ in parts:
            u = u_ref[...]
            h_ref[...] = (jax.nn.silu(_dot(u, w_gate)) * _dot(u, w_lin)).astype(h_ref.dtype)

    def down():
        w_out = w2()
        for _, h_ref, acc_ref in parts:
            acc_ref[...] += _dot(h_ref[...], w_out)

    @pl.when(k == 0)
    def _():
        first()
        up()

    @pl.when((k > 0) & (k < n_chunks))
    def _():
        down()
        up()

    @pl.when(k == n_chunks)
    def _():
        last()
        down()


def _ffn_kernel(x_ref, g_ref, w1_ref, w3_ref, w2_ref, o_ref, u_ref, h_ref):
    def first():
        x = x_ref[...]
        u_ref[...] = _rmsnorm(x, g_ref[...]).astype(BF16)
        o_ref[...] = x

    _swiglu_skewed(pl.program_id(1), pl.num_programs(1) - 1, [(u_ref, h_ref, o_ref)],
                   lambda: w1_ref[...], lambda: w3_ref[...], lambda: w2_ref[...], first)


def _ffn(x, g, w1, w3, w2, tm):
    m = x.shape[0]
    f = w1.shape[1]
    tf = FFN_FF_TILE
    kf = f // tf
    return pl.pallas_call(
        _ffn_kernel,
        grid=(m // tm, kf + 1),
        in_specs=[
            pl.BlockSpec((tm, D_MODEL), lambda i, k: (i, 0)),
            pl.BlockSpec((1, D_MODEL), lambda i, k: (0, 0)),
            pl.BlockSpec((D_MODEL, tf), lambda i, k: (0, jnp.minimum(k, kf - 1))),
            pl.BlockSpec((D_MODEL, tf), lambda i, k: (0, jnp.minimum(k, kf - 1))),
            pl.BlockSpec((tf, D_MODEL), lambda i, k: (jnp.maximum(k - 1, 0), 0)),
        ],
        out_specs=pl.BlockSpec((tm, D_MODEL), lambda i, k: (i, 0)),
        out_shape=jax.ShapeDtypeStruct((m, D_MODEL), F32),
        scratch_shapes=[pltpu.VMEM((tm, D_MODEL), BF16), pltpu.VMEM((tm, tf), BF16)],
        compiler_params=_params(2),
        name="ffn0",
    )(x, g, w1, w3, w2)


def _qkv_kernel(x_ref, gkv_ref, wkv_ref, bkv_ref, gq_ref, wq_ref, bq_ref, kv_ref, q_ref):
    x = x_ref[...]
    xn = x * lax.rsqrt(jnp.mean(x * x, axis=-1, keepdims=True) + EPS)
    kv_ref[...] = _dot((xn * gkv_ref[...]).astype(BF16), wkv_ref[...]) + bkv_ref[...]
    q = _dot((xn * gq_ref[...]).astype(BF16), wq_ref[...]) + bq_ref[...]
    q_ref[...] = (q * HEAD_DIM ** -0.5).astype(BF16)


def _qkv(x, gkv, wkv, bkv, gq, wq, bq, tm):
    m = x.shape[0]
    row = lambda i: (i, 0)
    fixed = lambda i: (0, 0)
    return pl.pallas_call(
        _qkv_kernel,
        grid=(m // tm,),
        in_specs=[
            pl.BlockSpec((tm, D_MODEL), row),
            pl.BlockSpec((1, D_MODEL), fixed),
            pl.BlockSpec((D_MODEL, 2 * HKV), fixed),
            pl.BlockSpec((1, 2 * HKV), fixed),
            pl.BlockSpec((1, D_MODEL), fixed),
            pl.BlockSpec((D_MODEL, D_MODEL), fixed),
            pl.BlockSpec((1, D_MODEL), fixed),
        ],
        out_specs=[pl.BlockSpec((tm, 2 * HKV), row), pl.BlockSpec((tm, D_MODEL), row)],
        out_shape=[jax.ShapeDtypeStruct((m, 2 * HKV), F32), jax.ShapeDtypeStruct((m, D_MODEL), BF16)],
        compiler_params=_params(1),
        name="qkv",
    )(x, gkv, wkv, bkv, gq, wq, bq)


def _attn_seq_kernel(sinks_ref, q_ref, cur_ref, prev_ref, past_ref, o_ref, *, pos0):
    j = pl.program_id(1)
    tq = ATTN_TILE
    prev = jnp.where(j == 0, past_ref[...], prev_ref[...])
    band = jnp.concatenate([prev, cur_ref[...]], axis=0)
    c = lax.broadcasted_iota(jnp.int32, (2 * tq, tq), 0)
    r = lax.broadcasted_iota(jnp.int32, (2 * tq, tq), 1)
    key_pos = pos0 + (j - 1) * tq + c
    valid = (c > r) & (c <= r + tq) & (key_pos >= 0)
    low = lax.broadcasted_iota(jnp.int32, (2 * tq, LANES), 1) < HEAD_DIM
    zeros = jnp.zeros((2 * tq, LANES), F32)
    k_bd, v_bd_t = [], []
    v_t = band[:, HKV:].T
    zv = jnp.zeros((HEAD_DIM, 2 * tq), F32)
    for kvh in range(N_KV_HEADS):
        tile = band[:, (kvh // 2) * LANES:(kvh // 2 + 1) * LANES]
        swapped = pltpu.roll(tile, HEAD_DIM, axis=1)
        in_low, in_high = (tile, swapped) if kvh % 2 == 0 else (swapped, tile)
        k_bd.append(jnp.concatenate([jnp.where(low, in_low, zeros), jnp.where(low, zeros, in_high)],
                                    axis=0).astype(BF16))
        vt = v_t[kvh * HEAD_DIM:(kvh + 1) * HEAD_DIM]
        v_bd_t.append(jnp.concatenate([jnp.concatenate([vt, zv], axis=1),
                                       jnp.concatenate([zv, vt], axis=1)], axis=0).astype(BF16))
    for pair in range(N_HEADS // 2):
        kvh = (2 * pair) // GROUP
        ps = slice(pair * LANES, (pair + 1) * LANES)
        s_t = _dot_nt(k_bd[kvh], q_ref[:, ps])
        probs, inv = [], []
        for half in range(2):
            s = jnp.where(valid, s_t[half * 2 * tq:(half + 1) * 2 * tq], MASKED)
            sink = sinks_ref[2 * pair + half]
            m = jnp.maximum(jnp.max(s, axis=0, keepdims=True), sink)
            p = jnp.exp(s - m)
            denom = jnp.sum(p, axis=0, keepdims=True) + jnp.exp(sink - m)
            probs.append(p.astype(BF16))
            inv.append(jnp.broadcast_to(1.0 / denom, (HEAD_DIM, tq)))
        o_t = _dot(v_bd_t[kvh], jnp.concatenate(probs, axis=0)) * jnp.concatenate(inv, axis=0)
        o_ref[:, ps] = o_t.T.astype(BF16)


def _attn_seq(q, kv, past_kv, sinks, n_seq, pos0):
    m = q.shape[0]
    tq = ATTN_TILE
    nb = m // n_seq // tq
    return pl.pallas_call(
        functools.partial(_attn_seq_kernel, pos0=pos0),
        grid=(n_seq, nb),
        in_specs=[
            pl.BlockSpec(memory_space=pltpu.SMEM),
            pl.BlockSpec((tq, D_MODEL), lambda b, j: (b * nb + j, 0)),
            pl.BlockSpec((tq, 2 * HKV), lambda b, j: (b * nb + j, 0)),
            pl.BlockSpec((tq, 2 * HKV), lambda b, j: (b * nb + jnp.maximum(j - 1, 0), 0)),
            pl.BlockSpec((tq, 2 * HKV), lambda b, j: (0, 0)),
        ],
        out_specs=pl.BlockSpec((tq, D_MODEL), lambda b, j: (b * nb + j, 0)),
        out_shape=jax.ShapeDtypeStruct((m, D_MODEL), BF16),
        compiler_params=_params(2),
        name="attn_seq",
    )(sinks, q, kv, kv, past_kv)


def _attn_step_kernel(q_ref, new_ref, ck_ref, cv_ref, sink_ref, o_ref, nk_ref, nv_ref, *, pos0, n_t):
    bb = q_ref.shape[0]
    rows = N_KV_HEADS * n_t * GROUP
    pad = WINDOW - new_ref.shape[1]
    rho = lax.broadcasted_iota(jnp.int32, (rows, 2 * WINDOW), 0)
    c = lax.broadcasted_iota(jnp.int32, (rows, 2 * WINDOW), 1)
    t = (rho // GROUP) % n_t
    valid = (c > t) & (c <= t + WINDOW) & (pos0 - WINDOW + c >= 0)
    sink = sink_ref[...]
    for i in range(bb):
        new = jnp.concatenate([new_ref[i], jnp.zeros((pad, 2 * HKV), F32)], axis=0)
        kcat = jnp.concatenate([ck_ref[i], new[:, :HKV]], axis=0).astype(BF16)
        vcat = jnp.concatenate([cv_ref[i], new[:, HKV:]], axis=0).astype(BF16)
        s = jnp.where(valid, _dot_nt(q_ref[i], kcat), MASKED)
        m = jnp.maximum(jnp.max(s, axis=-1, keepdims=True), sink)
        p = jnp.exp(s - m)
        denom = jnp.sum(p, axis=-1, keepdims=True) + jnp.exp(sink - m)
        o = (_dot(p.astype(BF16), vcat) / denom).astype(BF16)
        for kvh in range(N_KV_HEADS):
            rs = n_t * GROUP
            o_ref[i, kvh] = o[kvh * rs:(kvh + 1) * rs, kvh * HEAD_DIM:(kvh + 1) * HEAD_DIM]
        nk_ref[i, 0:WINDOW - n_t, :] = ck_ref[i, n_t:WINDOW, :]
        nk_ref[i, WINDOW - n_t:WINDOW, :] = new_ref[i, 0:n_t, 0:HKV]
        nv_ref[i, 0:WINDOW - n_t, :] = cv_ref[i, n_t:WINDOW, :]
        nv_ref[i, WINDOW - n_t:WINDOW, :] = new_ref[i, 0:n_t, HKV:2 * HKV]


def _attn_step(q_bd, new_kv, cache_k, cache_v, sink_rows, pos0, n_t, bb=8):
    b = q_bd.shape[0]
    rows = q_bd.shape[1]
    n_new = new_kv.shape[1]
    blk = lambda *s: pl.BlockSpec((bb,) + s, lambda i: (i,) + (0,) * len(s))
    return pl.pallas_call(
        functools.partial(_attn_step_kernel, pos0=pos0, n_t=n_t),
        grid=(b // bb,),
        in_specs=[
            blk(rows, HKV),
            blk(n_new, 2 * HKV),
            blk(WINDOW, HKV),
            blk(WINDOW, HKV),
            pl.BlockSpec((rows, 1), lambda i: (0, 0)),
        ],
        out_specs=[blk(N_KV_HEADS, n_t * GROUP, HEAD_DIM), blk(WINDOW, HKV), blk(WINDOW, HKV)],
        out_shape=[
            jax.ShapeDtypeStruct((b, N_KV_HEADS, n_t * GROUP, HEAD_DIM), BF16),
            jax.ShapeDtypeStruct((b, WINDOW, HKV), F32),
            jax.ShapeDtypeStruct((b, WINDOW, HKV), F32),
        ],
        compiler_params=_params(1),
        name="attn_step",
    )(q_bd, new_kv, cache_k, cache_v, sink_rows)


def _oproj_kernel(o_ref, h_ref, wo_ref, bo_ref, g_ref, wr_ref, h3_ref, u_ref, comb_ref, sel_ref):
    h3 = h_ref[...] + (_dot(o_ref[...], wo_ref[...]) + bo_ref[...])
    h3_ref[...] = h3
    u = _rmsnorm(h3, g_ref[...])
    u_ref[...] = u
    logits = _dot(u.astype(BF16), wr_ref[...])
    lane = lax.broadcasted_iota(jnp.int32, logits.shape, 1)
    lg = jnp.where(lane < N_EXPERTS, logits, -jnp.inf)
    m1 = jnp.max(lg, axis=-1, keepdims=True)
    i1 = jnp.min(jnp.where(lg == m1, lane, LANES), axis=-1, keepdims=True)
    top1 = lane == i1
    lg2 = jnp.where(top1, -jnp.inf, lg)
    m2 = jnp.max(lg2, axis=-1, keepdims=True)
    i2 = jnp.min(jnp.where(lg2 == m2, lane, LANES), axis=-1, keepdims=True)
    top2 = lane == i2
    e2 = jnp.exp(m2 - m1)
    denom = 1.0 + e2
    comb = jnp.where(top1, 1.0 / denom, 0.0) + jnp.where(top2, e2 / denom, 0.0)
    comb_ref[...] = comb[:, :N_EXPERTS]
    sel_ref[...] = jnp.where(top1 | top2, 1, 0).astype(jnp.int32)[:, :N_EXPERTS]


def _oproj(o, h, wo, bo, g, wr, tm):
    m = o.shape[0]
    row = lambda i: (i, 0)
    fixed = lambda i: (0, 0)
    return pl.pallas_call(
        _oproj_kernel,
        grid=(m // tm,),
        in_specs=[
            pl.BlockSpec((tm, D_MODEL), row),
            pl.BlockSpec((tm, D_MODEL), row),
            pl.BlockSpec((D_MODEL, D_MODEL), fixed),
            pl.BlockSpec((1, D_MODEL), fixed),
            pl.BlockSpec((1, D_MODEL), fixed),
            pl.BlockSpec((D_MODEL, LANES), fixed),
        ],
        out_specs=[
            pl.BlockSpec((tm, D_MODEL), row),
            pl.BlockSpec((tm, D_MODEL), row),
            pl.BlockSpec((tm, N_EXPERTS), row),
            pl.BlockSpec((tm, N_EXPERTS), row),
        ],
        out_shape=[
            jax.ShapeDtypeStruct((m, D_MODEL), F32),
            jax.ShapeDtypeStruct((m, D_MODEL), F32),
            jax.ShapeDtypeStruct((m, N_EXPERTS), F32),
            jax.ShapeDtypeStruct((m, N_EXPERTS), jnp.int32),
        ],
        compiler_params=_params(1),
        name="oproj_router",
    )(o, h, wo, bo, g, wr)


def _row_copy(src, r, dst, p, sem):
    return pltpu.make_async_copy(src.at[pl.ds(r, 1)], dst.at[pl.ds(p, 1)], sem)


def _zero_fill_copies(zeros_ref, xs_ref, last_tile_ref, has_rows_ref, nvalid_ref, sem, n_tiles, min_tiles):
    tm = MOE_TILE
    pairs = []
    for e in range(N_EXPERTS):
        start = pl.multiple_of(last_tile_ref[e], tm)
        cp = pltpu.make_async_copy(zeros_ref, xs_ref.at[pl.ds(start, tm)], sem)
        pairs.append((has_rows_ref[e] != 0, cp))
    for tile in range(min_tiles, n_tiles):
        cp = pltpu.make_async_copy(zeros_ref, xs_ref.at[pl.ds(tile * tm, tm)], sem)
        pairs.append((tile >= nvalid_ref[0], cp))
    return pairs


def _dispatch_kernel(pos_a_ref, pos_b_ref, last_tile_ref, has_rows_ref, nvalid_ref,
                     up_ref, us_ref, xs_ref, zeros_ref, sem, zsem, *, n_blocks_p, n_tiles, min_tiles):
    i = pl.program_id(0)
    rt = ROW_DMA_TILE

    @pl.when(i == 0)
    def _():
        zeros_ref[...] = jnp.zeros(zeros_ref.shape, F32)
        pairs = _zero_fill_copies(zeros_ref, xs_ref, last_tile_ref, has_rows_ref, nvalid_ref,
                                  zsem, n_tiles, min_tiles)
        for cond, cp in pairs:
            @pl.when(cond)
            def _():
                cp.start()
        for cond, cp in pairs:
            @pl.when(cond)
            def _():
                cp.wait()

    def scatter(src_ref):
        def body(r, carry):
            tok = i * rt + r
            _row_copy(src_ref, r, xs_ref, pos_a_ref[tok], sem).start()
            _row_copy(src_ref, r, xs_ref, pos_b_ref[tok], sem).start()
            return carry
        lax.fori_loop(0, rt, body, 0, unroll=8)
        for _ in range(2):
            pltpu.make_async_copy(src_ref, xs_ref.at[pl.ds(0, rt)], sem).wait()

    @pl.when(i < n_blocks_p)
    def _():
        scatter(up_ref)

    @pl.when(i >= n_blocks_p)
    def _():
        scatter(us_ref)


def _dispatch(pos_a, pos_b, last_tile, has_rows, nvalid, u_p, u_s, n_tiles, min_tiles):
    rt = ROW_DMA_TILE
    nbp = u_p.shape[0] // rt
    nbs = u_s.shape[0] // rt
    grid_spec = pltpu.PrefetchScalarGridSpec(
        num_scalar_prefetch=5,
        grid=(nbp + nbs,),
        in_specs=[
            pl.BlockSpec((rt, D_MODEL), lambda i, *_: (jnp.minimum(i, nbp - 1), 0)),
            pl.BlockSpec((rt, D_MODEL), lambda i, *_: (jnp.maximum(i - nbp, 0), 0)),
        ],
        out_specs=pl.BlockSpec(memory_space=pl.ANY),
        scratch_shapes=[
            pltpu.VMEM((MOE_TILE, D_MODEL), F32),
            pltpu.SemaphoreType.DMA,
            pltpu.SemaphoreType.DMA,
        ],
    )
    return pl.pallas_call(
        functools.partial(_dispatch_kernel, n_blocks_p=nbp, n_tiles=n_tiles, min_tiles=min_tiles),
        grid_spec=grid_spec,
        out_shape=jax.ShapeDtypeStruct((n_tiles * MOE_TILE, D_MODEL), F32),
        compiler_params=_params(1),
        name="moe_dispatch",
    )(pos_a, pos_b, last_tile, has_rows, nvalid, u_p, u_s)


def _moe_kernel(te_ref, quarters_ref, nv_ref, xs_ref, w1_ref, w3_ref, w2_ref, y_ref, x_ref, h_ref, sem):
    i = pl.program_id(0)
    k = pl.program_id(1)
    tm = MOE_TILE
    valid = i < nv_ref[0]
    quarters = quarters_ref[i]
    n_chunks = pl.num_programs(1) - 1
    weights = (lambda: w1_ref[0], lambda: w3_ref[0], lambda: w2_ref[0])

    def fetch(tile):
        return pltpu.make_async_copy(xs_ref.at[pl.ds(pl.multiple_of(tile * tm, tm), tm)], x_ref, sem)

    def zero_tile():
        y_ref[...] = jnp.zeros(y_ref.shape, F32)

    def first():
        zero_tile()

        @pl.when(i == 0)
        def _():
            fetch(0).start()

        fetch(i).wait()

    def last():
        @pl.when(i + 1 < nv_ref[0])
        def _():
            fetch(i + 1).start()

    for n in range(1, MOE_QUARTERS + 1):
        rows = n * (tm // MOE_QUARTERS)
        part = (x_ref.at[pl.ds(0, rows)], h_ref.at[pl.ds(0, rows)], y_ref.at[pl.ds(0, rows)])

        @pl.when(valid & (quarters == n))
        def _():
            _swiglu_skewed(k, n_chunks, [part], *weights, first, last)

    @pl.when(jnp.logical_not(valid) & (k == 0))
    def _():
        zero_tile()


def _moe(tile_expert, tile_quarters, nvalid, xs, w1, w3, w2):
    tm, tf = MOE_TILE, MOE_FF_TILE
    n_tiles = xs.shape[0] // tm
    kf = w1.shape[2] // tf

    def out_row(i, k, te, full, nv):
        return (i, 0)

    def up(i, k, te, full, nv):
        return (te[jnp.minimum(i, nv[0] - 1)], 0, jnp.where(i < nv[0], jnp.minimum(k, kf - 1), kf - 1))

    def down(i, k, te, full, nv):
        return (te[jnp.minimum(i, nv[0] - 1)], jnp.where(i < nv[0], jnp.maximum(k - 1, 0), kf - 1), 0)

    grid_spec = pltpu.PrefetchScalarGridSpec(
        num_scalar_prefetch=3,
        grid=(n_tiles, kf + 1),
        in_specs=[
            pl.BlockSpec(memory_space=pl.ANY),
            pl.BlockSpec((1, D_MODEL, tf), up),
            pl.BlockSpec((1, D_MODEL, tf), up),
            pl.BlockSpec((1, tf, D_MODEL), down),
        ],
        out_specs=pl.BlockSpec((tm, D_MODEL), out_row),
        scratch_shapes=[
            pltpu.VMEM((tm, D_MODEL), F32),
            pltpu.VMEM((tm, tf), F32),
            pltpu.SemaphoreType.DMA,
        ],
    )
    return pl.pallas_call(
        _moe_kernel,
        grid_spec=grid_spec,
        out_shape=jax.ShapeDtypeStruct(xs.shape, F32),
        compiler_params=_params(2),
        name="moe_ffn",
    )(tile_expert, tile_quarters, nvalid, xs, w1, w3, w2)


def _combine_kernel(pos_a_ref, pos_b_ref, h_ref, ga_ref, gb_ref, g_ref, y_ref, o_ref, ya_ref, yb_ref, sem, *, base):
    i = pl.program_id(0)
    rt = ROW_DMA_TILE

    def fetch(step, slot):
        def body(r, carry):
            tok = base + step * rt + r
            _row_copy(y_ref, pos_a_ref[tok], ya_ref.at[slot], r, sem.at[slot]).start()
            _row_copy(y_ref, pos_b_ref[tok], yb_ref.at[slot], r, sem.at[slot]).start()
            return carry
        lax.fori_loop(0, rt, body, 0, unroll=8)

    @pl.when(i == 0)
    def _():
        fetch(0, 0)

    @pl.when(i + 1 < pl.num_programs(0))
    def _():
        fetch(i + 1, (i + 1) % 2)

    slot = i % 2
    pltpu.make_async_copy(y_ref.at[pl.ds(0, rt)], ya_ref.at[slot], sem.at[slot]).wait()
    pltpu.make_async_copy(y_ref.at[pl.ds(0, rt)], yb_ref.at[slot], sem.at[slot]).wait()
    moe = ga_ref[...] * ya_ref[slot] + gb_ref[...] * yb_ref[slot]
    o_ref[...] = _rmsnorm(h_ref[...] + moe, g_ref[...])


def _combine(pos_a, pos_b, h, gate_a, gate_b, g, y, base):
    rt = ROW_DMA_TILE
    m = h.shape[0]
    row = lambda i, *_: (i, 0)
    grid_spec = pltpu.PrefetchScalarGridSpec(
        num_scalar_prefetch=2,
        grid=(m // rt,),
        in_specs=[
            pl.BlockSpec((rt, D_MODEL), row),
            pl.BlockSpec((rt, 1), row),
            pl.BlockSpec((rt, 1), row),
            pl.BlockSpec((1, D_MODEL), lambda i, *_: (0, 0)),
            pl.BlockSpec(memory_space=pl.ANY),
        ],
        out_specs=pl.BlockSpec((rt, D_MODEL), row),
        scratch_shapes=[
            pltpu.VMEM((2, rt, D_MODEL), F32),
            pltpu.VMEM((2, rt, D_MODEL), F32),
            pltpu.SemaphoreType.DMA((2,)),
        ],
    )
    return pl.pallas_call(
        functools.partial(_combine_kernel, base=base),
        grid_spec=grid_spec,
        out_shape=jax.ShapeDtypeStruct((m, D_MODEL), F32),
        compiler_params=_params(1),
        name="moe_combine",
    )(pos_a, pos_b, h, gate_a, gate_b, g, y)


def _token_cumsum(sel):
    n_e, n_tok = sel.shape
    blocks = n_tok // LANES
    s = sel.astype(F32).reshape(n_e, blocks, LANES)
    idx = jnp.arange(LANES)
    within = jnp.einsum("ebl,lm->ebm", s, (idx[:, None] <= idx[None, :]).astype(F32),
                        precision=lax.Precision.HIGHEST)
    totals = within[:, :, -1]
    bidx = jnp.arange(blocks)
    offsets = jnp.einsum("eb,bc->ec", totals, (bidx[:, None] < bidx[None, :]).astype(F32),
                         precision=lax.Precision.HIGHEST)
    return (within + offsets[:, :, None]).astype(jnp.int32).reshape(n_e, n_tok)


def _route_tables(sel, comb, n_tiles):
    tm = MOE_TILE
    sel = sel.T
    comb = comb.T
    cum = _token_cumsum(sel)
    counts = cum[:, -1]
    padded = ((counts + tm - 1) // tm) * tm
    ends = jnp.cumsum(padded)
    starts = ends - padded
    slot = starts[:, None] + cum - sel
    order = jnp.cumsum(sel, axis=0)
    first = (sel == 1) & (order == 1)
    second = (sel == 1) & (order == 2)
    pos_a = jnp.sum(jnp.where(first, slot, 0), axis=0).astype(jnp.int32)
    pos_b = jnp.sum(jnp.where(second, slot, 0), axis=0).astype(jnp.int32)
    gate_a = jnp.sum(jnp.where(first, comb, 0.0), axis=0)[:, None]
    gate_b = jnp.sum(jnp.where(second, comb, 0.0), axis=0)[:, None]
    tile_start = jnp.arange(n_tiles, dtype=jnp.int32) * tm
    tile_expert = jnp.minimum(jnp.sum(tile_start[:, None] >= ends[None, :], axis=1), N_EXPERTS - 1)
    rows_in_tile = (starts + counts)[tile_expert] - tile_start
    quarter = tm // MOE_QUARTERS
    tile_quarters = jnp.clip((rows_in_tile + quarter - 1) // quarter, 0, MOE_QUARTERS).astype(jnp.int32)
    nvalid = (ends[-1:] // tm).astype(jnp.int32)
    last_tile = jnp.maximum(ends - tm, 0).astype(jnp.int32)
    has_rows = (counts > 0).astype(jnp.int32)
    return (pos_a, pos_b, gate_a, gate_b, tile_expert.astype(jnp.int32), tile_quarters, nvalid,
            last_tile, has_rows)


def kernel(x_prompt, x_sample, state_pool, cache_k, cache_v, meta_tokens, g_pool, w_pool, ls_pool, g_ffn0, w_ff1, w_ff3, w_ff2, g_kv, w_kv, b_kv, g_attn, w_q, b_q, sinks, w_o, b_o, g_ffn1, w_router, w_e1, w_e3, w_e2, g_final):
    n_seq, seq, _ = x_prompt.shape
    n_dec, n_t, _ = x_sample.shape
    past_len = PAST_LEN
    window = cache_k.shape[1]
    assert window == WINDOW and seq % POOL_TILE == 0 and N_META <= POOL_TILE

    vec = lambda a: a.reshape(1, -1).astype(F32)
    g_pool, ls_pool, g_ffn0, g_kv, b_kv, g_attn, b_q, b_o, g_ffn1, g_final = map(
        vec, (g_pool, ls_pool, g_ffn0, g_kv, b_kv, g_attn, b_q, b_o, g_ffn1, g_final))
    w_pool, w_ff1, w_ff3, w_ff2, w_kv, w_q, w_o = (
        w.astype(BF16) for w in (w_pool, w_ff1, w_ff3, w_ff2, w_kv, w_q, w_o))
    w_router = jnp.pad(w_router, ((0, 0), (0, LANES - N_EXPERTS))).astype(BF16)
    meta = meta_tokens.astype(F32)

    def layer0_tail(h1, tm_ffn, tm_proj):
        h2 = _ffn(h1, g_ffn0, w_ff1, w_ff3, w_ff2, min(tm_ffn, h1.shape[0]))
        kv, q = _qkv(h2, g_kv, w_kv, b_kv, g_attn, w_q, b_q, tm_proj)
        return h2, kv, q

    x_meta = jnp.pad(meta, ((0, POOL_TILE - N_META), (0, 0)))[None]
    h1_m, _ = _pool_seq(x_meta, jnp.zeros_like(meta), g_pool, w_pool, ls_pool, 0)
    _, kv_m, _ = layer0_tail(h1_m[0], POOL_TILE, POOL_TILE)
    past_kv = jnp.pad(kv_m[:N_META], ((WINDOW - N_META, 0), (0, 0)))

    h1_p, pool_p = _pool_seq(x_prompt, meta, g_pool, w_pool, ls_pool, N_META)
    h2_p, kv_p, q_p = layer0_tail(h1_p.reshape(n_seq * seq, D_MODEL), FFN_TILE, PROJ_TILE)
    o_p = _attn_seq(q_p, kv_p, past_kv, sinks.astype(F32), n_seq, N_META)
    h3_p, u_p, comb_p, sel_p = _oproj(o_p, h2_p, w_o, b_o, g_ffn1, w_router, PROJ_TILE)
    kv_tail = kv_p.reshape(n_seq, seq, 2 * HKV)[:, seq - WINDOW:]
    k_p = kv_tail[..., :HKV].reshape(n_seq, WINDOW, N_KV_HEADS, HEAD_DIM)
    v_p = kv_tail[..., HKV:].reshape(n_seq, WINDOW, N_KV_HEADS, HEAD_DIM)

    h1_s, pool_s = _pool_step(x_sample.transpose(1, 0, 2), state_pool.transpose(1, 0, 2),
                              g_pool, w_pool, ls_pool, past_len)
    n_s = n_t * n_dec
    h2_s, kv_s, q_s = layer0_tail(h1_s.reshape(n_s, D_MODEL), FFN_TILE, PROJ_TILE)
    q5 = q_s.reshape(n_t, n_dec, N_KV_HEADS, GROUP, HEAD_DIM).transpose(1, 2, 0, 3, 4)
    eye = jnp.eye(N_KV_HEADS, dtype=BF16)
    q_bd = (q5[:, :, :, :, None, :] * eye[None, :, None, None, :, None]).reshape(
        n_dec, N_KV_HEADS * n_t * GROUP, HKV)
    new_kv = jnp.pad(kv_s.reshape(n_t, n_dec, 2 * HKV).transpose(1, 0, 2), ((0, 0), (0, 16 - n_t), (0, 0)))
    sink_rows = jnp.broadcast_to(sinks.astype(F32).reshape(N_KV_HEADS, 1, GROUP),
                                 (N_KV_HEADS, n_t, GROUP)).reshape(-1, 1)
    o_s4, k_s, v_s = _attn_step(q_bd, new_kv, cache_k.reshape(n_dec, WINDOW, HKV),
                                cache_v.reshape(n_dec, WINDOW, HKV), sink_rows, past_len, n_t)
    o_s = o_s4.reshape(n_dec, N_KV_HEADS, n_t, GROUP, HEAD_DIM).transpose(2, 0, 1, 3, 4).reshape(n_s, D_MODEL)
    h3_s, u_s, comb_s, sel_s = _oproj(o_s, h2_s, w_o, b_o, g_ffn1, w_router, PROJ_TILE)

    n_tok = n_seq * seq + n_s
    min_tiles = 2 * n_tok // MOE_TILE
    n_tiles = min_tiles + N_EXPERTS
    pos_a, pos_b, gate_a, gate_b, tile_expert, tile_quarters, nvalid, last_tile, has_rows = _route_tables(
        jnp.concatenate([sel_p, sel_s]), jnp.concatenate([comb_p, comb_s]), n_tiles)
    xs = _dispatch(pos_a, pos_b, last_tile, has_rows, nvalid, u_p, u_s, n_tiles, min_tiles)
    y = _moe(tile_expert, tile_quarters, nvalid, xs, w_e1.astype(F32), w_e3.astype(F32), w_e2.astype(F32))
    n_p = n_seq * seq
    y_p = _combine(pos_a, pos_b, h3_p, gate_a[:n_p], gate_b[:n_p], g_final, y, 0)
    y_s = _combine(pos_a, pos_b, h3_s, gate_a[n_p:], gate_b[n_p:], g_final, y, n_p)

    return (y_p.reshape(n_seq, seq, D_MODEL),
            y_s.reshape(n_t, n_dec, D_MODEL).transpose(1, 0, 2),
            pool_p,
            pool_s.transpose(1, 0, 2),
            k_p, v_p,
            k_s.reshape(n_dec, WINDOW, N_KV_HEADS, HEAD_DIM),
            v_s.reshape(n_dec, WINDOW, N_KV_HEADS, HEAD_DIM))
```

```python
import functools

import jax
import jax.numpy as jnp
from jax import lax
from jax.experimental import pallas as pl
from jax.experimental.pallas import tpu as pltpu

F32 = jnp.float32
BF16 = jnp.bfloat16

D_MODEL = 2048
N_META = 16
POOL_WINDOWS = (2, 4, 8, 16)
POOL_GROUP_DIM = D_MODEL // len(POOL_WINDOWS)
POOL_STATE = max(POOL_WINDOWS) - 1
HEAD_DIM = 64
N_HEADS = D_MODEL // HEAD_DIM
N_KV_HEADS = 4
GROUP = N_HEADS // N_KV_HEADS
HKV = N_KV_HEADS * HEAD_DIM
WINDOW = 128
N_EXPERTS = 8
EPS = 1e-5
PAST_LEN = 8192
MASKED = -1e30

LANES = 128
V7X_VMEM_BYTES = 64 * 2 ** 20
VMEM_LIMIT = V7X_VMEM_BYTES - 8 * 2 ** 20

POOL_TILE = 128
ATTN_TILE = WINDOW
MOE_TILE = 1024
MOE_QUARTERS = 4
MOE_FF_TILE = 512
FFN_FF_TILE = 512
FFN_TILE = 1024
PROJ_TILE = 256
ROW_DMA_TILE = 256


def _dot(a, b):
    return jnp.dot(a, b, preferred_element_type=F32)


def _dot_nt(a, b):
    return lax.dot_general(a, b, (((1,), (1,)), ((), ())), preferred_element_type=F32)


def _rmsnorm(x, g):
    return x * lax.rsqrt(jnp.mean(x * x, axis=-1, keepdims=True) + EPS) * g


def _params(n_axes):
    return pltpu.CompilerParams(dimension_semantics=("arbitrary",) * n_axes,
                                vmem_limit_bytes=VMEM_LIMIT)


def _pool_seq_kernel(x_ref, meta_ref, g_ref, w_ref, ls_ref, h_ref, st_ref, hi_ref, lo_ref, u_ref, *, pos0):
    t = pl.program_id(1)
    tt = POOL_TILE
    g = g_ref[...]

    def split(rows):
        hi = rows.astype(BF16)
        return hi, (rows - hi.astype(F32)).astype(BF16)

    @pl.when(t == 0)
    def _():
        zeros = jnp.zeros((tt - N_META, D_MODEL), BF16)
        hi_ref[0:tt - N_META, :] = zeros
        lo_ref[0:tt - N_META, :] = zeros
        hi_ref[tt - N_META:tt, :], lo_ref[tt - N_META:tt, :] = split(_rmsnorm(meta_ref[...], g))

    @pl.when(t > 0)
    def _():
        hi_ref[0:tt, :] = hi_ref[tt:2 * tt, :]
        lo_ref[0:tt, :] = lo_ref[tt:2 * tt, :]

    x = x_ref[0]
    u = _rmsnorm(x, g)
    u_ref[...] = u
    hi_ref[tt:2 * tt, :], lo_ref[tt:2 * tt, :] = split(u)
    hi = hi_ref[...]
    lo = lo_ref[...]
    r = lax.broadcasted_iota(jnp.int32, (tt, 2 * tt), 0)
    c = lax.broadcasted_iota(jnp.int32, (tt, 2 * tt), 1)
    pos = pos0 + t * tt + lax.broadcasted_iota(jnp.int32, (tt, 1), 0)
    for gi, w in enumerate(POOL_WINDOWS):
        sl = slice(gi * POOL_GROUP_DIM, (gi + 1) * POOL_GROUP_DIM)
        band = jnp.where((c > r + tt - w) & (c <= r + tt), 1.0, 0.0).astype(BF16)
        win_sum = _dot(band, hi[:, sl]) + _dot(band, lo[:, sl])
        cnt = jnp.minimum(w, pos + 1).astype(F32)
        mix = win_sum / cnt - u[:, sl]
        o = _dot(mix.astype(BF16), w_ref[gi]) * ls_ref[:, sl]
        h_ref[0, :, sl] = x[:, sl] + o

    @pl.when(t == pl.num_programs(1) - 1)
    def _():
        st_ref[0] = u_ref[tt - POOL_STATE:tt, :]


def _pool_seq(x, meta, g, w, ls, pos0):
    b, t, _ = x.shape
    tt = POOL_TILE
    return pl.pallas_call(
        functools.partial(_pool_seq_kernel, pos0=pos0),
        grid=(b, t // tt),
        in_specs=[
            pl.BlockSpec((1, tt, D_MODEL), lambda i, j: (i, j, 0)),
            pl.BlockSpec((N_META, D_MODEL), lambda i, j: (0, 0)),
            pl.BlockSpec((1, D_MODEL), lambda i, j: (0, 0)),
            pl.BlockSpec((len(POOL_WINDOWS), POOL_GROUP_DIM, POOL_GROUP_DIM), lambda i, j: (0, 0, 0)),
            pl.BlockSpec((1, D_MODEL), lambda i, j: (0, 0)),
        ],
        out_specs=[
            pl.BlockSpec((1, tt, D_MODEL), lambda i, j: (i, j, 0)),
            pl.BlockSpec((1, POOL_STATE, D_MODEL), lambda i, j: (i, 0, 0)),
        ],
        out_shape=[
            jax.ShapeDtypeStruct((b, t, D_MODEL), F32),
            jax.ShapeDtypeStruct((b, POOL_STATE, D_MODEL), F32),
        ],
        scratch_shapes=[pltpu.VMEM((2 * tt, D_MODEL), BF16), pltpu.VMEM((2 * tt, D_MODEL), BF16),
                        pltpu.VMEM((tt, D_MODEL), F32)],
        compiler_params=_params(2),
        name="pool_seq",
    )(x, meta, g, w, ls)


def _pool_step_kernel(x_ref, st_ref, g_ref, w_ref, ls_ref, h_ref, nst_ref, e_ref, *, pos0):
    n_t, bb, _ = x_ref.shape
    g = g_ref[...]
    for j in range(POOL_STATE):
        e_ref[j] = st_ref[j]
    for t in range(n_t):
        e_ref[POOL_STATE + t] = _rmsnorm(x_ref[t], g)
    for j in range(POOL_STATE):
        nst_ref[j] = e_ref[j + n_t]
    for gi, w in enumerate(POOL_WINDOWS):
        sl = slice(gi * POOL_GROUP_DIM, (gi + 1) * POOL_GROUP_DIM)
        mixes = []
        for t in range(n_t):
            win_sum = e_ref[POOL_STATE + t, :, sl]
            for i in range(1, w):
                win_sum = win_sum + e_ref[POOL_STATE + t - i, :, sl]
            cnt = float(min(w, pos0 + t + 1))
            mixes.append(win_sum / cnt - e_ref[POOL_STATE + t, :, sl])
        mix = jnp.concatenate(mixes, axis=0)
        o = _dot(mix.astype(BF16), w_ref[gi]) * ls_ref[:, sl]
        for t in range(n_t):
            h_ref[t, :, sl] = x_ref[t, :, sl] + o[t * bb:(t + 1) * bb]


def _pool_step(x_t, st_t, g, w, ls, pos0, bb=32):
    n_t, b, _ = x_t.shape
    return pl.pallas_call(
        functools.partial(_pool_step_kernel, pos0=pos0),
        grid=(b // bb,),
        in_specs=[
            pl.BlockSpec((n_t, bb, D_MODEL), lambda i: (0, i, 0)),
            pl.BlockSpec((POOL_STATE, bb, D_MODEL), lambda i: (0, i, 0)),
            pl.BlockSpec((1, D_MODEL), lambda i: (0, 0)),
            pl.BlockSpec((len(POOL_WINDOWS), POOL_GROUP_DIM, POOL_GROUP_DIM), lambda i: (0, 0, 0)),
            pl.BlockSpec((1, D_MODEL), lambda i: (0, 0)),
        ],
        out_specs=[
            pl.BlockSpec((n_t, bb, D_MODEL), lambda i: (0, i, 0)),
            pl.BlockSpec((POOL_STATE, bb, D_MODEL), lambda i: (0, i, 0)),
        ],
        out_shape=[
            jax.ShapeDtypeStruct((n_t, b, D_MODEL), F32),
            jax.ShapeDtypeStruct((POOL_STATE, b, D_MODEL), F32),
        ],
        scratch_shapes=[pltpu.VMEM((POOL_STATE + n_t, bb, D_MODEL), F32)],
        compiler_params=_params(1),
        name="pool_step",
    )(x_t, st_t, g, w, ls)


def _swiglu_skewed(k, n_chunks, parts, w1, w3, w2, first, last=lambda: None):
    def up():
        w_gate, w_lin = w1(), w3()
        for u_ref, h_ref, _ in parts:
            u = u_ref[...]
            h_ref[...] = (jax.nn.silu(_dot(u, w_gate)) * _dot(u, w_lin)).astype(h_ref.dtype)

    def down():
        w_out = w2()
        for _, h_ref, acc_ref in parts:
            acc_ref[...] += _dot(h_ref[...], w_out)

    @pl.when(k == 0)
    def _():
        first()
        up()

    @pl.when((k > 0) & (k < n_chunks))
    def _():
        down()
        up()

    @pl.when(k == n_chunks)
    def _():
        last()
        down()


def _ffn_kernel(x_ref, g_ref, w1_ref, w3_ref, w2_ref, o_ref, u_ref, h_ref):
    def first():
        x = x_ref[...]
        u_ref[...] = _rmsnorm(x, g_ref[...]).astype(BF16)
        o_ref[...] = x

    _swiglu_skewed(pl.program_id(1), pl.num_programs(1) - 1, [(u_ref, h_ref, o_ref)],
                   lambda: w1_ref[...], lambda: w3_ref[...], lambda: w2_ref[...], first)


def _ffn(x, g, w1, w3, w2, tm):
    m = x.shape[0]
    f = w1.shape[1]
    tf = FFN_FF_TILE
    kf = f // tf
    return pl.pallas_call(
        _ffn_kernel,
        grid=(m // tm, kf + 1),
        in_specs=[
            pl.BlockSpec((tm, D_MODEL), lambda i, k: (i, 0)),
            pl.BlockSpec((1, D_MODEL), lambda i, k: (0, 0)),
            pl.BlockSpec((D_MODEL, tf), lambda i, k: (0, jnp.minimum(k, kf - 1))),
            pl.BlockSpec((D_MODEL, tf), lambda i, k: (0, jnp.minimum(k, kf - 1))),
            pl.BlockSpec((tf, D_MODEL), lambda i, k: (jnp.maximum(k - 1, 0), 0)),
        ],
        out_specs=pl.BlockSpec((tm, D_MODEL), lambda i, k: (i, 0)),
        out_shape=jax.ShapeDtypeStruct((m, D_MODEL), F32),
        scratch_shapes=[pltpu.VMEM((tm, D_MODEL), BF16), pltpu.VMEM((tm, tf), BF16)],
        compiler_params=_params(2),
        name="ffn0",
    )(x, g, w1, w3, w2)


def _qkv_kernel(x_ref, gkv_ref, wkv_ref, bkv_ref, gq_ref, wq_ref, bq_ref, kv_ref, q_ref):
    x = x_ref[...]
    xn = x * lax.rsqrt(jnp.mean(x * x, axis=-1, keepdims=True) + EPS)
    kv_ref[...] = _dot(xn * gkv_ref[...], wkv_ref[...]) + bkv_ref[...]
    q = _dot(xn * gq_ref[...], wq_ref[...]) + bq_ref[...]
    q_ref[...] = (q * HEAD_DIM ** -0.5).astype(BF16)


def _qkv(x, gkv, wkv, bkv, gq, wq, bq, tm):
    m = x.shape[0]
    row = lambda i: (i, 0)
    fixed = lambda i: (0, 0)
    return pl.pallas_call(
        _qkv_kernel,
        grid=(m // tm,),
        in_specs=[
            pl.BlockSpec((tm, D_MODEL), row),
            pl.BlockSpec((1, D_MODEL), fixed),
            pl.BlockSpec((D_MODEL, 2 * HKV), fixed, pipeline_mode=pl.Buffered(1)),
            pl.BlockSpec((1, 2 * HKV), fixed),
            pl.BlockSpec((1, D_MODEL), fixed),
            pl.BlockSpec((D_MODEL, D_MODEL), fixed, pipeline_mode=pl.Buffered(1)),
            pl.BlockSpec((1, D_MODEL), fixed),
        ],
        out_specs=[pl.BlockSpec((tm, 2 * HKV), row), pl.BlockSpec((tm, D_MODEL), row)],
        out_shape=[jax.ShapeDtypeStruct((m, 2 * HKV), F32), jax.ShapeDtypeStruct((m, D_MODEL), BF16)],
        compiler_params=_params(1),
        name="qkv",
    )(x, gkv, wkv, bkv, gq, wq, bq)


def _attn_seq_kernel(sinks_ref, q_ref, cur_ref, prev_ref, past_ref, o_ref, *, pos0):
    j = pl.program_id(1)
    tq = ATTN_TILE
    prev = jnp.where(j == 0, past_ref[...], prev_ref[...])
    band = jnp.concatenate([prev, cur_ref[...]], axis=0)
    c = lax.broadcasted_iota(jnp.int32, (2 * tq, tq), 0)
    r = lax.broadcasted_iota(jnp.int32, (2 * tq, tq), 1)
    key_pos = pos0 + (j - 1) * tq + c
    valid = (c > r) & (c <= r + tq) & (key_pos >= 0)
    low = lax.broadcasted_iota(jnp.int32, (2 * tq, LANES), 1) < HEAD_DIM
    zeros = jnp.zeros((2 * tq, LANES), F32)
    k_bd, v_bd_t = [], []
    v_t = band[:, HKV:].T
    zv = jnp.zeros((HEAD_DIM, 2 * tq), F32)
    for kvh in range(N_KV_HEADS):
        tile = band[:, (kvh // 2) * LANES:(kvh // 2 + 1) * LANES]
        swapped = pltpu.roll(tile, HEAD_DIM, axis=1)
        in_low, in_high = (tile, swapped) if kvh % 2 == 0 else (swapped, tile)
        k_bd.append(jnp.concatenate([jnp.where(low, in_low, zeros), jnp.where(low, zeros, in_high)],
                                    axis=0).astype(BF16))
        vt = v_t[kvh * HEAD_DIM:(kvh + 1) * HEAD_DIM]
        v_bd_t.append(jnp.concatenate([jnp.concatenate([vt, zv], axis=1),
                                       jnp.concatenate([zv, vt], axis=1)], axis=0).astype(BF16))
    for pair in range(N_HEADS // 2):
        kvh = (2 * pair) // GROUP
        ps = slice(pair * LANES, (pair + 1) * LANES)
        s_t = _dot_nt(k_bd[kvh], q_ref[:, ps])
        probs, inv = [], []
        for half in range(2):
            s = jnp.where(valid, s_t[half * 2 * tq:(half + 1) * 2 * tq], MASKED)
            sink = sinks_ref[2 * pair + half]
            m = jnp.maximum(jnp.max(s, axis=0, keepdims=True), sink)
            p = jnp.exp(s - m)
            denom = jnp.sum(p, axis=0, keepdims=True) + jnp.exp(sink - m)
            probs.append(p.astype(BF16))
            inv.append(jnp.broadcast_to(1.0 / denom, (HEAD_DIM, tq)))
        o_t = _dot(v_bd_t[kvh], jnp.concatenate(probs, axis=0)) * jnp.concatenate(inv, axis=0)
        o_ref[:, ps] = o_t.T.astype(BF16)


def _attn_seq(q, kv, past_kv, sinks, n_seq, pos0):
    m = q.shape[0]
    tq = ATTN_TILE
    nb = m // n_seq // tq
    return pl.pallas_call(
        functools.partial(_attn_seq_kernel, pos0=pos0),
        grid=(n_seq, nb),
        in_specs=[
            pl.BlockSpec(memory_space=pltpu.SMEM),
            pl.BlockSpec((tq, D_MODEL), lambda b, j: (b * nb + j, 0)),
            pl.BlockSpec((tq, 2 * HKV), lambda b, j: (b * nb + j, 0)),
            pl.BlockSpec((tq, 2 * HKV), lambda b, j: (b * nb + jnp.maximum(j - 1, 0), 0)),
            pl.BlockSpec((tq, 2 * HKV), lambda b, j: (0, 0)),
        ],
        out_specs=pl.BlockSpec((tq, D_MODEL), lambda b, j: (b * nb + j, 0)),
        out_shape=jax.ShapeDtypeStruct((m, D_MODEL), BF16),
        compiler_params=_params(2),
        name="attn_seq",
    )(sinks, q, kv, kv, past_kv)


def _attn_step_kernel(q_ref, new_ref, ck_ref, cv_ref, sink_ref, o_ref, nk_ref, nv_ref, *, pos0, n_t):
    bb = q_ref.shape[0]
    rows = N_KV_HEADS * n_t * GROUP
    pad = WINDOW - new_ref.shape[1]
    rho = lax.broadcasted_iota(jnp.int32, (rows, 2 * WINDOW), 0)
    c = lax.broadcasted_iota(jnp.int32, (rows, 2 * WINDOW), 1)
    t = (rho // GROUP) % n_t
    valid = (c > t) & (c <= t + WINDOW) & (pos0 - WINDOW + c >= 0)
    sink = sink_ref[...]
    for i in range(bb):
        new = jnp.concatenate([new_ref[i], jnp.zeros((pad, 2 * HKV), F32)], axis=0)
        kcat = jnp.concatenate([ck_ref[i], new[:, :HKV]], axis=0).astype(BF16)
        vcat = jnp.concatenate([cv_ref[i], new[:, HKV:]], axis=0).astype(BF16)
        s = jnp.where(valid, _dot_nt(q_ref[i], kcat), MASKED)
        m = jnp.maximum(jnp.max(s, axis=-1, keepdims=True), sink)
        p = jnp.exp(s - m)
        denom = jnp.sum(p, axis=-1, keepdims=True) + jnp.exp(sink - m)
        o = (_dot(p.astype(BF16), vcat) / denom).astype(BF16)
        for kvh in range(N_KV_HEADS):
            rs = n_t * GROUP
            o_ref[i, kvh] = o[kvh * rs:(kvh + 1) * rs, kvh * HEAD_DIM:(kvh + 1) * HEAD_DIM]
        nk_ref[i, 0:WINDOW - n_t, :] = ck_ref[i, n_t:WINDOW, :]
        nk_ref[i, WINDOW - n_t:WINDOW, :] = new_ref[i, 0:n_t, 0:HKV]
        nv_ref[i, 0:WINDOW - n_t, :] = cv_ref[i, n_t:WINDOW, :]
        nv_ref[i, WINDOW - n_t:WINDOW, :] = new_ref[i, 0:n_t, HKV:2 * HKV]


def _attn_step(q_bd, new_kv, cache_k, cache_v, sink_rows, pos0, n_t, bb=8):
    b = q_bd.shape[0]
    rows = q_bd.shape[1]
    n_new = new_kv.shape[1]
    blk = lambda *s: pl.BlockSpec((bb,) + s, lambda i: (i,) + (0,) * len(s))
    return pl.pallas_call(
        functools.partial(_attn_step_kernel, pos0=pos0, n_t=n_t),
        grid=(b // bb,),
        in_specs=[
            blk(rows, HKV),
            blk(n_new, 2 * HKV),
            blk(WINDOW, HKV),
            blk(WINDOW, HKV),
            pl.BlockSpec((rows, 1), lambda i: (0, 0)),
        ],
        out_specs=[blk(N_KV_HEADS, n_t * GROUP, HEAD_DIM), blk(WINDOW, HKV), blk(WINDOW, HKV)],
        out_shape=[
            jax.ShapeDtypeStruct((b, N_KV_HEADS, n_t * GROUP, HEAD_DIM), BF16),
            jax.ShapeDtypeStruct((b, WINDOW, HKV), F32),
            jax.ShapeDtypeStruct((b, WINDOW, HKV), F32),
        ],
        compiler_params=_params(1),
        name="attn_step",
    )(q_bd, new_kv, cache_k, cache_v, sink_rows)


def _oproj_kernel(o_ref, h_ref, wo_ref, bo_ref, g_ref, wr_ref, h3_ref, u_ref, comb_ref, sel_ref):
    h3 = h_ref[...] + (_dot(o_ref[...].astype(F32), wo_ref[...]) + bo_ref[...])
    h3_ref[...] = h3
    u = _rmsnorm(h3, g_ref[...])
    u_ref[...] = u
    logits = _dot(u.astype(BF16), wr_ref[...])
    lane = lax.broadcasted_iota(jnp.int32, logits.shape, 1)
    lg = jnp.where(lane < N_EXPERTS, logits, -jnp.inf)
    m1 = jnp.max(lg, axis=-1, keepdims=True)
    i1 = jnp.min(jnp.where(lg == m1, lane, LANES), axis=-1, keepdims=True)
    top1 = lane == i1
    lg2 = jnp.where(top1, -jnp.inf, lg)
    m2 = jnp.max(lg2, axis=-1, keepdims=True)
    i2 = jnp.min(jnp.where(lg2 == m2, lane, LANES), axis=-1, keepdims=True)
    top2 = lane == i2
    e2 = jnp.exp(m2 - m1)
    denom = 1.0 + e2
    comb = jnp.where(top1, 1.0 / denom, 0.0) + jnp.where(top2, e2 / denom, 0.0)
    comb_ref[...] = comb[:, :N_EXPERTS]
    sel_ref[...] = jnp.where(top1 | top2, 1, 0).astype(jnp.int32)[:, :N_EXPERTS]


def _oproj(o, h, wo, bo, g, wr, tm):
    m = o.shape[0]
    row = lambda i: (i, 0)
    fixed = lambda i: (0, 0)
    return pl.pallas_call(
        _oproj_kernel,
        grid=(m // tm,),
        in_specs=[
            pl.BlockSpec((tm, D_MODEL), row),
            pl.BlockSpec((tm, D_MODEL), row),
            pl.BlockSpec((D_MODEL, D_MODEL), fixed, pipeline_mode=pl.Buffered(1)),
            pl.BlockSpec((1, D_MODEL), fixed),
            pl.BlockSpec((1, D_MODEL), fixed),
            pl.BlockSpec((D_MODEL, LANES), fixed),
        ],
        out_specs=[
            pl.BlockSpec((tm, D_MODEL), row),
            pl.BlockSpec((tm, D_MODEL), row),
            pl.BlockSpec((tm, N_EXPERTS), row),
            pl.BlockSpec((tm, N_EXPERTS), row),
        ],
        out_shape=[
            jax.ShapeDtypeStruct((m, D_MODEL), F32),
            jax.ShapeDtypeStruct((m, D_MODEL), F32),
            jax.ShapeDtypeStruct((m, N_EXPERTS), F32),
            jax.ShapeDtypeStruct((m, N_EXPERTS), jnp.int32),
        ],
        compiler_params=_params(1),
        name="oproj_router",
    )(o, h, wo, bo, g, wr)


def _row_copy(src, r, dst, p, sem):
    return pltpu.make_async_copy(src.at[pl.ds(r, 1)], dst.at[pl.ds(p, 1)], sem)


def _zero_fill_copies(zeros_ref, xs_ref, last_tile_ref, has_rows_ref, nvalid_ref, sem, n_tiles, min_tiles):
    tm = MOE_TILE
    pairs = []
    for e in range(N_EXPERTS):
        start = pl.multiple_of(last_tile_ref[e], tm)
        cp = pltpu.make_async_copy(zeros_ref, xs_ref.at[pl.ds(start, tm)], sem)
        pairs.append((has_rows_ref[e] != 0, cp))
    for tile in range(min_tiles, n_tiles):
        cp = pltpu.make_async_copy(zeros_ref, xs_ref.at[pl.ds(tile * tm, tm)], sem)
        pairs.append((tile >= nvalid_ref[0], cp))
    return pairs


def _dispatch_kernel(pos_a_ref, pos_b_ref, last_tile_ref, has_rows_ref, nvalid_ref,
                     up_ref, us_ref, xs_ref, zeros_ref, sem, zsem, *, n_blocks_p, n_tiles, min_tiles):
    i = pl.program_id(0)
    rt = ROW_DMA_TILE

    @pl.when(i == 0)
    def _():
        zeros_ref[...] = jnp.zeros(zeros_ref.shape, F32)
        pairs = _zero_fill_copies(zeros_ref, xs_ref, last_tile_ref, has_rows_ref, nvalid_ref,
                                  zsem, n_tiles, min_tiles)
        for cond, cp in pairs:
            @pl.when(cond)
            def _():
                cp.start()
        for cond, cp in pairs:
            @pl.when(cond)
            def _():
                cp.wait()

    def scatter(src_ref):
        for r in range(rt):
            tok = i * rt + r
            _row_copy(src_ref, r, xs_ref, pos_a_ref[tok], sem).start()
            _row_copy(src_ref, r, xs_ref, pos_b_ref[tok], sem).start()
        for _ in range(2):
            pltpu.make_async_copy(src_ref, xs_ref.at[pl.ds(0, rt)], sem).wait()

    @pl.when(i < n_blocks_p)
    def _():
        scatter(up_ref)

    @pl.when(i >= n_blocks_p)
    def _():
        scatter(us_ref)


def _dispatch(pos_a, pos_b, last_tile, has_rows, nvalid, u_p, u_s, n_tiles, min_tiles):
    rt = ROW_DMA_TILE
    nbp = u_p.shape[0] // rt
    nbs = u_s.shape[0] // rt
    grid_spec = pltpu.PrefetchScalarGridSpec(
        num_scalar_prefetch=5,
        grid=(nbp + nbs,),
        in_specs=[
            pl.BlockSpec((rt, D_MODEL), lambda i, *_: (jnp.minimum(i, nbp - 1), 0)),
            pl.BlockSpec((rt, D_MODEL), lambda i, *_: (jnp.maximum(i - nbp, 0), 0)),
        ],
        out_specs=pl.BlockSpec(memory_space=pl.ANY),
        scratch_shapes=[
            pltpu.VMEM((MOE_TILE, D_MODEL), F32),
            pltpu.SemaphoreType.DMA,
            pltpu.SemaphoreType.DMA,
        ],
    )
    return pl.pallas_call(
        functools.partial(_dispatch_kernel, n_blocks_p=nbp, n_tiles=n_tiles, min_tiles=min_tiles),
        grid_spec=grid_spec,
        out_shape=jax.ShapeDtypeStruct((n_tiles * MOE_TILE, D_MODEL), F32),
        compiler_params=_params(1),
        name="moe_dispatch",
    )(pos_a, pos_b, last_tile, has_rows, nvalid, u_p, u_s)


def _moe_kernel(te_ref, quarters_ref, nv_ref, xs_ref, w1_ref, w3_ref, w2_ref, y_ref, x_ref, h_ref, sem):
    i = pl.program_id(0)
    k = pl.program_id(1)
    tm = MOE_TILE
    valid = i < nv_ref[0]
    quarters = quarters_ref[i]
    n_chunks = pl.num_programs(1) - 1
    weights = (lambda: w1_ref[0], lambda: w3_ref[0], lambda: w2_ref[0])

    def fetch(tile):
        return pltpu.make_async_copy(xs_ref.at[pl.ds(pl.multiple_of(tile * tm, tm), tm)], x_ref, sem)

    def zero_tile():
        y_ref[...] = jnp.zeros(y_ref.shape, F32)

    def first():
        zero_tile()

        @pl.when(i == 0)
        def _():
            fetch(0).start()

        fetch(i).wait()

    def last():
        @pl.when(i + 1 < nv_ref[0])
        def _():
            fetch(i + 1).start()

    for n in range(1, MOE_QUARTERS + 1):
        rows = n * (tm // MOE_QUARTERS)
        part = (x_ref.at[pl.ds(0, rows)], h_ref.at[pl.ds(0, rows)], y_ref.at[pl.ds(0, rows)])

        @pl.when(valid & (quarters == n))
        def _():
            _swiglu_skewed(k, n_chunks, [part], *weights, first, last)

    @pl.when(jnp.logical_not(valid) & (k == 0))
    def _():
        zero_tile()


def _moe(tile_expert, tile_quarters, nvalid, xs, w1, w3, w2):
    tm, tf = MOE_TILE, MOE_FF_TILE
    n_tiles = xs.shape[0] // tm
    kf = w1.shape[2] // tf

    def out_row(i, k, te, full, nv):
        return (i, 0)

    def up(i, k, te, full, nv):
        return (te[jnp.minimum(i, nv[0] - 1)], 0, jnp.where(i < nv[0], jnp.minimum(k, kf - 1), kf - 1))

    def down(i, k, te, full, nv):
        return (te[jnp.minimum(i, nv[0] - 1)], jnp.where(i < nv[0], jnp.maximum(k - 1, 0), kf - 1), 0)

    grid_spec = pltpu.PrefetchScalarGridSpec(
        num_scalar_prefetch=3,
        grid=(n_tiles, kf + 1),
        in_specs=[
            pl.BlockSpec(memory_space=pl.ANY),
            pl.BlockSpec((1, D_MODEL, tf), up),
            pl.BlockSpec((1, D_MODEL, tf), up),
            pl.BlockSpec((1, tf, D_MODEL), down),
        ],
        out_specs=pl.BlockSpec((tm, D_MODEL), out_row),
        scratch_shapes=[
            pltpu.VMEM((tm, D_MODEL), F32),
            pltpu.VMEM((tm, tf), F32),
            pltpu.SemaphoreType.DMA,
        ],
    )
    return pl.pallas_call(
        _moe_kernel,
        grid_spec=grid_spec,
        out_shape=jax.ShapeDtypeStruct(xs.shape, F32),
        compiler_params=_params(2),
        name="moe_ffn",
    )(tile_expert, tile_quarters, nvalid, xs, w1, w3, w2)


def _combine_kernel(pos_a_ref, pos_b_ref, h_ref, ga_ref, gb_ref, g_ref, y_ref, o_ref, ya_ref, yb_ref, sem, *, base):
    i = pl.program_id(0)
    rt = ROW_DMA_TILE

    def fetch(step, slot):
        for r in range(rt):
            tok = base + step * rt + r
            _row_copy(y_ref, pos_a_ref[tok], ya_ref.at[slot], r, sem.at[slot]).start()
            _row_copy(y_ref, pos_b_ref[tok], yb_ref.at[slot], r, sem.at[slot]).start()

    @pl.when(i == 0)
    def _():
        fetch(0, 0)

    @pl.when(i + 1 < pl.num_programs(0))
    def _():
        fetch(i + 1, (i + 1) % 2)

    slot = i % 2
    pltpu.make_async_copy(y_ref.at[pl.ds(0, rt)], ya_ref.at[slot], sem.at[slot]).wait()
    pltpu.make_async_copy(y_ref.at[pl.ds(0, rt)], yb_ref.at[slot], sem.at[slot]).wait()
    moe = ga_ref[...] * ya_ref[slot] + gb_ref[...] * yb_ref[slot]
    o_ref[...] = _rmsnorm(h_ref[...] + moe, g_ref[...])


def _combine(pos_a, pos_b, h, gate_a, gate_b, g, y, base):
    rt = ROW_DMA_TILE
    m = h.shape[0]
    row = lambda i, *_: (i, 0)
    grid_spec = pltpu.PrefetchScalarGridSpec(
        num_scalar_prefetch=2,
        grid=(m // rt,),
        in_specs=[
            pl.BlockSpec((rt, D_MODEL), row),
            pl.BlockSpec((rt, 1), row),
            pl.BlockSpec((rt, 1), row),
            pl.BlockSpec((1, D_MODEL), lambda i, *_: (0, 0)),
            pl.BlockSpec(memory_space=pl.ANY),
        ],
        out_specs=pl.BlockSpec((rt, D_MODEL), row),
        scratch_shapes=[
            pltpu.VMEM((2, rt, D_MODEL), F32),
            pltpu.VMEM((2, rt, D_MODEL), F32),
            pltpu.SemaphoreType.DMA((2,)),
        ],
    )
    return pl.pallas_call(
        functools.partial(_combine_kernel, base=base),
        grid_spec=grid_spec,
        out_shape=jax.ShapeDtypeStruct((m, D_MODEL), F32),
        compiler_params=_params(1),
        name="moe_combine",
    )(pos_a, pos_b, h, gate_a, gate_b, g, y)


def _token_cumsum(sel):
    n_e, n_tok = sel.shape
    blocks = n_tok // LANES
    s = sel.astype(F32).reshape(n_e, blocks, LANES)
    idx = jnp.arange(LANES)
    within = jnp.einsum("ebl,lm->ebm", s, (idx[:, None] <= idx[None, :]).astype(F32),
                        precision=lax.Precision.HIGHEST)
    totals = within[:, :, -1]
    bidx = jnp.arange(blocks)
    offsets = jnp.einsum("eb,bc->ec", totals, (bidx[:, None] < bidx[None, :]).astype(F32),
                         precision=lax.Precision.HIGHEST)
    return (within + offsets[:, :, None]).astype(jnp.int32).reshape(n_e, n_tok)


def _route_tables(sel, comb, n_tiles):
    tm = MOE_TILE
    sel = sel.T
    comb = comb.T
    cum = _token_cumsum(sel)
    counts = cum[:, -1]
    padded = ((counts + tm - 1) // tm) * tm
    ends = jnp.cumsum(padded)
    starts = ends - padded
    slot = starts[:, None] + cum - sel
    order = jnp.cumsum(sel, axis=0)
    first = (sel == 1) & (order == 1)
    second = (sel == 1) & (order == 2)
    pos_a = jnp.sum(jnp.where(first, slot, 0), axis=0).astype(jnp.int32)
    pos_b = jnp.sum(jnp.where(second, slot, 0), axis=0).astype(jnp.int32)
    gate_a = jnp.sum(jnp.where(first, comb, 0.0), axis=0)[:, None]
    gate_b = jnp.sum(jnp.where(second, comb, 0.0), axis=0)[:, None]
    tile_start = jnp.arange(n_tiles, dtype=jnp.int32) * tm
    tile_expert = jnp.minimum(jnp.sum(tile_start[:, None] >= ends[None, :], axis=1), N_EXPERTS - 1)
    rows_in_tile = (starts + counts)[tile_expert] - tile_start
    quarter = tm // MOE_QUARTERS
    tile_quarters = jnp.clip((rows_in_tile + quarter - 1) // quarter, 0, MOE_QUARTERS).astype(jnp.int32)
    nvalid = (ends[-1:] // tm).astype(jnp.int32)
    last_tile = jnp.maximum(ends - tm, 0).astype(jnp.int32)
    has_rows = (counts > 0).astype(jnp.int32)
    return (pos_a, pos_b, gate_a, gate_b, tile_expert.astype(jnp.int32), tile_quarters, nvalid,
            last_tile, has_rows)


def kernel(x_prompt, x_sample, state_pool, cache_k, cache_v, meta_tokens, g_pool, w_pool, ls_pool, g_ffn0, w_ff1, w_ff3, w_ff2, g_kv, w_kv, b_kv, g_attn, w_q, b_q, sinks, w_o, b_o, g_ffn1, w_router, w_e1, w_e3, w_e2, g_final):
    n_seq, seq, _ = x_prompt.shape
    n_dec, n_t, _ = x_sample.shape
    past_len = PAST_LEN
    window = cache_k.shape[1]
    assert window == WINDOW and seq % POOL_TILE == 0 and N_META <= POOL_TILE

    vec = lambda a: a.reshape(1, -1).astype(F32)
    g_pool, ls_pool, g_ffn0, g_kv, b_kv, g_attn, b_q, b_o, g_ffn1, g_final = map(
        vec, (g_pool, ls_pool, g_ffn0, g_kv, b_kv, g_attn, b_q, b_o, g_ffn1, g_final))
    w_pool, w_ff1, w_ff3, w_ff2 = (w.astype(BF16) for w in (w_pool, w_ff1, w_ff3, w_ff2))
    w_kv, w_q, w_o = (w.astype(F32) for w in (w_kv, w_q, w_o))
    w_router = jnp.pad(w_router, ((0, 0), (0, LANES - N_EXPERTS))).astype(BF16)
    meta = meta_tokens.astype(F32)

    def layer0_tail(h1, tm_ffn, tm_proj):
        h2 = _ffn(h1, g_ffn0, w_ff1, w_ff3, w_ff2, min(tm_ffn, h1.shape[0]))
        kv, q = _qkv(h2, g_kv, w_kv, b_kv, g_attn, w_q, b_q, tm_proj)
        return h2, kv, q

    x_meta = jnp.pad(meta, ((0, POOL_TILE - N_META), (0, 0)))[None]
    h1_m, _ = _pool_seq(x_meta, jnp.zeros_like(meta), g_pool, w_pool, ls_pool, 0)
    _, kv_m, _ = layer0_tail(h1_m[0], POOL_TILE, POOL_TILE)
    past_kv = jnp.pad(kv_m[:N_META], ((WINDOW - N_META, 0), (0, 0)))

    h1_p, pool_p = _pool_seq(x_prompt, meta, g_pool, w_pool, ls_pool, N_META)
    h2_p, kv_p, q_p = layer0_tail(h1_p.reshape(n_seq * seq, D_MODEL), FFN_TILE, PROJ_TILE)
    o_p = _attn_seq(q_p, kv_p, past_kv, sinks.astype(F32), n_seq, N_META)
    h3_p, u_p, comb_p, sel_p = _oproj(o_p, h2_p, w_o, b_o, g_ffn1, w_router, PROJ_TILE)
    kv_tail = kv_p.reshape(n_seq, seq, 2 * HKV)[:, seq - WINDOW:]
    k_p = kv_tail[..., :HKV].reshape(n_seq, WINDOW, N_KV_HEADS, HEAD_DIM)
    v_p = kv_tail[..., HKV:].reshape(n_seq, WINDOW, N_KV_HEADS, HEAD_DIM)

    h1_s, pool_s = _pool_step(x_sample.transpose(1, 0, 2), state_pool.transpose(1, 0, 2),
                              g_pool, w_pool, ls_pool, past_len)
    n_s = n_t * n_dec
    h2_s, kv_s, q_s = layer0_tail(h1_s.reshape(n_s, D_MODEL), FFN_TILE, PROJ_TILE)
    q5 = q_s.reshape(n_t, n_dec, N_KV_HEADS, GROUP, HEAD_DIM).transpose(1, 2, 0, 3, 4)
    eye = jnp.eye(N_KV_HEADS, dtype=BF16)
    q_bd = (q5[:, :, :, :, None, :] * eye[None, :, None, None, :, None]).reshape(
        n_dec, N_KV_HEADS * n_t * GROUP, HKV)
    new_kv = jnp.pad(kv_s.reshape(n_t, n_dec, 2 * HKV).transpose(1, 0, 2), ((0, 0), (0, 16 - n_t), (0, 0)))
    sink_rows = jnp.broadcast_to(sinks.astype(F32).reshape(N_KV_HEADS, 1, GROUP),
                                 (N_KV_HEADS, n_t, GROUP)).reshape(-1, 1)
    o_s4, k_s, v_s = _attn_step(q_bd, new_kv, cache_k.reshape(n_dec, WINDOW, HKV),
                                cache_v.reshape(n_dec, WINDOW, HKV), sink_rows, past_len, n_t)
    o_s = o_s4.reshape(n_dec, N_KV_HEADS, n_t, GROUP, HEAD_DIM).transpose(2, 0, 1, 3, 4).reshape(n_s, D_MODEL)
    h3_s, u_s, comb_s, sel_s = _oproj(o_s, h2_s, w_o, b_o, g_ffn1, w_router, PROJ_TILE)

    n_tok = n_seq * seq + n_s
    min_tiles = 2 * n_tok // MOE_TILE
    n_tiles = min_tiles + N_EXPERTS
    pos_a, pos_b, gate_a, gate_b, tile_expert, tile_quarters, nvalid, last_tile, has_rows = _route_tables(
        jnp.concatenate([sel_p, sel_s]), jnp.concatenate([comb_p, comb_s]), n_tiles)
    xs = _dispatch(pos_a, pos_b, last_tile, has_rows, nvalid, u_p, u_s, n_tiles, min_tiles)
    y = _moe(tile_expert, tile_quarters, nvalid, xs, w_e1.astype(F32), w_e3.astype(F32), w_e2.astype(F32))
    n_p = n_seq * seq
    y_p = _combine(pos_a, pos_b, h3_p, gate_a[:n_p], gate_b[:n_p], g_final, y, 0)
    y_s = _combine(pos_a, pos_b, h3_s, gate_a[n_p:], gate_b[n_p:], g_final, y, n_p)

    return (y_p.reshape(n_seq, seq, D_MODEL),
            y_s.reshape(n_t, n_dec, D_MODEL).transpose(1, 0, 2),
            pool_p,
            pool_s.transpose(1, 0, 2),
            k_p, v_p,
            k_s.reshape(n_dec, WINDOW, N_KV_HEADS, HEAD_DIM),
            v_s.reshape(n_dec, WINDOW, N_KV_HEADS, HEAD_DIM))
```

```python
import functools

import jax
import jax.numpy as jnp
from jax import lax
from jax.experimental import pallas as pl
from jax.experimental.pallas import tpu as pltpu

F32 = jnp.float32
BF16 = jnp.bfloat16

D_MODEL = 2048
N_META = 16
POOL_WINDOWS = (2, 4, 8, 16)
POOL_GROUP_DIM = D_MODEL // len(POOL_WINDOWS)
POOL_STATE = max(POOL_WINDOWS) - 1
HEAD_DIM = 64
N_HEADS = D_MODEL // HEAD_DIM
N_KV_HEADS = 4
GROUP = N_HEADS // N_KV_HEADS
HKV = N_KV_HEADS * HEAD_DIM
WINDOW = 128
N_EXPERTS = 8
EPS = 1e-5
PAST_LEN = 8192
MASKED = -1e30

LANES = 128
V7X_VMEM_BYTES = 64 * 2 ** 20
VMEM_LIMIT = V7X_VMEM_BYTES - 8 * 2 ** 20

POOL_TILE = 128
ATTN_TILE = WINDOW
MOE_TILE = 1024
MOE_QUARTERS = 4
MOE_FF_TILE = 512
FFN_FF_TILE = 512
FFN_TILE = 1024
PROJ_TILE = 256
ROW_DMA_TILE = 256


def _dot(a, b):
    return jnp.dot(a, b, preferred_element_type=F32)


def _dot_nt(a, b):
    return lax.dot_general(a, b, (((1,), (1,)), ((), ())), preferred_element_type=F32)


def _rmsnorm(x, g):
    return x * lax.rsqrt(jnp.mean(x * x, axis=-1, keepdims=True) + EPS) * g


def _params(n_axes):
    return pltpu.CompilerParams(dimension_semantics=("arbitrary",) * n_axes,
                                vmem_limit_bytes=VMEM_LIMIT)


def _pool_seq_kernel(x_ref, meta_ref, g_ref, w_ref, ls_ref, h_ref, st_ref, hi_ref, lo_ref, u_ref, *, pos0):
    t = pl.program_id(1)
    tt = POOL_TILE
    g = g_ref[...]

    def split(rows):
        hi = rows.astype(BF16)
        return hi, (rows - hi.astype(F32)).astype(BF16)

    @pl.when(t == 0)
    def _():
        zeros = jnp.zeros((tt - N_META, D_MODEL), BF16)
        hi_ref[0:tt - N_META, :] = zeros
        lo_ref[0:tt - N_META, :] = zeros
        hi_ref[tt - N_META:tt, :], lo_ref[tt - N_META:tt, :] = split(_rmsnorm(meta_ref[...], g))

    @pl.when(t > 0)
    def _():
        hi_ref[0:tt, :] = hi_ref[tt:2 * tt, :]
        lo_ref[0:tt, :] = lo_ref[tt:2 * tt, :]

    x = x_ref[0]
    u = _rmsnorm(x, g)
    u_ref[...] = u
    hi_ref[tt:2 * tt, :], lo_ref[tt:2 * tt, :] = split(u)
    hi = hi_ref[...]
    lo = lo_ref[...]
    r = lax.broadcasted_iota(jnp.int32, (tt, 2 * tt), 0)
    c = lax.broadcasted_iota(jnp.int32, (tt, 2 * tt), 1)
    pos = pos0 + t * tt + lax.broadcasted_iota(jnp.int32, (tt, 1), 0)
    for gi, w in enumerate(POOL_WINDOWS):
        sl = slice(gi * POOL_GROUP_DIM, (gi + 1) * POOL_GROUP_DIM)
        band = jnp.where((c > r + tt - w) & (c <= r + tt), 1.0, 0.0).astype(BF16)
        win_sum = _dot(band, hi[:, sl]) + _dot(band, lo[:, sl])
        cnt = jnp.minimum(w, pos + 1).astype(F32)
        mix = win_sum / cnt - u[:, sl]
        o = _dot(mix.astype(BF16), w_ref[gi]) * ls_ref[:, sl]
        h_ref[0, :, sl] = x[:, sl] + o

    @pl.when(t == pl.num_programs(1) - 1)
    def _():
        st_ref[0] = u_ref[tt - POOL_STATE:tt, :]


def _pool_seq(x, meta, g, w, ls, pos0):
    b, t, _ = x.shape
    tt = POOL_TILE
    return pl.pallas_call(
        functools.partial(_pool_seq_kernel, pos0=pos0),
        grid=(b, t // tt),
        in_specs=[
            pl.BlockSpec((1, tt, D_MODEL), lambda i, j: (i, j, 0)),
            pl.BlockSpec((N_META, D_MODEL), lambda i, j: (0, 0)),
            pl.BlockSpec((1, D_MODEL), lambda i, j: (0, 0)),
            pl.BlockSpec((len(POOL_WINDOWS), POOL_GROUP_DIM, POOL_GROUP_DIM), lambda i, j: (0, 0, 0)),
            pl.BlockSpec((1, D_MODEL), lambda i, j: (0, 0)),
        ],
        out_specs=[
            pl.BlockSpec((1, tt, D_MODEL), lambda i, j: (i, j, 0)),
            pl.BlockSpec((1, POOL_STATE, D_MODEL), lambda i, j: (i, 0, 0)),
        ],
        out_shape=[
            jax.ShapeDtypeStruct((b, t, D_MODEL), F32),
            jax.ShapeDtypeStruct((b, POOL_STATE, D_MODEL), F32),
        ],
        scratch_shapes=[pltpu.VMEM((2 * tt, D_MODEL), BF16), pltpu.VMEM((2 * tt, D_MODEL), BF16),
                        pltpu.VMEM((tt, D_MODEL), F32)],
        compiler_params=_params(2),
        name="pool_seq",
    )(x, meta, g, w, ls)


def _pool_step_kernel(x_ref, st_ref, g_ref, w_ref, ls_ref, h_ref, nst_ref, e_ref, *, pos0):
    n_t, bb, _ = x_ref.shape
    g = g_ref[...]
    for j in range(POOL_STATE):
        e_ref[j] = st_ref[j]
    for t in range(n_t):
        e_ref[POOL_STATE + t] = _rmsnorm(x_ref[t], g)
    for j in range(POOL_STATE):
        nst_ref[j] = e_ref[j + n_t]
    for gi, w in enumerate(POOL_WINDOWS):
        sl = slice(gi * POOL_GROUP_DIM, (gi + 1) * POOL_GROUP_DIM)
        mixes = []
        for t in range(n_t):
            win_sum = e_ref[POOL_STATE + t, :, sl]
            for i in range(1, w):
                win_sum = win_sum + e_ref[POOL_STATE + t - i, :, sl]
            cnt = float(min(w, pos0 + t + 1))
            mixes.append(win_sum / cnt - e_ref[POOL_STATE + t, :, sl])
        mix = jnp.concatenate(mixes, axis=0)
        o = _dot(mix.astype(BF16), w_ref[gi]) * ls_ref[:, sl]
        for t in range(n_t):
            h_ref[t, :, sl] = x_ref[t, :, sl] + o[t * bb:(t + 1) * bb]


def _pool_step(x_t, st_t, g, w, ls, pos0, bb=32):
    n_t, b, _ = x_t.shape
    return pl.pallas_call(
        functools.partial(_pool_step_kernel, pos0=pos0),
        grid=(b // bb,),
        in_specs=[
            pl.BlockSpec((n_t, bb, D_MODEL), lambda i: (0, i, 0)),
            pl.BlockSpec((POOL_STATE, bb, D_MODEL), lambda i: (0, i, 0)),
            pl.BlockSpec((1, D_MODEL), lambda i: (0, 0)),
            pl.BlockSpec((len(POOL_WINDOWS), POOL_GROUP_DIM, POOL_GROUP_DIM), lambda i: (0, 0, 0)),
            pl.BlockSpec((1, D_MODEL), lambda i: (0, 0)),
        ],
        out_specs=[
            pl.BlockSpec((n_t, bb, D_MODEL), lambda i: (0, i, 0)),
            pl.BlockSpec((POOL_STATE, bb, D_MODEL), lambda i: (0, i, 0)),
        ],
        out_shape=[
            jax.ShapeDtypeStruct((n_t, b, D_MODEL), F32),
            jax.ShapeDtypeStruct((POOL_STATE, b, D_MODEL), F32),
        ],
        scratch_shapes=[pltpu.VMEM((POOL_STATE + n_t, bb, D_MODEL), F32)],
        compiler_params=_params(1),
        name="pool_step",
    )(x_t, st_t, g, w, ls)


def _swiglu_skewed(k, n_chunks, parts, w1, w3, w2, first, last=lambda: None):
    def up():
        w_gate, w_lin = w1(), w3()
        for u_ref, h_ref, _ in parts:
            u = u_ref[...]
            h_ref[...] = (jax.nn.silu(_dot(u, w_gate)) * _dot(u, w_lin)).astype(h_ref.dtype)

    def down():
        w_out = w2()
        for _, h_ref, acc_ref in parts:
            acc_ref[...] += _dot(h_ref[...], w_out)

    @pl.when(k == 0)
    def _():
        first()
        up()

    @pl.when((k > 0) & (k < n_chunks))
    def _():
        down()
        up()

    @pl.when(k == n_chunks)
    def _():
        last()
        down()


def _ffn_kernel(x_ref, g_ref, w1_ref, w3_ref, w2_ref, o_ref, u_ref, h_ref):
    def first():
        x = x_ref[...]
        u_ref[...] = _rmsnorm(x, g_ref[...]).astype(BF16)
        o_ref[...] = x

    _swiglu_skewed(pl.program_id(1), pl.num_programs(1) - 1, [(u_ref, h_ref, o_ref)],
                   lambda: w1_ref[...], lambda: w3_ref[...], lambda: w2_ref[...], first)


def _ffn(x, g, w1, w3, w2, tm):
    m = x.shape[0]
    f = w1.shape[1]
    tf = FFN_FF_TILE
    kf = f // tf
    return pl.pallas_call(
        _ffn_kernel,
        grid=(m // tm, kf + 1),
        in_specs=[
            pl.BlockSpec((tm, D_MODEL), lambda i, k: (i, 0)),
            pl.BlockSpec((1, D_MODEL), lambda i, k: (0, 0)),
            pl.BlockSpec((D_MODEL, tf), lambda i, k: (0, jnp.minimum(k, kf - 1))),
            pl.BlockSpec((D_MODEL, tf), lambda i, k: (0, jnp.minimum(k, kf - 1))),
            pl.BlockSpec((tf, D_MODEL), lambda i, k: (jnp.maximum(k - 1, 0), 0)),
        ],
        out_specs=pl.BlockSpec((tm, D_MODEL), lambda i, k: (i, 0)),
        out_shape=jax.ShapeDtypeStruct((m, D_MODEL), F32),
        scratch_shapes=[pltpu.VMEM((tm, D_MODEL), BF16), pltpu.VMEM((tm, tf), BF16)],
        compiler_params=_params(2),
        name="ffn0",
    )(x, g, w1, w3, w2)


def _qkv_kernel(x_ref, gkv_ref, wkv_ref, bkv_ref, gq_ref, wq_ref, bq_ref, kv_ref, q_ref):
    x = x_ref[...]
    xn = x * lax.rsqrt(jnp.mean(x * x, axis=-1, keepdims=True) + EPS)
    kv_ref[...] = _dot(xn * gkv_ref[...], wkv_ref[...]) + bkv_ref[...]
    q = _dot(xn * gq_ref[...], wq_ref[...]) + bq_ref[...]
    q_ref[...] = (q * HEAD_DIM ** -0.5).astype(BF16)


def _qkv(x, gkv, wkv, bkv, gq, wq, bq, tm):
    m = x.shape[0]
    row = lambda i: (i, 0)
    fixed = lambda i: (0, 0)
    return pl.pallas_call(
        _qkv_kernel,
        grid=(m // tm,),
        in_specs=[
            pl.BlockSpec((tm, D_MODEL), row),
            pl.BlockSpec((1, D_MODEL), fixed),
            pl.BlockSpec((D_MODEL, 2 * HKV), fixed, pipeline_mode=pl.Buffered(1)),
            pl.BlockSpec((1, 2 * HKV), fixed),
            pl.BlockSpec((1, D_MODEL), fixed),
            pl.BlockSpec((D_MODEL, D_MODEL), fixed, pipeline_mode=pl.Buffered(1)),
            pl.BlockSpec((1, D_MODEL), fixed),
        ],
        out_specs=[pl.BlockSpec((tm, 2 * HKV), row), pl.BlockSpec((tm, D_MODEL), row)],
        out_shape=[jax.ShapeDtypeStruct((m, 2 * HKV), F32), jax.ShapeDtypeStruct((m, D_MODEL), BF16)],
        compiler_params=_params(1),
        name="qkv",
    )(x, gkv, wkv, bkv, gq, wq, bq)


def _attn_seq_kernel(sinks_ref, q_ref, cur_ref, prev_ref, past_ref, o_ref, *, pos0):
    j = pl.program_id(1)
    tq = ATTN_TILE
    prev = jnp.where(j == 0, past_ref[...], prev_ref[...])
    band = jnp.concatenate([prev, cur_ref[...]], axis=0)
    c = lax.broadcasted_iota(jnp.int32, (2 * tq, tq), 0)
    r = lax.broadcasted_iota(jnp.int32, (2 * tq, tq), 1)
    key_pos = pos0 + (j - 1) * tq + c
    valid = (c > r) & (c <= r + tq) & (key_pos >= 0)
    low = lax.broadcasted_iota(jnp.int32, (2 * tq, LANES), 1) < HEAD_DIM
    zeros = jnp.zeros((2 * tq, LANES), F32)
    k_bd, v_bd_t = [], []
    v_t = band[:, HKV:].T
    zv = jnp.zeros((HEAD_DIM, 2 * tq), F32)
    for kvh in range(N_KV_HEADS):
        tile = band[:, (kvh // 2) * LANES:(kvh // 2 + 1) * LANES]
        swapped = pltpu.roll(tile, HEAD_DIM, axis=1)
        in_low, in_high = (tile, swapped) if kvh % 2 == 0 else (swapped, tile)
        k_bd.append(jnp.concatenate([jnp.where(low, in_low, zeros), jnp.where(low, zeros, in_high)],
                                    axis=0).astype(BF16))
        vt = v_t[kvh * HEAD_DIM:(kvh + 1) * HEAD_DIM]
        v_bd_t.append(jnp.concatenate([jnp.concatenate([vt, zv], axis=1),
                                       jnp.concatenate([zv, vt], axis=1)], axis=0).astype(BF16))
    for pair in range(N_HEADS // 2):
        kvh = (2 * pair) // GROUP
        ps = slice(pair * LANES, (pair + 1) * LANES)
        s_t = _dot_nt(k_bd[kvh], q_ref[:, ps])
        probs, inv = [], []
        for half in range(2):
            s = jnp.where(valid, s_t[half * 2 * tq:(half + 1) * 2 * tq], MASKED)
            sink = sinks_ref[2 * pair + half]
            m = jnp.maximum(jnp.max(s, axis=0, keepdims=True), sink)
            p = jnp.exp(s - m)
            denom = jnp.sum(p, axis=0, keepdims=True) + jnp.exp(sink - m)
            probs.append(p.astype(BF16))
            inv.append(jnp.broadcast_to(1.0 / denom, (HEAD_DIM, tq)))
        o_t = _dot(v_bd_t[kvh], jnp.concatenate(probs, axis=0)) * jnp.concatenate(inv, axis=0)
        o_ref[:, ps] = o_t.T.astype(BF16)


def _attn_seq(q, kv, past_kv, sinks, n_seq, pos0):
    m = q.shape[0]
    tq = ATTN_TILE
    nb = m // n_seq // tq
    return pl.pallas_call(
        functools.partial(_attn_seq_kernel, pos0=pos0),
        grid=(n_seq, nb),
        in_specs=[
            pl.BlockSpec(memory_space=pltpu.SMEM),
            pl.BlockSpec((tq, D_MODEL), lambda b, j: (b * nb + j, 0)),
            pl.BlockSpec((tq, 2 * HKV), lambda b, j: (b * nb + j, 0)),
            pl.BlockSpec((tq, 2 * HKV), lambda b, j: (b * nb + jnp.maximum(j - 1, 0), 0)),
            pl.BlockSpec((tq, 2 * HKV), lambda b, j: (0, 0)),
        ],
        out_specs=pl.BlockSpec((tq, D_MODEL), lambda b, j: (b * nb + j, 0)),
        out_shape=jax.ShapeDtypeStruct((m, D_MODEL), BF16),
        compiler_params=_params(2),
        name="attn_seq",
    )(sinks, q, kv, kv, past_kv)


def _attn_step_kernel(q_ref, new_ref, ck_ref, cv_ref, sink_ref, o_ref, nk_ref, nv_ref, *, pos0, n_t):
    bb = q_ref.shape[0]
    rows = N_KV_HEADS * n_t * GROUP
    pad = WINDOW - new_ref.shape[1]
    rho = lax.broadcasted_iota(jnp.int32, (rows, 2 * WINDOW), 0)
    c = lax.broadcasted_iota(jnp.int32, (rows, 2 * WINDOW), 1)
    t = (rho // GROUP) % n_t
    valid = (c > t) & (c <= t + WINDOW) & (pos0 - WINDOW + c >= 0)
    sink = sink_ref[...]
    for i in range(bb):
        new = jnp.concatenate([new_ref[i], jnp.zeros((pad, 2 * HKV), F32)], axis=0)
        kcat = jnp.concatenate([ck_ref[i], new[:, :HKV]], axis=0).astype(BF16)
        vcat = jnp.concatenate([cv_ref[i], new[:, HKV:]], axis=0).astype(BF16)
        s = jnp.where(valid, _dot_nt(q_ref[i], kcat), MASKED)
        m = jnp.maximum(jnp.max(s, axis=-1, keepdims=True), sink)
        p = jnp.exp(s - m)
        denom = jnp.sum(p, axis=-1, keepdims=True) + jnp.exp(sink - m)
        o = (_dot(p.astype(BF16), vcat) / denom).astype(BF16)
        for kvh in range(N_KV_HEADS):
            rs = n_t * GROUP
            o_ref[i, kvh] = o[kvh * rs:(kvh + 1) * rs, kvh * HEAD_DIM:(kvh + 1) * HEAD_DIM]
        nk_ref[i, 0:WINDOW - n_t, :] = ck_ref[i, n_t:WINDOW, :]
        nk_ref[i, WINDOW - n_t:WINDOW, :] = new_ref[i, 0:n_t, 0:HKV]
        nv_ref[i, 0:WINDOW - n_t, :] = cv_ref[i, n_t:WINDOW, :]
        nv_ref[i, WINDOW - n_t:WINDOW, :] = new_ref[i, 0:n_t, HKV:2 * HKV]


def _attn_step(q_bd, new_kv, cache_k, cache_v, sink_rows, pos0, n_t, bb=8):
    b = q_bd.shape[0]
    rows = q_bd.shape[1]
    n_new = new_kv.shape[1]
    blk = lambda *s: pl.BlockSpec((bb,) + s, lambda i: (i,) + (0,) * len(s))
    return pl.pallas_call(
        functools.partial(_attn_step_kernel, pos0=pos0, n_t=n_t),
        grid=(b // bb,),
        in_specs=[
            blk(rows, HKV),
            blk(n_new, 2 * HKV),
            blk(WINDOW, HKV),
            blk(WINDOW, HKV),
            pl.BlockSpec((rows, 1), lambda i: (0, 0)),
        ],
        out_specs=[blk(N_KV_HEADS, n_t * GROUP, HEAD_DIM), blk(WINDOW, HKV), blk(WINDOW, HKV)],
        out_shape=[
            jax.ShapeDtypeStruct((b, N_KV_HEADS, n_t * GROUP, HEAD_DIM), BF16),
            jax.ShapeDtypeStruct((b, WINDOW, HKV), F32),
            jax.ShapeDtypeStruct((b, WINDOW, HKV), F32),
        ],
        compiler_params=_params(1),
        name="attn_step",
    )(q_bd, new_kv, cache_k, cache_v, sink_rows)


def _oproj_kernel(o_ref, h_ref, wo_ref, bo_ref, g_ref, wr_ref, h3_ref, u_ref, comb_ref, sel_ref):
    h3 = h_ref[...] + (_dot(o_ref[...].astype(F32), wo_ref[...]) + bo_ref[...])
    h3_ref[...] = h3
    u = _rmsnorm(h3, g_ref[...])
    u_ref[...] = u
    logits = _dot(u.astype(BF16), wr_ref[...])
    lane = lax.broadcasted_iota(jnp.int32, logits.shape, 1)
    lg = jnp.where(lane < N_EXPERTS, logits, -jnp.inf)
    m1 = jnp.max(lg, axis=-1, keepdims=True)
    i1 = jnp.min(jnp.where(lg == m1, lane, LANES), axis=-1, keepdims=True)
    top1 = lane == i1
    lg2 = jnp.where(top1, -jnp.inf, lg)
    m2 = jnp.max(lg2, axis=-1, keepdims=True)
    i2 = jnp.min(jnp.where(lg2 == m2, lane, LANES), axis=-1, keepdims=True)
    top2 = lane == i2
    e2 = jnp.exp(m2 - m1)
    denom = 1.0 + e2
    comb = jnp.where(top1, 1.0 / denom, 0.0) + jnp.where(top2, e2 / denom, 0.0)
    comb_ref[...] = comb[:, :N_EXPERTS]
    sel_ref[...] = jnp.where(top1 | top2, 1, 0).astype(jnp.int32)[:, :N_EXPERTS]


def _oproj(o, h, wo, bo, g, wr, tm):
    m = o.shape[0]
    row = lambda i: (i, 0)
    fixed = lambda i: (0, 0)
    return pl.pallas_call(
        _oproj_kernel,
        grid=(m // tm,),
        in_specs=[
            pl.BlockSpec((tm, D_MODEL), row),
            pl.BlockSpec((tm, D_MODEL), row),
            pl.BlockSpec((D_MODEL, D_MODEL), fixed, pipeline_mode=pl.Buffered(1)),
            pl.BlockSpec((1, D_MODEL), fixed),
            pl.BlockSpec((1, D_MODEL), fixed),
            pl.BlockSpec((D_MODEL, LANES), fixed),
        ],
        out_specs=[
            pl.BlockSpec((tm, D_MODEL), row),
            pl.BlockSpec((tm, D_MODEL), row),
            pl.BlockSpec((tm, N_EXPERTS), row),
            pl.BlockSpec((tm, N_EXPERTS), row),
        ],
        out_shape=[
            jax.ShapeDtypeStruct((m, D_MODEL), F32),
            jax.ShapeDtypeStruct((m, D_MODEL), F32),
            jax.ShapeDtypeStruct((m, N_EXPERTS), F32),
            jax.ShapeDtypeStruct((m, N_EXPERTS), jnp.int32),
        ],
        compiler_params=_params(1),
        name="oproj_router",
    )(o, h, wo, bo, g, wr)


def _row_copy(src, r, dst, p, sem):
    return pltpu.make_async_copy(src.at[pl.ds(r, 1)], dst.at[pl.ds(p, 1)], sem)


def _zero_fill_copies(zeros_ref, xs_ref, last_tile_ref, has_rows_ref, nvalid_ref, sem, n_tiles, min_tiles):
    tm = MOE_TILE
    pairs = []
    for e in range(N_EXPERTS):
        start = pl.multiple_of(last_tile_ref[e], tm)
        cp = pltpu.make_async_copy(zeros_ref, xs_ref.at[pl.ds(start, tm)], sem)
        pairs.append((has_rows_ref[e] != 0, cp))
    for tile in range(min_tiles, n_tiles):
        cp = pltpu.make_async_copy(zeros_ref, xs_ref.at[pl.ds(tile * tm, tm)], sem)
        pairs.append((tile >= nvalid_ref[0], cp))
    return pairs


def _dispatch_kernel(pos_a_ref, pos_b_ref, last_tile_ref, has_rows_ref, nvalid_ref,
                     up_ref, us_ref, xs_ref, zeros_ref, sem, zsem, *, n_blocks_p, n_tiles, min_tiles):
    i = pl.program_id(0)
    rt = ROW_DMA_TILE

    @pl.when(i == 0)
    def _():
        zeros_ref[...] = jnp.zeros(zeros_ref.shape, F32)
        pairs = _zero_fill_copies(zeros_ref, xs_ref, last_tile_ref, has_rows_ref, nvalid_ref,
                                  zsem, n_tiles, min_tiles)
        for cond, cp in pairs:
            @pl.when(cond)
            def _():
                cp.start()
        for cond, cp in pairs:
            @pl.when(cond)
            def _():
                cp.wait()

    def scatter(src_ref):
        for r in range(rt):
            tok = i * rt + r
            _row_copy(src_ref, r, xs_ref, pos_a_ref[tok], sem).start(priority=0)
            _row_copy(src_ref, r, xs_ref, pos_b_ref[tok], sem).start(priority=1)
        for _ in range(2):
            pltpu.make_async_copy(src_ref, xs_ref.at[pl.ds(0, rt)], sem).wait()

    @pl.when(i < n_blocks_p)
    def _():
        scatter(up_ref)

    @pl.when(i >= n_blocks_p)
    def _():
        scatter(us_ref)


def _dispatch(pos_a, pos_b, last_tile, has_rows, nvalid, u_p, u_s, n_tiles, min_tiles):
    rt = ROW_DMA_TILE
    nbp = u_p.shape[0] // rt
    nbs = u_s.shape[0] // rt
    grid_spec = pltpu.PrefetchScalarGridSpec(
        num_scalar_prefetch=5,
        grid=(nbp + nbs,),
        in_specs=[
            pl.BlockSpec((rt, D_MODEL), lambda i, *_: (jnp.minimum(i, nbp - 1), 0)),
            pl.BlockSpec((rt, D_MODEL), lambda i, *_: (jnp.maximum(i - nbp, 0), 0)),
        ],
        out_specs=pl.BlockSpec(memory_space=pl.ANY),
        scratch_shapes=[
            pltpu.VMEM((MOE_TILE, D_MODEL), F32),
            pltpu.SemaphoreType.DMA,
            pltpu.SemaphoreType.DMA,
        ],
    )
    return pl.pallas_call(
        functools.partial(_dispatch_kernel, n_blocks_p=nbp, n_tiles=n_tiles, min_tiles=min_tiles),
        grid_spec=grid_spec,
        out_shape=jax.ShapeDtypeStruct((n_tiles * MOE_TILE, D_MODEL), F32),
        compiler_params=_params(1),
        name="moe_dispatch",
    )(pos_a, pos_b, last_tile, has_rows, nvalid, u_p, u_s)


def _moe_kernel(te_ref, quarters_ref, nv_ref, xs_ref, w1_ref, w3_ref, w2_ref, y_ref, x_ref, h_ref, sem):
    i = pl.program_id(0)
    k = pl.program_id(1)
    tm = MOE_TILE
    valid = i < nv_ref[0]
    quarters = quarters_ref[i]
    n_chunks = pl.num_programs(1) - 1
    weights = (lambda: w1_ref[0], lambda: w3_ref[0], lambda: w2_ref[0])

    def fetch(tile):
        return pltpu.make_async_copy(xs_ref.at[pl.ds(pl.multiple_of(tile * tm, tm), tm)], x_ref, sem)

    def zero_tile():
        y_ref[...] = jnp.zeros(y_ref.shape, F32)

    def first():
        zero_tile()

        @pl.when(i == 0)
        def _():
            fetch(0).start()

        fetch(i).wait()

    def last():
        @pl.when(i + 1 < nv_ref[0])
        def _():
            fetch(i + 1).start()

    for n in range(1, MOE_QUARTERS + 1):
        rows = n * (tm // MOE_QUARTERS)
        part = (x_ref.at[pl.ds(0, rows)], h_ref.at[pl.ds(0, rows)], y_ref.at[pl.ds(0, rows)])

        @pl.when(valid & (quarters == n))
        def _():
            _swiglu_skewed(k, n_chunks, [part], *weights, first, last)

    @pl.when(jnp.logical_not(valid) & (k == 0))
    def _():
        zero_tile()


def _moe(tile_expert, tile_quarters, nvalid, xs, w1, w3, w2):
    tm, tf = MOE_TILE, MOE_FF_TILE
    n_tiles = xs.shape[0] // tm
    kf = w1.shape[2] // tf

    def out_row(i, k, te, full, nv):
        return (i, 0)

    def up(i, k, te, full, nv):
        return (te[jnp.minimum(i, nv[0] - 1)], 0, jnp.where(i < nv[0], jnp.minimum(k, kf - 1), kf - 1))

    def down(i, k, te, full, nv):
        return (te[jnp.minimum(i, nv[0] - 1)], jnp.where(i < nv[0], jnp.maximum(k - 1, 0), kf - 1), 0)

    grid_spec = pltpu.PrefetchScalarGridSpec(
        num_scalar_prefetch=3,
        grid=(n_tiles, kf + 1),
        in_specs=[
            pl.BlockSpec(memory_space=pl.ANY),
            pl.BlockSpec((1, D_MODEL, tf), up),
            pl.BlockSpec((1, D_MODEL, tf), up),
            pl.BlockSpec((1, tf, D_MODEL), down),
        ],
        out_specs=pl.BlockSpec((tm, D_MODEL), out_row),
        scratch_shapes=[
            pltpu.VMEM((tm, D_MODEL), F32),
            pltpu.VMEM((tm, tf), F32),
            pltpu.SemaphoreType.DMA,
        ],
    )
    return pl.pallas_call(
        _moe_kernel,
        grid_spec=grid_spec,
        out_shape=jax.ShapeDtypeStruct(xs.shape, F32),
        compiler_params=_params(2),
        name="moe_ffn",
    )(tile_expert, tile_quarters, nvalid, xs, w1, w3, w2)


def _combine_kernel(pos_a_ref, pos_b_ref, h_ref, ga_ref, gb_ref, g_ref, y_ref, o_ref, ya_ref, yb_ref, sem, *, base):
    i = pl.program_id(0)
    rt = ROW_DMA_TILE

    def fetch(step, slot):
        for r in range(rt):
            tok = base + step * rt + r
            _row_copy(y_ref, pos_a_ref[tok], ya_ref.at[slot], r, sem.at[slot]).start(priority=0)
            _row_copy(y_ref, pos_b_ref[tok], yb_ref.at[slot], r, sem.at[slot]).start(priority=1)

    @pl.when(i == 0)
    def _():
        fetch(0, 0)

    @pl.when(i + 1 < pl.num_programs(0))
    def _():
        fetch(i + 1, (i + 1) % 2)

    slot = i % 2
    pltpu.make_async_copy(y_ref.at[pl.ds(0, rt)], ya_ref.at[slot], sem.at[slot]).wait()
    pltpu.make_async_copy(y_ref.at[pl.ds(0, rt)], yb_ref.at[slot], sem.at[slot]).wait()
    moe = ga_ref[...] * ya_ref[slot] + gb_ref[...] * yb_ref[slot]
    o_ref[...] = _rmsnorm(h_ref[...] + moe, g_ref[...])


def _combine(pos_a, pos_b, h, gate_a, gate_b, g, y, base):
    rt = ROW_DMA_TILE
    m = h.shape[0]
    row = lambda i, *_: (i, 0)
    grid_spec = pltpu.PrefetchScalarGridSpec(
        num_scalar_prefetch=2,
        grid=(m // rt,),
        in_specs=[
            pl.BlockSpec((rt, D_MODEL), row),
            pl.BlockSpec((rt, 1), row),
            pl.BlockSpec((rt, 1), row),
            pl.BlockSpec((1, D_MODEL), lambda i, *_: (0, 0)),
            pl.BlockSpec(memory_space=pl.ANY),
        ],
        out_specs=pl.BlockSpec((rt, D_MODEL), row),
        scratch_shapes=[
            pltpu.VMEM((2, rt, D_MODEL), F32),
            pltpu.VMEM((2, rt, D_MODEL), F32),
            pltpu.SemaphoreType.DMA((2,)),
        ],
    )
    return pl.pallas_call(
        functools.partial(_combine_kernel, base=base),
        grid_spec=grid_spec,
        out_shape=jax.ShapeDtypeStruct((m, D_MODEL), F32),
        compiler_params=_params(1),
        name="moe_combine",
    )(pos_a, pos_b, h, gate_a, gate_b, g, y)


def _token_cumsum(sel):
    n_e, n_tok = sel.shape
    blocks = n_tok // LANES
    s = sel.astype(F32).reshape(n_e, blocks, LANES)
    idx = jnp.arange(LANES)
    within = jnp.einsum("ebl,lm->ebm", s, (idx[:, None] <= idx[None, :]).astype(F32),
                        precision=lax.Precision.HIGHEST)
    totals = within[:, :, -1]
    bidx = jnp.arange(blocks)
    offsets = jnp.einsum("eb,bc->ec", totals, (bidx[:, None] < bidx[None, :]).astype(F32),
                         precision=lax.Precision.HIGHEST)
    return (within + offsets[:, :, None]).astype(jnp.int32).reshape(n_e, n_tok)


def _route_tables(sel, comb, n_tiles):
    tm = MOE_TILE
    sel = sel.T
    comb = comb.T
    cum = _token_cumsum(sel)
    counts = cum[:, -1]
    padded = ((counts + tm - 1) // tm) * tm
    ends = jnp.cumsum(padded)
    starts = ends - padded
    slot = starts[:, None] + cum - sel
    order = jnp.cumsum(sel, axis=0)
    first = (sel == 1) & (order == 1)
    second = (sel == 1) & (order == 2)
    pos_a = jnp.sum(jnp.where(first, slot, 0), axis=0).astype(jnp.int32)
    pos_b = jnp.sum(jnp.where(second, slot, 0), axis=0).astype(jnp.int32)
    gate_a = jnp.sum(jnp.where(first, comb, 0.0), axis=0)[:, None]
    gate_b = jnp.sum(jnp.where(second, comb, 0.0), axis=0)[:, None]
    tile_start = jnp.arange(n_tiles, dtype=jnp.int32) * tm
    tile_expert = jnp.minimum(jnp.sum(tile_start[:, None] >= ends[None, :], axis=1), N_EXPERTS - 1)
    rows_in_tile = (starts + counts)[tile_expert] - tile_start
    quarter = tm // MOE_QUARTERS
    tile_quarters = jnp.clip((rows_in_tile + quarter - 1) // quarter, 0, MOE_QUARTERS).astype(jnp.int32)
    nvalid = (ends[-1:] // tm).astype(jnp.int32)
    last_tile = jnp.maximum(ends - tm, 0).astype(jnp.int32)
    has_rows = (counts > 0).astype(jnp.int32)
    return (pos_a, pos_b, gate_a, gate_b, tile_expert.astype(jnp.int32), tile_quarters, nvalid,
            last_tile, has_rows)


def kernel(x_prompt, x_sample, state_pool, cache_k, cache_v, meta_tokens, g_pool, w_pool, ls_pool, g_ffn0, w_ff1, w_ff3, w_ff2, g_kv, w_kv, b_kv, g_attn, w_q, b_q, sinks, w_o, b_o, g_ffn1, w_router, w_e1, w_e3, w_e2, g_final):
    n_seq, seq, _ = x_prompt.shape
    n_dec, n_t, _ = x_sample.shape
    past_len = PAST_LEN
    window = cache_k.shape[1]
    assert window == WINDOW and seq % POOL_TILE == 0 and N_META <= POOL_TILE

    vec = lambda a: a.reshape(1, -1).astype(F32)
    g_pool, ls_pool, g_ffn0, g_kv, b_kv, g_attn, b_q, b_o, g_ffn1, g_final = map(
        vec, (g_pool, ls_pool, g_ffn0, g_kv, b_kv, g_attn, b_q, b_o, g_ffn1, g_final))
    w_pool, w_ff1, w_ff3, w_ff2 = (w.astype(BF16) for w in (w_pool, w_ff1, w_ff3, w_ff2))
    w_kv, w_q, w_o = (w.astype(F32) for w in (w_kv, w_q, w_o))
    w_router = jnp.pad(w_router, ((0, 0), (0, LANES - N_EXPERTS))).astype(BF16)
    meta = meta_tokens.astype(F32)

    def layer0_tail(h1, tm_ffn, tm_proj):
        h2 = _ffn(h1, g_ffn0, w_ff1, w_ff3, w_ff2, min(tm_ffn, h1.shape[0]))
        kv, q = _qkv(h2, g_kv, w_kv, b_kv, g_attn, w_q, b_q, tm_proj)
        return h2, kv, q

    x_meta = jnp.pad(meta, ((0, POOL_TILE - N_META), (0, 0)))[None]
    h1_m, _ = _pool_seq(x_meta, jnp.zeros_like(meta), g_pool, w_pool, ls_pool, 0)

    h1_s, pool_s = _pool_step(x_sample.transpose(1, 0, 2), state_pool.transpose(1, 0, 2),
                              g_pool, w_pool, ls_pool, past_len)
    n_s = n_t * n_dec
    h1_ms = jnp.concatenate([h1_m[0], h1_s.reshape(n_s, D_MODEL)])
    h2_ms, kv_ms, q_ms = layer0_tail(h1_ms, h1_ms.shape[0], POOL_TILE)
    h2_s, kv_s, q_s = h2_ms[POOL_TILE:], kv_ms[POOL_TILE:], q_ms[POOL_TILE:]
    past_kv = jnp.pad(kv_ms[:N_META], ((WINDOW - N_META, 0), (0, 0)))

    h1_p, pool_p = _pool_seq(x_prompt, meta, g_pool, w_pool, ls_pool, N_META)
    h2_p, kv_p, q_p = layer0_tail(h1_p.reshape(n_seq * seq, D_MODEL), FFN_TILE, PROJ_TILE)
    o_p = _attn_seq(q_p, kv_p, past_kv, sinks.astype(F32), n_seq, N_META)
    h3_p, u_p, comb_p, sel_p = _oproj(o_p, h2_p, w_o, b_o, g_ffn1, w_router, PROJ_TILE)
    kv_tail = kv_p.reshape(n_seq, seq, 2 * HKV)[:, seq - WINDOW:]
    k_p = kv_tail[..., :HKV].reshape(n_seq, WINDOW, N_KV_HEADS, HEAD_DIM)
    v_p = kv_tail[..., HKV:].reshape(n_seq, WINDOW, N_KV_HEADS, HEAD_DIM)

    q5 = q_s.reshape(n_t, n_dec, N_KV_HEADS, GROUP, HEAD_DIM).transpose(1, 2, 0, 3, 4)
    eye = jnp.eye(N_KV_HEADS, dtype=BF16)
    q_bd = (q5[:, :, :, :, None, :] * eye[None, :, None, None, :, None]).reshape(
        n_dec, N_KV_HEADS * n_t * GROUP, HKV)
    new_kv = jnp.pad(kv_s.reshape(n_t, n_dec, 2 * HKV).transpose(1, 0, 2), ((0, 0), (0, 16 - n_t), (0, 0)))
    sink_rows = jnp.broadcast_to(sinks.astype(F32).reshape(N_KV_HEADS, 1, GROUP),
                                 (N_KV_HEADS, n_t, GROUP)).reshape(-1, 1)
    o_s4, k_s, v_s = _attn_step(q_bd, new_kv, cache_k.reshape(n_dec, WINDOW, HKV),
                                cache_v.reshape(n_dec, WINDOW, HKV), sink_rows, past_len, n_t)
    o_s = o_s4.reshape(n_dec, N_KV_HEADS, n_t, GROUP, HEAD_DIM).transpose(2, 0, 1, 3, 4).reshape(n_s, D_MODEL)
    h3_s, u_s, comb_s, sel_s = _oproj(o_s, h2_s, w_o, b_o, g_ffn1, w_router, PROJ_TILE)

    n_tok = n_seq * seq + n_s
    min_tiles = 2 * n_tok // MOE_TILE
    n_tiles = min_tiles + N_EXPERTS
    pos_a, pos_b, gate_a, gate_b, tile_expert, tile_quarters, nvalid, last_tile, has_rows = _route_tables(
        jnp.concatenate([sel_p, sel_s]), jnp.concatenate([comb_p, comb_s]), n_tiles)
    xs = _dispatch(pos_a, pos_b, last_tile, has_rows, nvalid, u_p, u_s, n_tiles, min_tiles)
    y = _moe(tile_expert, tile_quarters, nvalid, xs, w_e1.astype(F32), w_e3.astype(F32), w_e2.astype(F32))
    n_p = n_seq * seq
    y_p = _combine(pos_a, pos_b, h3_p, gate_a[:n_p], gate_b[:n_p], g_final, y, 0)
    y_s = _combine(pos_a, pos_b, h3_s, gate_a[n_p:], gate_b[n_p:], g_final, y, n_p)

    return (y_p.reshape(n_seq, seq, D_MODEL),
            y_s.reshape(n_t, n_dec, D_MODEL).transpose(1, 0, 2),
            pool_p,
            pool_s.transpose(1, 0, 2),
            k_p, v_p,
            k_s.reshape(n_dec, WINDOW, N_KV_HEADS, HEAD_DIM),
            v_s.reshape(n_dec, WINDOW, N_KV_HEADS, HEAD_DIM))
```

```python
import functools

import jax
import jax.numpy as jnp
from jax import lax
from jax.experimental import pallas as pl
from jax.experimental.pallas import tpu as pltpu

F32 = jnp.float32
BF16 = jnp.bfloat16

D_MODEL = 2048
N_META = 16
POOL_WINDOWS = (2, 4, 8, 16)
POOL_GROUP_DIM = D_MODEL // len(POOL_WINDOWS)
POOL_STATE = max(POOL_WINDOWS) - 1
HEAD_DIM = 64
N_HEADS = D_MODEL // HEAD_DIM
N_KV_HEADS = 4
GROUP = N_HEADS // N_KV_HEADS
HKV = N_KV_HEADS * HEAD_DIM
WINDOW = 128
N_EXPERTS = 8
EPS = 1e-5
PAST_LEN = 8192
MASKED = -1e30

LANES = 128
V7X_VMEM_BYTES = 64 * 2 ** 20
VMEM_LIMIT = V7X_VMEM_BYTES - 8 * 2 ** 20

POOL_TILE = 128
ATTN_TILE = WINDOW
MOE_TILE = 1024
MOE_QUARTERS = 4
MOE_FF_TILE = 512
FFN_FF_TILE = 512
FFN_TILE = 1024
PROJ_TILE = 512
ROW_DMA_TILE = 256


def _dot(a, b):
    return jnp.dot(a, b, preferred_element_type=F32)


def _dot_nt(a, b):
    return lax.dot_general(a, b, (((1,), (1,)), ((), ())), preferred_element_type=F32)


def _rmsnorm(x, g):
    return x * lax.rsqrt(jnp.mean(x * x, axis=-1, keepdims=True) + EPS) * g


def _params(n_axes):
    return pltpu.CompilerParams(dimension_semantics=("arbitrary",) * n_axes,
                                vmem_limit_bytes=VMEM_LIMIT)


def _pool_seq_kernel(x_ref, meta_ref, g_ref, w_ref, ls_ref, h_ref, st_ref, hi_ref, lo_ref, u_ref, *, pos0):
    t = pl.program_id(1)
    tt = POOL_TILE
    g = g_ref[...]

    def split(rows):
        hi = rows.astype(BF16)
        return hi, (rows - hi.astype(F32)).astype(BF16)

    @pl.when(t == 0)
    def _():
        zeros = jnp.zeros((tt - N_META, D_MODEL), BF16)
        hi_ref[0:tt - N_META, :] = zeros
        lo_ref[0:tt - N_META, :] = zeros
        hi_ref[tt - N_META:tt, :], lo_ref[tt - N_META:tt, :] = split(_rmsnorm(meta_ref[...], g))

    @pl.when(t > 0)
    def _():
        hi_ref[0:tt, :] = hi_ref[tt:2 * tt, :]
        lo_ref[0:tt, :] = lo_ref[tt:2 * tt, :]

    x = x_ref[0]
    u = _rmsnorm(x, g)
    u_ref[...] = u
    hi_ref[tt:2 * tt, :], lo_ref[tt:2 * tt, :] = split(u)
    hi = hi_ref[...]
    lo = lo_ref[...]
    r = lax.broadcasted_iota(jnp.int32, (tt, 2 * tt), 0)
    c = lax.broadcasted_iota(jnp.int32, (tt, 2 * tt), 1)
    pos = pos0 + t * tt + lax.broadcasted_iota(jnp.int32, (tt, 1), 0)
    for gi, w in enumerate(POOL_WINDOWS):
        sl = slice(gi * POOL_GROUP_DIM, (gi + 1) * POOL_GROUP_DIM)
        band = jnp.where((c > r + tt - w) & (c <= r + tt), 1.0, 0.0).astype(BF16)
        win_sum = _dot(band, hi[:, sl]) + _dot(band, lo[:, sl])
        cnt = jnp.minimum(w, pos + 1).astype(F32)
        mix = win_sum / cnt - u[:, sl]
        o = _dot(mix.astype(BF16), w_ref[gi]) * ls_ref[:, sl]
        h_ref[0, :, sl] = x[:, sl] + o

    @pl.when(t == pl.num_programs(1) - 1)
    def _():
        st_ref[0] = u_ref[tt - POOL_STATE:tt, :]


def _pool_seq(x, meta, g, w, ls, pos0):
    b, t, _ = x.shape
    tt = POOL_TILE
    return pl.pallas_call(
        functools.partial(_pool_seq_kernel, pos0=pos0),
        grid=(b, t // tt),
        in_specs=[
            pl.BlockSpec((1, tt, D_MODEL), lambda i, j: (i, j, 0)),
            pl.BlockSpec((N_META, D_MODEL), lambda i, j: (0, 0)),
            pl.BlockSpec((1, D_MODEL), lambda i, j: (0, 0)),
            pl.BlockSpec((len(POOL_WINDOWS), POOL_GROUP_DIM, POOL_GROUP_DIM), lambda i, j: (0, 0, 0)),
            pl.BlockSpec((1, D_MODEL), lambda i, j: (0, 0)),
        ],
        out_specs=[
            pl.BlockSpec((1, tt, D_MODEL), lambda i, j: (i, j, 0)),
            pl.BlockSpec((1, POOL_STATE, D_MODEL), lambda i, j: (i, 0, 0)),
        ],
        out_shape=[
            jax.ShapeDtypeStruct((b, t, D_MODEL), F32),
            jax.ShapeDtypeStruct((b, POOL_STATE, D_MODEL), F32),
        ],
        scratch_shapes=[pltpu.VMEM((2 * tt, D_MODEL), BF16), pltpu.VMEM((2 * tt, D_MODEL), BF16),
                        pltpu.VMEM((tt, D_MODEL), F32)],
        compiler_params=_params(2),
        name="pool_seq",
    )(x, meta, g, w, ls)


def _pool_step_kernel(x_ref, st_ref, g_ref, w_ref, ls_ref, h_ref, nst_ref, e_ref, *, pos0):
    bb, n_t, _ = x_ref.shape
    g = g_ref[...]
    for j in range(POOL_STATE):
        e_ref[j] = st_ref[:, j, :]
    for t in range(n_t):
        e_ref[POOL_STATE + t] = _rmsnorm(x_ref[:, t, :], g)
    for j in range(POOL_STATE):
        nst_ref[:, j, :] = e_ref[j + n_t]
    for gi, w in enumerate(POOL_WINDOWS):
        sl = slice(gi * POOL_GROUP_DIM, (gi + 1) * POOL_GROUP_DIM)
        mixes = []
        for t in range(n_t):
            win_sum = e_ref[POOL_STATE + t, :, sl]
            for i in range(1, w):
                win_sum = win_sum + e_ref[POOL_STATE + t - i, :, sl]
            cnt = float(min(w, pos0 + t + 1))
            mixes.append(win_sum / cnt - e_ref[POOL_STATE + t, :, sl])
        mix = jnp.concatenate(mixes, axis=0)
        o = _dot(mix.astype(BF16), w_ref[gi]) * ls_ref[:, sl]
        for t in range(n_t):
            h_ref[t, :, sl] = x_ref[:, t, sl] + o[t * bb:(t + 1) * bb]


def _pool_step(x, st, g, w, ls, pos0, bb=32):
    b, n_t, _ = x.shape
    return pl.pallas_call(
        functools.partial(_pool_step_kernel, pos0=pos0),
        grid=(b // bb,),
        in_specs=[
            pl.BlockSpec((bb, n_t, D_MODEL), lambda i: (i, 0, 0)),
            pl.BlockSpec((bb, POOL_STATE, D_MODEL), lambda i: (i, 0, 0)),
            pl.BlockSpec((1, D_MODEL), lambda i: (0, 0)),
            pl.BlockSpec((len(POOL_WINDOWS), POOL_GROUP_DIM, POOL_GROUP_DIM), lambda i: (0, 0, 0)),
            pl.BlockSpec((1, D_MODEL), lambda i: (0, 0)),
        ],
        out_specs=[
            pl.BlockSpec((n_t, bb, D_MODEL), lambda i: (0, i, 0)),
            pl.BlockSpec((bb, POOL_STATE, D_MODEL), lambda i: (i, 0, 0)),
        ],
        out_shape=[
            jax.ShapeDtypeStruct((n_t, b, D_MODEL), F32),
            jax.ShapeDtypeStruct((b, POOL_STATE, D_MODEL), F32),
        ],
        scratch_shapes=[pltpu.VMEM((POOL_STATE + n_t, bb, D_MODEL), F32)],
        compiler_params=_params(1),
        name="pool_step",
    )(x, st, g, w, ls)


def _swiglu_skewed(k, n_chunks, parts, w1, w3, w2, first, last=lambda: None):
    def up():
        w_gate, w_lin = w1(), w3()
        for u_ref, h_ref, _ in parts:
            u = u_ref[...]
            h_ref[...] = (jax.nn.silu(_dot(u, w_gate)) * _dot(u, w_lin)).astype(h_ref.dtype)

    def down():
        w_out = w2()
        for _, h_ref, acc_ref in parts:
            acc_ref[...] += _dot(h_ref[...], w_out)

    @pl.when(k == 0)
    def _():
        first()
        up()

    @pl.when((k > 0) & (k < n_chunks))
    def _():
        down()
        up()

    @pl.when(k == n_chunks)
    def _():
        last()
        down()


def _ffn_kernel(x_ref, g_ref, w1_ref, w3_ref, w2_ref, o_ref, u_ref, h_ref):
    def first():
        x = x_ref[...]
        u_ref[...] = _rmsnorm(x, g_ref[...]).astype(BF16)
        o_ref[...] = x

    _swiglu_skewed(pl.program_id(1), pl.num_programs(1) - 1, [(u_ref, h_ref, o_ref)],
                   lambda: w1_ref[...], lambda: w3_ref[...], lambda: w2_ref[...], first)


def _ffn(x, g, w1, w3, w2, tm):
    m = x.shape[0]
    f = w1.shape[1]
    tf = FFN_FF_TILE
    kf = f // tf
    return pl.pallas_call(
        _ffn_kernel,
        grid=(m // tm, kf + 1),
        in_specs=[
            pl.BlockSpec((tm, D_MODEL), lambda i, k: (i, 0)),
            pl.BlockSpec((1, D_MODEL), lambda i, k: (0, 0)),
            pl.BlockSpec((D_MODEL, tf), lambda i, k: (0, jnp.minimum(k, kf - 1))),
            pl.BlockSpec((D_MODEL, tf), lambda i, k: (0, jnp.minimum(k, kf - 1))),
            pl.BlockSpec((tf, D_MODEL), lambda i, k: (jnp.maximum(k - 1, 0), 0)),
        ],
        out_specs=pl.BlockSpec((tm, D_MODEL), lambda i, k: (i, 0)),
        out_shape=jax.ShapeDtypeStruct((m, D_MODEL), F32),
        scratch_shapes=[pltpu.VMEM((tm, D_MODEL), BF16), pltpu.VMEM((tm, tf), BF16)],
        compiler_params=_params(2),
        name="ffn0",
    )(x, g, w1, w3, w2)


def _qkv_kernel(x_ref, gkv_ref, wkv_ref, bkv_ref, gq_ref, wq_ref, bq_ref, kv_ref, q_ref):
    x = x_ref[...]
    xn = x * lax.rsqrt(jnp.mean(x * x, axis=-1, keepdims=True) + EPS)
    kv_ref[...] = _dot(xn * gkv_ref[...], wkv_ref[...]) + bkv_ref[...]
    q = _dot(xn * gq_ref[...], wq_ref[...]) + bq_ref[...]
    q_ref[...] = (q * HEAD_DIM ** -0.5).astype(BF16)


def _qkv(x, gkv, wkv, bkv, gq, wq, bq, tm):
    m = x.shape[0]
    row = lambda i: (i, 0)
    fixed = lambda i: (0, 0)
    return pl.pallas_call(
        _qkv_kernel,
        grid=(m // tm,),
        in_specs=[
            pl.BlockSpec((tm, D_MODEL), row),
            pl.BlockSpec((1, D_MODEL), fixed),
            pl.BlockSpec((D_MODEL, 2 * HKV), fixed, pipeline_mode=pl.Buffered(1)),
            pl.BlockSpec((1, 2 * HKV), fixed),
            pl.BlockSpec((1, D_MODEL), fixed),
            pl.BlockSpec((D_MODEL, D_MODEL), fixed, pipeline_mode=pl.Buffered(1)),
            pl.BlockSpec((1, D_MODEL), fixed),
        ],
        out_specs=[pl.BlockSpec((tm, 2 * HKV), row), pl.BlockSpec((tm, D_MODEL), row)],
        out_shape=[jax.ShapeDtypeStruct((m, 2 * HKV), F32), jax.ShapeDtypeStruct((m, D_MODEL), BF16)],
        compiler_params=_params(1),
        name="qkv",
    )(x, gkv, wkv, bkv, gq, wq, bq)


def _attn_seq_kernel(sinks_ref, q_ref, cur_ref, prev_ref, past_ref, o_ref, *, pos0):
    j = pl.program_id(1)
    tq = ATTN_TILE
    prev = jnp.where(j == 0, past_ref[...], prev_ref[...])
    band = jnp.concatenate([prev, cur_ref[...]], axis=0)
    c = lax.broadcasted_iota(jnp.int32, (2 * tq, tq), 0)
    r = lax.broadcasted_iota(jnp.int32, (2 * tq, tq), 1)
    key_pos = pos0 + (j - 1) * tq + c
    valid = (c > r) & (c <= r + tq) & (key_pos >= 0)
    low = lax.broadcasted_iota(jnp.int32, (2 * tq, LANES), 1) < HEAD_DIM
    zeros = jnp.zeros((2 * tq, LANES), F32)
    k_bd, v_bd_t = [], []
    v_t = band[:, HKV:].T
    zv = jnp.zeros((HEAD_DIM, 2 * tq), F32)
    for kvh in range(N_KV_HEADS):
        tile = band[:, (kvh // 2) * LANES:(kvh // 2 + 1) * LANES]
        swapped = pltpu.roll(tile, HEAD_DIM, axis=1)
        in_low, in_high = (tile, swapped) if kvh % 2 == 0 else (swapped, tile)
        k_bd.append(jnp.concatenate([jnp.where(low, in_low, zeros), jnp.where(low, zeros, in_high)],
                                    axis=0).astype(BF16))
        vt = v_t[kvh * HEAD_DIM:(kvh + 1) * HEAD_DIM]
        v_bd_t.append(jnp.concatenate([jnp.concatenate([vt, zv], axis=1),
                                       jnp.concatenate([zv, vt], axis=1)], axis=0).astype(BF16))
    for pair in range(N_HEADS // 2):
        kvh = (2 * pair) // GROUP
        ps = slice(pair * LANES, (pair + 1) * LANES)
        s_t = _dot_nt(k_bd[kvh], q_ref[:, ps])
        probs, inv = [], []
        for half in range(2):
            s = jnp.where(valid, s_t[half * 2 * tq:(half + 1) * 2 * tq], MASKED)
            sink = sinks_ref[2 * pair + half]
            m = jnp.maximum(jnp.max(s, axis=0, keepdims=True), sink)
            p = jnp.exp(s - m)
            denom = jnp.sum(p, axis=0, keepdims=True) + jnp.exp(sink - m)
            probs.append(p.astype(BF16))
            inv.append(jnp.broadcast_to(1.0 / denom, (HEAD_DIM, tq)))
        o_t = _dot(v_bd_t[kvh], jnp.concatenate(probs, axis=0)) * jnp.concatenate(inv, axis=0)
        o_ref[:, ps] = o_t.T.astype(BF16)


def _attn_seq(q, kv, past_kv, sinks, n_seq, pos0):
    m = q.shape[0]
    tq = ATTN_TILE
    nb = m // n_seq // tq
    return pl.pallas_call(
        functools.partial(_attn_seq_kernel, pos0=pos0),
        grid=(n_seq, nb),
        in_specs=[
            pl.BlockSpec(memory_space=pltpu.SMEM),
            pl.BlockSpec((tq, D_MODEL), lambda b, j: (b * nb + j, 0)),
            pl.BlockSpec((tq, 2 * HKV), lambda b, j: (b * nb + j, 0)),
            pl.BlockSpec((tq, 2 * HKV), lambda b, j: (b * nb + jnp.maximum(j - 1, 0), 0)),
            pl.BlockSpec((tq, 2 * HKV), lambda b, j: (0, 0)),
        ],
        out_specs=pl.BlockSpec((tq, D_MODEL), lambda b, j: (b * nb + j, 0)),
        out_shape=jax.ShapeDtypeStruct((m, D_MODEL), BF16),
        compiler_params=_params(2),
        name="attn_seq",
    )(sinks, q, kv, kv, past_kv)


def _attn_step_kernel(q_ref, new_ref, ck_ref, cv_ref, sink_ref, o_ref, nk_ref, nv_ref, *, pos0, n_t):
    bb = q_ref.shape[0]
    rows = N_KV_HEADS * n_t * GROUP
    pad = WINDOW - new_ref.shape[1]
    rho = lax.broadcasted_iota(jnp.int32, (rows, 2 * WINDOW), 0)
    c = lax.broadcasted_iota(jnp.int32, (rows, 2 * WINDOW), 1)
    t = (rho // GROUP) % n_t
    valid = (c > t) & (c <= t + WINDOW) & (pos0 - WINDOW + c >= 0)
    sink = sink_ref[...]
    for i in range(bb):
        new = jnp.concatenate([new_ref[i], jnp.zeros((pad, 2 * HKV), F32)], axis=0)
        kcat = jnp.concatenate([ck_ref[i], new[:, :HKV]], axis=0).astype(BF16)
        vcat = jnp.concatenate([cv_ref[i], new[:, HKV:]], axis=0).astype(BF16)
        s = jnp.where(valid, _dot_nt(q_ref[i], kcat), MASKED)
        m = jnp.maximum(jnp.max(s, axis=-1, keepdims=True), sink)
        p = jnp.exp(s - m)
        denom = jnp.sum(p, axis=-1, keepdims=True) + jnp.exp(sink - m)
        o = (_dot(p.astype(BF16), vcat) / denom).astype(BF16)
        for kvh in range(N_KV_HEADS):
            rs = n_t * GROUP
            o_ref[i, kvh] = o[kvh * rs:(kvh + 1) * rs, kvh * HEAD_DIM:(kvh + 1) * HEAD_DIM]
        nk_ref[i, 0:WINDOW - n_t, :] = ck_ref[i, n_t:WINDOW, :]
        nk_ref[i, WINDOW - n_t:WINDOW, :] = new_ref[i, 0:n_t, 0:HKV]
        nv_ref[i, 0:WINDOW - n_t, :] = cv_ref[i, n_t:WINDOW, :]
        nv_ref[i, WINDOW - n_t:WINDOW, :] = new_ref[i, 0:n_t, HKV:2 * HKV]


def _attn_step(q_bd, new_kv, cache_k, cache_v, sink_rows, pos0, n_t, bb=8):
    b = q_bd.shape[0]
    rows = q_bd.shape[1]
    n_new = new_kv.shape[1]
    blk = lambda *s: pl.BlockSpec((bb,) + s, lambda i: (i,) + (0,) * len(s))
    return pl.pallas_call(
        functools.partial(_attn_step_kernel, pos0=pos0, n_t=n_t),
        grid=(b // bb,),
        in_specs=[
            blk(rows, HKV),
            blk(n_new, 2 * HKV),
            blk(WINDOW, HKV),
            blk(WINDOW, HKV),
            pl.BlockSpec((rows, 1), lambda i: (0, 0)),
        ],
        out_specs=[blk(N_KV_HEADS, n_t * GROUP, HEAD_DIM), blk(WINDOW, HKV), blk(WINDOW, HKV)],
        out_shape=[
            jax.ShapeDtypeStruct((b, N_KV_HEADS, n_t * GROUP, HEAD_DIM), BF16),
            jax.ShapeDtypeStruct((b, WINDOW, HKV), F32),
            jax.ShapeDtypeStruct((b, WINDOW, HKV), F32),
        ],
        compiler_params=_params(1),
        name="attn_step",
    )(q_bd, new_kv, cache_k, cache_v, sink_rows)


def _oproj_kernel(o_ref, h_ref, wo_ref, bo_ref, g_ref, wr_ref, h3_ref, u_ref, comb_ref, sel_ref):
    h3 = h_ref[...] + (_dot(o_ref[...].astype(F32), wo_ref[...]) + bo_ref[...])
    h3_ref[...] = h3
    u = _rmsnorm(h3, g_ref[...])
    u_ref[...] = u
    logits = _dot(u.astype(BF16), wr_ref[...])
    lane = lax.broadcasted_iota(jnp.int32, logits.shape, 1)
    lg = jnp.where(lane < N_EXPERTS, logits, -jnp.inf)
    m1 = jnp.max(lg, axis=-1, keepdims=True)
    i1 = jnp.min(jnp.where(lg == m1, lane, LANES), axis=-1, keepdims=True)
    top1 = lane == i1
    lg2 = jnp.where(top1, -jnp.inf, lg)
    m2 = jnp.max(lg2, axis=-1, keepdims=True)
    i2 = jnp.min(jnp.where(lg2 == m2, lane, LANES), axis=-1, keepdims=True)
    top2 = lane == i2
    e2 = jnp.exp(m2 - m1)
    denom = 1.0 + e2
    comb = jnp.where(top1, 1.0 / denom, 0.0) + jnp.where(top2, e2 / denom, 0.0)
    comb_ref[...] = comb[:, :N_EXPERTS]
    sel_ref[...] = jnp.where(top1 | top2, 1, 0).astype(jnp.int32)[:, :N_EXPERTS]


def _oproj(o, h, wo, bo, g, wr, tm):
    m = o.shape[0]
    row = lambda i: (i, 0)
    fixed = lambda i: (0, 0)
    return pl.pallas_call(
        _oproj_kernel,
        grid=(m // tm,),
        in_specs=[
            pl.BlockSpec((tm, D_MODEL), row),
            pl.BlockSpec((tm, D_MODEL), row),
            pl.BlockSpec((D_MODEL, D_MODEL), fixed, pipeline_mode=pl.Buffered(1)),
            pl.BlockSpec((1, D_MODEL), fixed),
            pl.BlockSpec((1, D_MODEL), fixed),
            pl.BlockSpec((D_MODEL, LANES), fixed),
        ],
        out_specs=[
            pl.BlockSpec((tm, D_MODEL), row),
            pl.BlockSpec((tm, D_MODEL), row),
            pl.BlockSpec((tm, N_EXPERTS), row),
            pl.BlockSpec((tm, N_EXPERTS), row),
        ],
        out_shape=[
            jax.ShapeDtypeStruct((m, D_MODEL), F32),
            jax.ShapeDtypeStruct((m, D_MODEL), F32),
            jax.ShapeDtypeStruct((m, N_EXPERTS), F32),
            jax.ShapeDtypeStruct((m, N_EXPERTS), jnp.int32),
        ],
        compiler_params=_params(1),
        name="oproj_router",
    )(o, h, wo, bo, g, wr)


def _row_copy(src, r, dst, p, sem):
    return pltpu.make_async_copy(src.at[pl.ds(r, 1)], dst.at[pl.ds(p, 1)], sem)


def _zero_fill_copies(zeros_ref, xs_ref, last_tile_ref, has_rows_ref, nvalid_ref, sem, n_tiles, min_tiles):
    tm = MOE_TILE
    pairs = []
    for e in range(N_EXPERTS):
        start = pl.multiple_of(last_tile_ref[e], tm)
        cp = pltpu.make_async_copy(zeros_ref, xs_ref.at[pl.ds(start, tm)], sem)
        pairs.append((has_rows_ref[e] != 0, cp))
    for tile in range(min_tiles, n_tiles):
        cp = pltpu.make_async_copy(zeros_ref, xs_ref.at[pl.ds(tile * tm, tm)], sem)
        pairs.append((tile >= nvalid_ref[0], cp))
    return pairs


def _dispatch_kernel(pos_a_ref, pos_b_ref, last_tile_ref, has_rows_ref, nvalid_ref,
                     up_ref, us_ref, xs_ref, zeros_ref, sem, zsem, *, n_blocks_p, n_tiles, min_tiles):
    i = pl.program_id(0)
    rt = ROW_DMA_TILE

    @pl.when(i == 0)
    def _():
        zeros_ref[...] = jnp.zeros(zeros_ref.shape, F32)
        pairs = _zero_fill_copies(zeros_ref, xs_ref, last_tile_ref, has_rows_ref, nvalid_ref,
                                  zsem, n_tiles, min_tiles)
        for cond, cp in pairs:
            @pl.when(cond)
            def _():
                cp.start()
        for cond, cp in pairs:
            @pl.when(cond)
            def _():
                cp.wait()

    def scatter(src_ref):
        for r in range(rt):
            tok = i * rt + r
            _row_copy(src_ref, r, xs_ref, pos_a_ref[tok], sem).start(priority=0)
            _row_copy(src_ref, r, xs_ref, pos_b_ref[tok], sem).start(priority=1)
        for _ in range(2):
            pltpu.make_async_copy(src_ref, xs_ref.at[pl.ds(0, rt)], sem).wait()

    @pl.when(i < n_blocks_p)
    def _():
        scatter(up_ref)

    @pl.when(i >= n_blocks_p)
    def _():
        scatter(us_ref)


def _dispatch(pos_a, pos_b, last_tile, has_rows, nvalid, u_p, u_s, n_tiles, min_tiles):
    rt = ROW_DMA_TILE
    nbp = u_p.shape[0] // rt
    nbs = u_s.shape[0] // rt
    grid_spec = pltpu.PrefetchScalarGridSpec(
        num_scalar_prefetch=5,
        grid=(nbp + nbs,),
        in_specs=[
            pl.BlockSpec((rt, D_MODEL), lambda i, *_: (jnp.minimum(i, nbp - 1), 0)),
            pl.BlockSpec((rt, D_MODEL), lambda i, *_: (jnp.maximum(i - nbp, 0), 0)),
        ],
        out_specs=pl.BlockSpec(memory_space=pl.ANY),
        scratch_shapes=[
            pltpu.VMEM((MOE_TILE, D_MODEL), F32),
            pltpu.SemaphoreType.DMA,
            pltpu.SemaphoreType.DMA,
        ],
    )
    return pl.pallas_call(
        functools.partial(_dispatch_kernel, n_blocks_p=nbp, n_tiles=n_tiles, min_tiles=min_tiles),
        grid_spec=grid_spec,
        out_shape=jax.ShapeDtypeStruct((n_tiles * MOE_TILE, D_MODEL), F32),
        compiler_params=_params(1),
        name="moe_dispatch",
    )(pos_a, pos_b, last_tile, has_rows, nvalid, u_p, u_s)


def _moe_kernel(te_ref, quarters_ref, nv_ref, xs_ref, w1_ref, w3_ref, w2_ref, y_ref, x_ref, h_ref, sem):
    i = pl.program_id(0)
    k = pl.program_id(1)
    tm = MOE_TILE
    valid = i < nv_ref[0]
    quarters = quarters_ref[i]
    n_chunks = pl.num_programs(1) - 1
    weights = (lambda: w1_ref[0], lambda: w3_ref[0], lambda: w2_ref[0])

    def fetch(tile):
        return pltpu.make_async_copy(xs_ref.at[pl.ds(pl.multiple_of(tile * tm, tm), tm)], x_ref, sem)

    def zero_tile():
        y_ref[...] = jnp.zeros(y_ref.shape, F32)

    def first():
        zero_tile()

        @pl.when(i == 0)
        def _():
            fetch(0).start()

        fetch(i).wait()

    def last():
        @pl.when(i + 1 < nv_ref[0])
        def _():
            fetch(i + 1).start()

    for n in range(1, MOE_QUARTERS + 1):
        rows = n * (tm // MOE_QUARTERS)
        part = (x_ref.at[pl.ds(0, rows)], h_ref.at[pl.ds(0, rows)], y_ref.at[pl.ds(0, rows)])

        @pl.when(valid & (quarters == n))
        def _():
            _swiglu_skewed(k, n_chunks, [part], *weights, first, last)

    @pl.when(jnp.logical_not(valid) & (k == 0))
    def _():
        zero_tile()


def _moe(tile_expert, tile_quarters, nvalid, xs, w1, w3, w2):
    tm, tf = MOE_TILE, MOE_FF_TILE
    n_tiles = xs.shape[0] // tm
    kf = w1.shape[2] // tf

    def out_row(i, k, te, full, nv):
        return (i, 0)

    def up(i, k, te, full, nv):
        return (te[jnp.minimum(i, nv[0] - 1)], 0, jnp.where(i < nv[0], jnp.minimum(k, kf - 1), kf - 1))

    def down(i, k, te, full, nv):
        return (te[jnp.minimum(i, nv[0] - 1)], jnp.where(i < nv[0], jnp.maximum(k - 1, 0), kf - 1), 0)

    grid_spec = pltpu.PrefetchScalarGridSpec(
        num_scalar_prefetch=3,
        grid=(n_tiles, kf + 1),
        in_specs=[
            pl.BlockSpec(memory_space=pl.ANY),
            pl.BlockSpec((1, D_MODEL, tf), up),
            pl.BlockSpec((1, D_MODEL, tf), up),
            pl.BlockSpec((1, tf, D_MODEL), down),
        ],
        out_specs=pl.BlockSpec((tm, D_MODEL), out_row),
        scratch_shapes=[
            pltpu.VMEM((tm, D_MODEL), F32),
            pltpu.VMEM((tm, tf), F32),
            pltpu.SemaphoreType.DMA,
        ],
    )
    return pl.pallas_call(
        _moe_kernel,
        grid_spec=grid_spec,
        out_shape=jax.ShapeDtypeStruct(xs.shape, F32),
        compiler_params=_params(2),
        name="moe_ffn",
    )(tile_expert, tile_quarters, nvalid, xs, w1, w3, w2)


def _combine_kernel(pos_a_ref, pos_b_ref, h_ref, ga_ref, gb_ref, g_ref, y_ref, o_ref, ya_ref, yb_ref, sem, *, base):
    i = pl.program_id(0)
    rt = ROW_DMA_TILE

    def fetch(step, slot):
        for r in range(rt):
            tok = base + step * rt + r
            _row_copy(y_ref, pos_a_ref[tok], ya_ref.at[slot], r, sem.at[slot]).start(priority=0)
            _row_copy(y_ref, pos_b_ref[tok], yb_ref.at[slot], r, sem.at[slot]).start(priority=1)

    @pl.when(i == 0)
    def _():
        fetch(0, 0)

    @pl.when(i + 1 < pl.num_programs(0))
    def _():
        fetch(i + 1, (i + 1) % 2)

    slot = i % 2
    pltpu.make_async_copy(y_ref.at[pl.ds(0, rt)], ya_ref.at[slot], sem.at[slot]).wait()
    pltpu.make_async_copy(y_ref.at[pl.ds(0, rt)], yb_ref.at[slot], sem.at[slot]).wait()
    moe = ga_ref[...] * ya_ref[slot] + gb_ref[...] * yb_ref[slot]
    o_ref[...] = _rmsnorm(h_ref[...] + moe, g_ref[...])


def _combine(pos_a, pos_b, h, gate_a, gate_b, g, y, base):
    rt = ROW_DMA_TILE
    m = h.shape[0]
    row = lambda i, *_: (i, 0)
    grid_spec = pltpu.PrefetchScalarGridSpec(
        num_scalar_prefetch=2,
        grid=(m // rt,),
        in_specs=[
            pl.BlockSpec((rt, D_MODEL), row),
            pl.BlockSpec((rt, 1), row),
            pl.BlockSpec((rt, 1), row),
            pl.BlockSpec((1, D_MODEL), lambda i, *_: (0, 0)),
            pl.BlockSpec(memory_space=pl.ANY),
        ],
        out_specs=pl.BlockSpec((rt, D_MODEL), row),
        scratch_shapes=[
            pltpu.VMEM((2, rt, D_MODEL), F32),
            pltpu.VMEM((2, rt, D_MODEL), F32),
            pltpu.SemaphoreType.DMA((2,)),
        ],
    )
    return pl.pallas_call(
        functools.partial(_combine_kernel, base=base),
        grid_spec=grid_spec,
        out_shape=jax.ShapeDtypeStruct((m, D_MODEL), F32),
        compiler_params=_params(1),
        name="moe_combine",
    )(pos_a, pos_b, h, gate_a, gate_b, g, y)


def _token_cumsum(sel):
    n_e, n_tok = sel.shape
    blocks = n_tok // LANES
    s = sel.astype(F32).reshape(n_e, blocks, LANES)
    idx = jnp.arange(LANES)
    within = jnp.einsum("ebl,lm->ebm", s, (idx[:, None] <= idx[None, :]).astype(F32),
                        precision=lax.Precision.HIGHEST)
    totals = within[:, :, -1]
    bidx = jnp.arange(blocks)
    offsets = jnp.einsum("eb,bc->ec", totals, (bidx[:, None] < bidx[None, :]).astype(F32),
                         precision=lax.Precision.HIGHEST)
    return (within + offsets[:, :, None]).astype(jnp.int32).reshape(n_e, n_tok)


def _route_tables(sel, comb, n_tiles):
    tm = MOE_TILE
    sel = sel.T
    comb = comb.T
    cum = _token_cumsum(sel)
    counts = cum[:, -1]
    padded = ((counts + tm - 1) // tm) * tm
    ends = jnp.cumsum(padded)
    starts = ends - padded
    slot = starts[:, None] + cum - sel
    order = jnp.cumsum(sel, axis=0)
    first = (sel == 1) & (order == 1)
    second = (sel == 1) & (order == 2)
    pos_a = jnp.sum(jnp.where(first, slot, 0), axis=0).astype(jnp.int32)
    pos_b = jnp.sum(jnp.where(second, slot, 0), axis=0).astype(jnp.int32)
    gate_a = jnp.sum(jnp.where(first, comb, 0.0), axis=0)[:, None]
    gate_b = jnp.sum(jnp.where(second, comb, 0.0), axis=0)[:, None]
    tile_start = jnp.arange(n_tiles, dtype=jnp.int32) * tm
    tile_expert = jnp.minimum(jnp.sum(tile_start[:, None] >= ends[None, :], axis=1), N_EXPERTS - 1)
    rows_in_tile = (starts + counts)[tile_expert] - tile_start
    quarter = tm // MOE_QUARTERS
    tile_quarters = jnp.clip((rows_in_tile + quarter - 1) // quarter, 0, MOE_QUARTERS).astype(jnp.int32)
    nvalid = (ends[-1:] // tm).astype(jnp.int32)
    last_tile = jnp.maximum(ends - tm, 0).astype(jnp.int32)
    has_rows = (counts > 0).astype(jnp.int32)
    return (pos_a, pos_b, gate_a, gate_b, tile_expert.astype(jnp.int32), tile_quarters, nvalid,
            last_tile, has_rows)


def kernel(x_prompt, x_sample, state_pool, cache_k, cache_v, meta_tokens, g_pool, w_pool, ls_pool, g_ffn0, w_ff1, w_ff3, w_ff2, g_kv, w_kv, b_kv, g_attn, w_q, b_q, sinks, w_o, b_o, g_ffn1, w_router, w_e1, w_e3, w_e2, g_final):
    n_seq, seq, _ = x_prompt.shape
    n_dec, n_t, _ = x_sample.shape
    past_len = PAST_LEN
    window = cache_k.shape[1]
    assert window == WINDOW and seq % POOL_TILE == 0 and N_META <= POOL_TILE

    vec = lambda a: a.reshape(1, -1).astype(F32)
    g_pool, ls_pool, g_ffn0, g_kv, b_kv, g_attn, b_q, b_o, g_ffn1, g_final = map(
        vec, (g_pool, ls_pool, g_ffn0, g_kv, b_kv, g_attn, b_q, b_o, g_ffn1, g_final))
    w_pool, w_ff1, w_ff3, w_ff2 = (w.astype(BF16) for w in (w_pool, w_ff1, w_ff3, w_ff2))
    w_kv, w_q, w_o = (w.astype(F32) for w in (w_kv, w_q, w_o))
    w_router = jnp.pad(w_router, ((0, 0), (0, LANES - N_EXPERTS))).astype(BF16)
    meta = meta_tokens.astype(F32)

    def layer0_tail(h1, tm_ffn, tm_proj):
        h2 = _ffn(h1, g_ffn0, w_ff1, w_ff3, w_ff2, min(tm_ffn, h1.shape[0]))
        kv, q = _qkv(h2, g_kv, w_kv, b_kv, g_attn, w_q, b_q, tm_proj)
        return h2, kv, q

    x_meta = jnp.pad(meta, ((0, POOL_TILE - N_META), (0, 0)))[None]
    h1_m, _ = _pool_seq(x_meta, jnp.zeros_like(meta), g_pool, w_pool, ls_pool, 0)

    h1_s, pool_s = _pool_step(x_sample, state_pool, g_pool, w_pool, ls_pool, past_len)
    n_s = n_t * n_dec
    h1_ms = jnp.concatenate([h1_m[0], h1_s.reshape(n_s, D_MODEL)])
    h2_ms, kv_ms, q_ms = layer0_tail(h1_ms, h1_ms.shape[0], POOL_TILE)
    h2_s, kv_s, q_s = h2_ms[POOL_TILE:], kv_ms[POOL_TILE:], q_ms[POOL_TILE:]
    past_kv = jnp.pad(kv_ms[:N_META], ((WINDOW - N_META, 0), (0, 0)))

    h1_p, pool_p = _pool_seq(x_prompt, meta, g_pool, w_pool, ls_pool, N_META)
    h2_p, kv_p, q_p = layer0_tail(h1_p.reshape(n_seq * seq, D_MODEL), FFN_TILE, PROJ_TILE)
    o_p = _attn_seq(q_p, kv_p, past_kv, sinks.astype(F32), n_seq, N_META)
    h3_p, u_p, comb_p, sel_p = _oproj(o_p, h2_p, w_o, b_o, g_ffn1, w_router, PROJ_TILE)
    kv_tail = kv_p.reshape(n_seq, seq, 2 * HKV)[:, seq - WINDOW:]
    k_p = kv_tail[..., :HKV].reshape(n_seq, WINDOW, N_KV_HEADS, HEAD_DIM)
    v_p = kv_tail[..., HKV:].reshape(n_seq, WINDOW, N_KV_HEADS, HEAD_DIM)

    q5 = q_s.reshape(n_t, n_dec, N_KV_HEADS, GROUP, HEAD_DIM).transpose(1, 2, 0, 3, 4)
    eye = jnp.eye(N_KV_HEADS, dtype=BF16)
    q_bd = (q5[:, :, :, :, None, :] * eye[None, :, None, None, :, None]).reshape(
        n_dec, N_KV_HEADS * n_t * GROUP, HKV)
    new_kv = jnp.pad(kv_s.reshape(n_t, n_dec, 2 * HKV).transpose(1, 0, 2), ((0, 0), (0, 16 - n_t), (0, 0)))
    sink_rows = jnp.broadcast_to(sinks.astype(F32).reshape(N_KV_HEADS, 1, GROUP),
                                 (N_KV_HEADS, n_t, GROUP)).reshape(-1, 1)
    o_s4, k_s, v_s = _attn_step(q_bd, new_kv, cache_k.reshape(n_dec, WINDOW, HKV),
                                cache_v.reshape(n_dec, WINDOW, HKV), sink_rows, past_len, n_t)
    o_s = o_s4.reshape(n_dec, N_KV_HEADS, n_t, GROUP, HEAD_DIM).transpose(2, 0, 1, 3, 4).reshape(n_s, D_MODEL)
    h3_s, u_s, comb_s, sel_s = _oproj(o_s, h2_s, w_o, b_o, g_ffn1, w_router, PROJ_TILE)

    n_tok = n_seq * seq + n_s
    min_tiles = 2 * n_tok // MOE_TILE
    n_tiles = min_tiles + N_EXPERTS
    pos_a, pos_b, gate_a, gate_b, tile_expert, tile_quarters, nvalid, last_tile, has_rows = _route_tables(
        jnp.concatenate([sel_p, sel_s]), jnp.concatenate([comb_p, comb_s]), n_tiles)
    xs = _dispatch(pos_a, pos_b, last_tile, has_rows, nvalid, u_p, u_s, n_tiles, min_tiles)
    y = _moe(tile_expert, tile_quarters, nvalid, xs, w_e1.astype(F32), w_e3.astype(F32), w_e2.astype(F32))
    n_p = n_seq * seq
    y_p = _combine(pos_a, pos_b, h3_p, gate_a[:n_p], gate_b[:n_p], g_final, y, 0)
    y_s = _combine(pos_a, pos_b, h3_s, gate_a[n_p:], gate_b[n_p:], g_final, y, n_p)

    return (y_p.reshape(n_seq, seq, D_MODEL),
            y_s.reshape(n_t, n_dec, D_MODEL).transpose(1, 0, 2),
            pool_p,
            pool_s,
            k_p, v_p,
            k_s.reshape(n_dec, WINDOW, N_KV_HEADS, HEAD_DIM),
            v_s.reshape(n_dec, WINDOW, N_KV_HEADS, HEAD_DIM))
```

```python
import functools

import jax
import jax.numpy as jnp
from jax import lax
from jax.experimental import pallas as pl
from jax.experimental.pallas import tpu as pltpu

F32 = jnp.float32
BF16 = jnp.bfloat16

D_MODEL = 2048
N_META = 16
POOL_WINDOWS = (2, 4, 8, 16)
POOL_GROUP_DIM = D_MODEL // len(POOL_WINDOWS)
POOL_STATE = max(POOL_WINDOWS) - 1
HEAD_DIM = 64
N_HEADS = D_MODEL // HEAD_DIM
N_KV_HEADS = 4
GROUP = N_HEADS // N_KV_HEADS
HKV = N_KV_HEADS * HEAD_DIM
WINDOW = 128
N_EXPERTS = 8
EPS = 1e-5
PAST_LEN = 8192
MASKED = -1e30

LANES = 128
V7X_VMEM_BYTES = 64 * 2 ** 20
VMEM_LIMIT = V7X_VMEM_BYTES - 8 * 2 ** 20

POOL_TILE = 128
ATTN_TILE = WINDOW
MOE_TILE = 1024
MOE_QUARTERS = 4
MOE_FF_TILE = 512
FFN_FF_TILE = 512
FFN_TILE = 1024
PROJ_TILE = 256
QKV_TILE = 512
ROW_DMA_TILE = 256
BF16_SUBLANES = 16


def _dot(a, b):
    return jnp.dot(a, b, preferred_element_type=F32)


def _dot_nt(a, b):
    return lax.dot_general(a, b, (((1,), (1,)), ((), ())), preferred_element_type=F32)


def _rmsnorm(x, g):
    return x * lax.rsqrt(jnp.mean(x * x, axis=-1, keepdims=True) + EPS) * g


def _params(n_axes):
    return pltpu.CompilerParams(dimension_semantics=("arbitrary",) * n_axes,
                                vmem_limit_bytes=VMEM_LIMIT)


def _pool_seq_kernel(x_ref, meta_ref, g_ref, w_ref, ls_ref, h_ref, st_ref, hi_ref, lo_ref, u_ref, *, pos0):
    t = pl.program_id(1)
    tt = POOL_TILE
    g = g_ref[...]

    def split(rows):
        hi = rows.astype(BF16)
        return hi, (rows - hi.astype(F32)).astype(BF16)

    @pl.when(t == 0)
    def _():
        zeros = jnp.zeros((tt - N_META, D_MODEL), BF16)
        hi_ref[0:tt - N_META, :] = zeros
        lo_ref[0:tt - N_META, :] = zeros
        hi_ref[tt - N_META:tt, :], lo_ref[tt - N_META:tt, :] = split(_rmsnorm(meta_ref[...], g))

    @pl.when(t > 0)
    def _():
        hi_ref[0:tt, :] = hi_ref[tt:2 * tt, :]
        lo_ref[0:tt, :] = lo_ref[tt:2 * tt, :]

    x = x_ref[0]
    u = _rmsnorm(x, g)
    u_ref[...] = u
    hi_ref[tt:2 * tt, :], lo_ref[tt:2 * tt, :] = split(u)
    hi = hi_ref[...]
    lo = lo_ref[...]
    r = lax.broadcasted_iota(jnp.int32, (tt, 2 * tt), 0)
    c = lax.broadcasted_iota(jnp.int32, (tt, 2 * tt), 1)
    pos = pos0 + t * tt + lax.broadcasted_iota(jnp.int32, (tt, 1), 0)
    for gi, w in enumerate(POOL_WINDOWS):
        sl = slice(gi * POOL_GROUP_DIM, (gi + 1) * POOL_GROUP_DIM)
        band = jnp.where((c > r + tt - w) & (c <= r + tt), 1.0, 0.0).astype(BF16)
        win_sum = _dot(band, hi[:, sl]) + _dot(band, lo[:, sl])
        cnt = jnp.minimum(w, pos + 1).astype(F32)
        mix = win_sum / cnt - u[:, sl]
        o = _dot(mix.astype(BF16), w_ref[gi]) * ls_ref[:, sl]
        h_ref[0, :, sl] = x[:, sl] + o

    @pl.when(t == pl.num_programs(1) - 1)
    def _():
        st_ref[0] = u_ref[tt - POOL_STATE:tt, :]


def _pool_seq(x, meta, g, w, ls, pos0):
    b, t, _ = x.shape
    tt = POOL_TILE
    return pl.pallas_call(
        functools.partial(_pool_seq_kernel, pos0=pos0),
        grid=(b, t // tt),
        in_specs=[
            pl.BlockSpec((1, tt, D_MODEL), lambda i, j: (i, j, 0)),
            pl.BlockSpec((N_META, D_MODEL), lambda i, j: (0, 0)),
            pl.BlockSpec((1, D_MODEL), lambda i, j: (0, 0)),
            pl.BlockSpec((len(POOL_WINDOWS), POOL_GROUP_DIM, POOL_GROUP_DIM), lambda i, j: (0, 0, 0)),
            pl.BlockSpec((1, D_MODEL), lambda i, j: (0, 0)),
        ],
        out_specs=[
            pl.BlockSpec((1, tt, D_MODEL), lambda i, j: (i, j, 0)),
            pl.BlockSpec((1, POOL_STATE, D_MODEL), lambda i, j: (i, 0, 0)),
        ],
        out_shape=[
            jax.ShapeDtypeStruct((b, t, D_MODEL), F32),
            jax.ShapeDtypeStruct((b, POOL_STATE, D_MODEL), F32),
        ],
        scratch_shapes=[pltpu.VMEM((2 * tt, D_MODEL), BF16), pltpu.VMEM((2 * tt, D_MODEL), BF16),
                        pltpu.VMEM((tt, D_MODEL), F32)],
        compiler_params=_params(2),
        name="pool_seq",
    )(x, meta, g, w, ls)


def _pool_step_kernel(x_ref, st_ref, g_ref, w_ref, ls_ref, h_ref, nst_ref, e_ref, *, pos0):
    n_t, bb, _ = x_ref.shape
    g = g_ref[...]
    for j in range(POOL_STATE):
        e_ref[j] = st_ref[j]
    for t in range(n_t):
        e_ref[POOL_STATE + t] = _rmsnorm(x_ref[t], g)
    for j in range(POOL_STATE):
        nst_ref[j] = e_ref[j + n_t]
    for gi, w in enumerate(POOL_WINDOWS):
        sl = slice(gi * POOL_GROUP_DIM, (gi + 1) * POOL_GROUP_DIM)
        mixes = []
        for t in range(n_t):
            win_sum = e_ref[POOL_STATE + t, :, sl]
            for i in range(1, w):
                win_sum = win_sum + e_ref[POOL_STATE + t - i, :, sl]
            cnt = float(min(w, pos0 + t + 1))
            mixes.append(win_sum / cnt - e_ref[POOL_STATE + t, :, sl])
        mix = jnp.concatenate(mixes, axis=0)
        o = _dot(mix.astype(BF16), w_ref[gi]) * ls_ref[:, sl]
        for t in range(n_t):
            h_ref[t, :, sl] = x_ref[t, :, sl] + o[t * bb:(t + 1) * bb]


def _pool_step(x_t, st_t, g, w, ls, pos0, bb=32):
    n_t, b, _ = x_t.shape
    return pl.pallas_call(
        functools.partial(_pool_step_kernel, pos0=pos0),
        grid=(b // bb,),
        in_specs=[
            pl.BlockSpec((n_t, bb, D_MODEL), lambda i: (0, i, 0)),
            pl.BlockSpec((POOL_STATE, bb, D_MODEL), lambda i: (0, i, 0)),
            pl.BlockSpec((1, D_MODEL), lambda i: (0, 0)),
            pl.BlockSpec((len(POOL_WINDOWS), POOL_GROUP_DIM, POOL_GROUP_DIM), lambda i: (0, 0, 0)),
            pl.BlockSpec((1, D_MODEL), lambda i: (0, 0)),
        ],
        out_specs=[
            pl.BlockSpec((n_t, bb, D_MODEL), lambda i: (0, i, 0)),
            pl.BlockSpec((POOL_STATE, bb, D_MODEL), lambda i: (0, i, 0)),
        ],
        out_shape=[
            jax.ShapeDtypeStruct((n_t, b, D_MODEL), F32),
            jax.ShapeDtypeStruct((POOL_STATE, b, D_MODEL), F32),
        ],
        scratch_shapes=[pltpu.VMEM((POOL_STATE + n_t, bb, D_MODEL), F32)],
        compiler_params=_params(1),
        name="pool_step",
    )(x_t, st_t, g, w, ls)


def _swiglu_skewed(k, n_chunks, parts, w1, w3, w2, first, last=lambda: None):
    def up():
        w_gate, w_lin = w1(), w3()
        for u_ref, h_ref, _ in parts:
            u = u_ref[...]
            h_ref[...] = (jax.nn.silu(_dot(u, w_gate)) * _dot(u, w_lin)).astype(h_ref.dtype)

    def down():
        w_out = w2()
        for _, h_ref, acc_ref in parts:
            acc_ref[...] += _dot(h_ref[...], w_out)

    @pl.when(k == 0)
    def _():
        first()
        up()

    @pl.when((k > 0) & (k < n_chunks))
    def _():
        down()
        up()

    @pl.when(k == n_chunks)
    def _():
        last()
        down()


def _ffn_kernel(x_ref, g_ref, w1_ref, w3_ref, w2_ref, o_ref, u_ref, h_ref):
    def first():
        x = x_ref[...]
        u_ref[...] = _rmsnorm(x, g_ref[...]).astype(BF16)
        o_ref[...] = x

    _swiglu_skewed(pl.program_id(1), pl.num_programs(1) - 1, [(u_ref, h_ref, o_ref)],
                   lambda: w1_ref[...], lambda: w3_ref[...], lambda: w2_ref[...], first)


def _ffn(x, g, w1, w3, w2, tm):
    m = x.shape[0]
    f = w1.shape[1]
    tf = FFN_FF_TILE
    kf = f // tf
    return pl.pallas_call(
        _ffn_kernel,
        grid=(m // tm, kf + 1),
        in_specs=[
            pl.BlockSpec((tm, D_MODEL), lambda i, k: (i, 0)),
            pl.BlockSpec((1, D_MODEL), lambda i, k: (0, 0)),
            pl.BlockSpec((D_MODEL, tf), lambda i, k: (0, jnp.minimum(k, kf - 1))),
            pl.BlockSpec((D_MODEL, tf), lambda i, k: (0, jnp.minimum(k, kf - 1))),
            pl.BlockSpec((tf, D_MODEL), lambda i, k: (jnp.maximum(k - 1, 0), 0)),
        ],
        out_specs=pl.BlockSpec((tm, D_MODEL), lambda i, k: (i, 0)),
        out_shape=jax.ShapeDtypeStruct((m, D_MODEL), F32),
        scratch_shapes=[pltpu.VMEM((tm, D_MODEL), BF16), pltpu.VMEM((tm, tf), BF16)],
        compiler_params=_params(2),
        name="ffn0",
    )(x, g, w1, w3, w2)


def _qkv_kernel(x_ref, gkv_ref, wkv_ref, bkv_ref, gq_ref, wq_ref, bq_ref, kv_ref, q_ref):
    x = x_ref[...]
    xn = x * lax.rsqrt(jnp.mean(x * x, axis=-1, keepdims=True) + EPS)
    kv_ref[...] = _dot(xn * gkv_ref[...], wkv_ref[...]) + bkv_ref[...]
    q = _dot(xn * gq_ref[...], wq_ref[...]) + bq_ref[...]
    q_ref[...] = (q * HEAD_DIM ** -0.5).astype(BF16)


def _qkv(x, gkv, wkv, bkv, gq, wq, bq, tm):
    m = x.shape[0]
    row = lambda i: (i, 0)
    fixed = lambda i: (0, 0)
    return pl.pallas_call(
        _qkv_kernel,
        grid=(m // tm,),
        in_specs=[
            pl.BlockSpec((tm, D_MODEL), row),
            pl.BlockSpec((1, D_MODEL), fixed),
            pl.BlockSpec((D_MODEL, 2 * HKV), fixed, pipeline_mode=pl.Buffered(1)),
            pl.BlockSpec((1, 2 * HKV), fixed),
            pl.BlockSpec((1, D_MODEL), fixed),
            pl.BlockSpec((D_MODEL, D_MODEL), fixed, pipeline_mode=pl.Buffered(1)),
            pl.BlockSpec((1, D_MODEL), fixed),
        ],
        out_specs=[pl.BlockSpec((tm, 2 * HKV), row), pl.BlockSpec((tm, D_MODEL), row)],
        out_shape=[jax.ShapeDtypeStruct((m, 2 * HKV), F32), jax.ShapeDtypeStruct((m, D_MODEL), BF16)],
        compiler_params=_params(1),
        name="qkv",
    )(x, gkv, wkv, bkv, gq, wq, bq)


def _attn_seq_kernel(sinks_ref, q_ref, cur_ref, prev_ref, past_ref, o_ref, *, pos0):
    j = pl.program_id(1)
    tq = ATTN_TILE
    prev = jnp.where(j == 0, past_ref[...], prev_ref[...])
    band = jnp.concatenate([prev, cur_ref[...]], axis=0)
    c = lax.broadcasted_iota(jnp.int32, (2 * tq, tq), 0)
    r = lax.broadcasted_iota(jnp.int32, (2 * tq, tq), 1)
    key_pos = pos0 + (j - 1) * tq + c
    valid = (c > r) & (c <= r + tq) & (key_pos >= 0)
    low = lax.broadcasted_iota(jnp.int32, (2 * tq, LANES), 1) < HEAD_DIM
    zeros = jnp.zeros((2 * tq, LANES), F32)
    k_bd, v_bd_t = [], []
    v_t = band[:, HKV:].T
    zv = jnp.zeros((HEAD_DIM, 2 * tq), F32)
    for kvh in range(N_KV_HEADS):
        tile = band[:, (kvh // 2) * LANES:(kvh // 2 + 1) * LANES]
        swapped = pltpu.roll(tile, HEAD_DIM, axis=1)
        in_low, in_high = (tile, swapped) if kvh % 2 == 0 else (swapped, tile)
        k_bd.append(jnp.concatenate([jnp.where(low, in_low, zeros), jnp.where(low, zeros, in_high)],
                                    axis=0).astype(BF16))
        vt = v_t[kvh * HEAD_DIM:(kvh + 1) * HEAD_DIM]
        v_bd_t.append(jnp.concatenate([jnp.concatenate([vt, zv], axis=1),
                                       jnp.concatenate([zv, vt], axis=1)], axis=0).astype(BF16))
    for pair in range(N_HEADS // 2):
        kvh = (2 * pair) // GROUP
        ps = slice(pair * LANES, (pair + 1) * LANES)
        s_t = _dot_nt(k_bd[kvh], q_ref[:, ps])
        probs, inv = [], []
        for half in range(2):
            s = jnp.where(valid, s_t[half * 2 * tq:(half + 1) * 2 * tq], MASKED)
            sink = sinks_ref[2 * pair + half]
            m = jnp.maximum(jnp.max(s, axis=0, keepdims=True), sink)
            p = jnp.exp(s - m)
            denom = jnp.sum(p, axis=0, keepdims=True) + jnp.exp(sink - m)
            probs.append(p.astype(BF16))
            inv.append(jnp.broadcast_to(1.0 / denom, (HEAD_DIM, tq)))
        o_t = _dot(v_bd_t[kvh], jnp.concatenate(probs, axis=0)) * jnp.concatenate(inv, axis=0)
        o_ref[:, ps] = o_t.T.astype(BF16)


def _attn_seq(q, kv, past_kv, sinks, n_seq, pos0):
    m = q.shape[0]
    tq = ATTN_TILE
    nb = m // n_seq // tq
    return pl.pallas_call(
        functools.partial(_attn_seq_kernel, pos0=pos0),
        grid=(n_seq, nb),
        in_specs=[
            pl.BlockSpec(memory_space=pltpu.SMEM),
            pl.BlockSpec((tq, D_MODEL), lambda b, j: (b * nb + j, 0)),
            pl.BlockSpec((tq, 2 * HKV), lambda b, j: (b * nb + j, 0)),
            pl.BlockSpec((tq, 2 * HKV), lambda b, j: (b * nb + jnp.maximum(j - 1, 0), 0)),
            pl.BlockSpec((tq, 2 * HKV), lambda b, j: (0, 0)),
        ],
        out_specs=pl.BlockSpec((tq, D_MODEL), lambda b, j: (b * nb + j, 0)),
        out_shape=jax.ShapeDtypeStruct((m, D_MODEL), BF16),
        compiler_params=_params(2),
        name="attn_seq",
    )(sinks, q, kv, kv, past_kv)


def _attn_step_kernel(q_ref, new_ref, ck_ref, cv_ref, sink_ref, o_ref, nk_ref, nv_ref, *, pos0, n_t):
    bb = q_ref.shape[0]
    rows = N_KV_HEADS * n_t * GROUP
    pad = WINDOW - new_ref.shape[1]
    rho = lax.broadcasted_iota(jnp.int32, (rows, 2 * WINDOW), 0)
    c = lax.broadcasted_iota(jnp.int32, (rows, 2 * WINDOW), 1)
    t = (rho // GROUP) % n_t
    valid = (c > t) & (c <= t + WINDOW) & (pos0 - WINDOW + c >= 0)
    sink = sink_ref[...]
    for i in range(bb):
        new = jnp.concatenate([new_ref[i], jnp.zeros((pad, 2 * HKV), F32)], axis=0)
        kcat = jnp.concatenate([ck_ref[i], new[:, :HKV]], axis=0).astype(BF16)
        vcat = jnp.concatenate([cv_ref[i], new[:, HKV:]], axis=0).astype(BF16)
        s = jnp.where(valid, _dot_nt(q_ref[i], kcat), MASKED)
        m = jnp.maximum(jnp.max(s, axis=-1, keepdims=True), sink)
        p = jnp.exp(s - m)
        denom = jnp.sum(p, axis=-1, keepdims=True) + jnp.exp(sink - m)
        o = (_dot(p.astype(BF16), vcat) / denom).astype(BF16)
        for kvh in range(N_KV_HEADS):
            rs = n_t * GROUP
            o_ref[i, kvh] = o[kvh * rs:(kvh + 1) * rs, kvh * HEAD_DIM:(kvh + 1) * HEAD_DIM]
        nk_ref[i, 0:WINDOW - n_t, :] = ck_ref[i, n_t:WINDOW, :]
        nk_ref[i, WINDOW - n_t:WINDOW, :] = new_ref[i, 0:n_t, 0:HKV]
        nv_ref[i, 0:WINDOW - n_t, :] = cv_ref[i, n_t:WINDOW, :]
        nv_ref[i, WINDOW - n_t:WINDOW, :] = new_ref[i, 0:n_t, HKV:2 * HKV]


def _attn_step(q_bd, new_kv, cache_k, cache_v, sink_rows, pos0, n_t, bb=16):
    b = q_bd.shape[0]
    rows = q_bd.shape[1]
    n_new = new_kv.shape[1]
    blk = lambda *s: pl.BlockSpec((bb,) + s, lambda i: (i,) + (0,) * len(s))
    return pl.pallas_call(
        functools.partial(_attn_step_kernel, pos0=pos0, n_t=n_t),
        grid=(b // bb,),
        in_specs=[
            blk(rows, HKV),
            blk(n_new, 2 * HKV),
            blk(WINDOW, HKV),
            blk(WINDOW, HKV),
            pl.BlockSpec((rows, 1), lambda i: (0, 0)),
        ],
        out_specs=[blk(N_KV_HEADS, n_t * GROUP, HEAD_DIM), blk(WINDOW, HKV), blk(WINDOW, HKV)],
        out_shape=[
            jax.ShapeDtypeStruct((b, N_KV_HEADS, n_t * GROUP, HEAD_DIM), BF16),
            jax.ShapeDtypeStruct((b, WINDOW, HKV), F32),
            jax.ShapeDtypeStruct((b, WINDOW, HKV), F32),
        ],
        compiler_params=_params(1),
        name="attn_step",
    )(q_bd, new_kv, cache_k, cache_v, sink_rows)


def _oproj_kernel(o_ref, h_ref, wo_ref, bo_ref, g_ref, wr_ref, h3_ref, u_ref, comb_ref, sel_ref):
    h3 = h_ref[...] + (_dot(o_ref[...].astype(F32), wo_ref[...]) + bo_ref[...])
    h3_ref[...] = h3
    u = _rmsnorm(h3, g_ref[...])
    u_ref[...] = u
    logits = _dot(u.astype(BF16), wr_ref[...])
    lane = lax.broadcasted_iota(jnp.int32, logits.shape, 1)
    lg = jnp.where(lane < N_EXPERTS, logits, -jnp.inf)
    m1 = jnp.max(lg, axis=-1, keepdims=True)
    i1 = jnp.min(jnp.where(lg == m1, lane, LANES), axis=-1, keepdims=True)
    top1 = lane == i1
    lg2 = jnp.where(top1, -jnp.inf, lg)
    m2 = jnp.max(lg2, axis=-1, keepdims=True)
    i2 = jnp.min(jnp.where(lg2 == m2, lane, LANES), axis=-1, keepdims=True)
    top2 = lane == i2
    e2 = jnp.exp(m2 - m1)
    denom = 1.0 + e2
    comb = jnp.where(top1, 1.0 / denom, 0.0) + jnp.where(top2, e2 / denom, 0.0)
    comb_ref[...] = comb[:, :N_EXPERTS]
    sel_ref[...] = jnp.where(top1 | top2, 1, 0).astype(jnp.int32)[:, :N_EXPERTS]


def _oproj(o, h, wo, bo, g, wr, tm):
    m = o.shape[0]
    row = lambda i: (i, 0)
    fixed = lambda i: (0, 0)
    return pl.pallas_call(
        _oproj_kernel,
        grid=(m // tm,),
        in_specs=[
            pl.BlockSpec((tm, D_MODEL), row),
            pl.BlockSpec((tm, D_MODEL), row),
            pl.BlockSpec((D_MODEL, D_MODEL), fixed, pipeline_mode=pl.Buffered(1)),
            pl.BlockSpec((1, D_MODEL), fixed),
            pl.BlockSpec((1, D_MODEL), fixed),
            pl.BlockSpec((D_MODEL, LANES), fixed),
        ],
        out_specs=[
            pl.BlockSpec((tm, D_MODEL), row),
            pl.BlockSpec((tm, D_MODEL), row),
            pl.BlockSpec((tm, N_EXPERTS), row),
            pl.BlockSpec((tm, N_EXPERTS), row),
        ],
        out_shape=[
            jax.ShapeDtypeStruct((m, D_MODEL), F32),
            jax.ShapeDtypeStruct((m, D_MODEL), F32),
            jax.ShapeDtypeStruct((m, N_EXPERTS), F32),
            jax.ShapeDtypeStruct((m, N_EXPERTS), jnp.int32),
        ],
        compiler_params=_params(1),
        name="oproj_router",
    )(o, h, wo, bo, g, wr)


def _row_copy(src, r, dst, p, sem):
    return pltpu.make_async_copy(src.at[pl.ds(r, 1)], dst.at[pl.ds(p, 1)], sem)


def _zero_fill_copies(zeros_ref, xs_ref, last_tile_ref, has_rows_ref, nvalid_ref, sem, n_tiles, min_tiles):
    tm = MOE_TILE
    pairs = []
    for e in range(N_EXPERTS):
        start = pl.multiple_of(last_tile_ref[e], tm)
        cp = pltpu.make_async_copy(zeros_ref, xs_ref.at[pl.ds(start, tm)], sem)
        pairs.append((has_rows_ref[e] != 0, cp))
    for tile in range(min_tiles, n_tiles):
        cp = pltpu.make_async_copy(zeros_ref, xs_ref.at[pl.ds(tile * tm, tm)], sem)
        pairs.append((tile >= nvalid_ref[0], cp))
    return pairs


def _dispatch_kernel(pos_a_ref, pos_b_ref, last_tile_ref, has_rows_ref, nvalid_ref,
                     up_ref, us_ref, xs_ref, zeros_ref, sem, zsem, *, n_blocks_p, n_tiles, min_tiles):
    i = pl.program_id(0)
    rt = ROW_DMA_TILE

    @pl.when(i == 0)
    def _():
        zeros_ref[...] = jnp.zeros(zeros_ref.shape, F32)
        pairs = _zero_fill_copies(zeros_ref, xs_ref, last_tile_ref, has_rows_ref, nvalid_ref,
                                  zsem, n_tiles, min_tiles)
        for cond, cp in pairs:
            @pl.when(cond)
            def _():
                cp.start()
        for cond, cp in pairs:
            @pl.when(cond)
            def _():
                cp.wait()

    def scatter(src_ref):
        for r in range(rt):
            tok = i * rt + r
            _row_copy(src_ref, r, xs_ref, pos_a_ref[tok], sem).start(priority=0)
            _row_copy(src_ref, r, xs_ref, pos_b_ref[tok], sem).start(priority=1)
        for _ in range(2):
            pltpu.make_async_copy(src_ref, xs_ref.at[pl.ds(0, rt)], sem).wait()

    @pl.when(i < n_blocks_p)
    def _():
        scatter(up_ref)

    @pl.when(i >= n_blocks_p)
    def _():
        scatter(us_ref)


def _dispatch(pos_a, pos_b, last_tile, has_rows, nvalid, u_p, u_s, n_tiles, min_tiles):
    rt = ROW_DMA_TILE
    nbp = u_p.shape[0] // rt
    nbs = u_s.shape[0] // rt
    grid_spec = pltpu.PrefetchScalarGridSpec(
        num_scalar_prefetch=5,
        grid=(nbp + nbs,),
        in_specs=[
            pl.BlockSpec((rt, D_MODEL), lambda i, *_: (jnp.minimum(i, nbp - 1), 0)),
            pl.BlockSpec((rt, D_MODEL), lambda i, *_: (jnp.maximum(i - nbp, 0), 0)),
        ],
        out_specs=pl.BlockSpec(memory_space=pl.ANY),
        scratch_shapes=[
            pltpu.VMEM((MOE_TILE, D_MODEL), F32),
            pltpu.SemaphoreType.DMA,
            pltpu.SemaphoreType.DMA,
        ],
    )
    return pl.pallas_call(
        functools.partial(_dispatch_kernel, n_blocks_p=nbp, n_tiles=n_tiles, min_tiles=min_tiles),
        grid_spec=grid_spec,
        out_shape=jax.ShapeDtypeStruct((n_tiles * MOE_TILE, D_MODEL), F32),
        compiler_params=_params(1),
        name="moe_dispatch",
    )(pos_a, pos_b, last_tile, has_rows, nvalid, u_p, u_s)


def _moe_kernel(te_ref, quarters_ref, nv_ref, xs_ref, w1_ref, w3_ref, w2_ref, y_ref, x_ref, h_ref, sem):
    i = pl.program_id(0)
    k = pl.program_id(1)
    tm = MOE_TILE
    valid = i < nv_ref[0]
    quarters = quarters_ref[i]
    n_chunks = pl.num_programs(1) - 1
    weights = (lambda: w1_ref[0], lambda: w3_ref[0], lambda: w2_ref[0])

    def fetch(tile):
        return pltpu.make_async_copy(xs_ref.at[pl.ds(pl.multiple_of(tile * tm, tm), tm)], x_ref, sem)

    def zero_tile():
        y_ref[...] = jnp.zeros(y_ref.shape, F32)

    def first():
        zero_tile()

        @pl.when(i == 0)
        def _():
            fetch(0).start()

        fetch(i).wait()

    def last():
        @pl.when(i + 1 < nv_ref[0])
        def _():
            fetch(i + 1).start()

    for n in range(1, MOE_QUARTERS + 1):
        rows = n * (tm // MOE_QUARTERS)
        part = (x_ref.at[pl.ds(0, rows)], h_ref.at[pl.ds(0, rows)], y_ref.at[pl.ds(0, rows)])

        @pl.when(valid & (quarters == n))
        def _():
            _swiglu_skewed(k, n_chunks, [part], *weights, first, last)

    @pl.when(jnp.logical_not(valid) & (k == 0))
    def _():
        zero_tile()


def _moe(tile_expert, tile_quarters, nvalid, xs, w1, w3, w2):
    tm, tf = MOE_TILE, MOE_FF_TILE
    n_tiles = xs.shape[0] // tm
    kf = w1.shape[2] // tf

    def out_row(i, k, te, full, nv):
        return (i, 0)

    def up(i, k, te, full, nv):
        return (te[jnp.minimum(i, nv[0] - 1)], 0, jnp.where(i < nv[0], jnp.minimum(k, kf - 1), kf - 1))

    def down(i, k, te, full, nv):
        return (te[jnp.minimum(i, nv[0] - 1)], jnp.where(i < nv[0], jnp.maximum(k - 1, 0), kf - 1), 0)

    grid_spec = pltpu.PrefetchScalarGridSpec(
        num_scalar_prefetch=3,
        grid=(n_tiles, kf + 1),
        in_specs=[
            pl.BlockSpec(memory_space=pl.ANY),
            pl.BlockSpec((1, D_MODEL, tf), up),
            pl.BlockSpec((1, D_MODEL, tf), up),
            pl.BlockSpec((1, tf, D_MODEL), down),
        ],
        out_specs=pl.BlockSpec((tm, D_MODEL), out_row),
        scratch_shapes=[
            pltpu.VMEM((tm, D_MODEL), F32),
            pltpu.VMEM((tm, tf), F32),
            pltpu.SemaphoreType.DMA,
        ],
    )
    return pl.pallas_call(
        _moe_kernel,
        grid_spec=grid_spec,
        out_shape=jax.ShapeDtypeStruct(xs.shape, F32),
        compiler_params=_params(2),
        name="moe_ffn",
    )(tile_expert, tile_quarters, nvalid, xs, w1, w3, w2)


def _combine_kernel(pos_a_ref, pos_b_ref, h_ref, ga_ref, gb_ref, g_ref, y_ref, o_ref, ya_ref, yb_ref, sem, *, base):
    i = pl.program_id(0)
    rt = ROW_DMA_TILE

    def fetch(step, slot):
        for r in range(rt):
            tok = base + step * rt + r
            _row_copy(y_ref, pos_a_ref[tok], ya_ref.at[slot], r, sem.at[slot]).start(priority=0)
            _row_copy(y_ref, pos_b_ref[tok], yb_ref.at[slot], r, sem.at[slot]).start(priority=1)

    @pl.when(i == 0)
    def _():
        fetch(0, 0)

    @pl.when(i + 1 < pl.num_programs(0))
    def _():
        fetch(i + 1, (i + 1) % 2)

    slot = i % 2
    pltpu.make_async_copy(y_ref.at[pl.ds(0, rt)], ya_ref.at[slot], sem.at[slot]).wait()
    pltpu.make_async_copy(y_ref.at[pl.ds(0, rt)], yb_ref.at[slot], sem.at[slot]).wait()
    moe = ga_ref[...] * ya_ref[slot] + gb_ref[...] * yb_ref[slot]
    o_ref[...] = _rmsnorm(h_ref[...] + moe, g_ref[...])


def _combine(pos_a, pos_b, h, gate_a, gate_b, g, y, base):
    rt = ROW_DMA_TILE
    m = h.shape[0]
    row = lambda i, *_: (i, 0)
    grid_spec = pltpu.PrefetchScalarGridSpec(
        num_scalar_prefetch=2,
        grid=(m // rt,),
        in_specs=[
            pl.BlockSpec((rt, D_MODEL), row),
            pl.BlockSpec((rt, 1), row),
            pl.BlockSpec((rt, 1), row),
            pl.BlockSpec((1, D_MODEL), lambda i, *_: (0, 0)),
            pl.BlockSpec(memory_space=pl.ANY),
        ],
        out_specs=pl.BlockSpec((rt, D_MODEL), row),
        scratch_shapes=[
            pltpu.VMEM((2, rt, D_MODEL), F32),
            pltpu.VMEM((2, rt, D_MODEL), F32),
            pltpu.SemaphoreType.DMA((2,)),
        ],
    )
    return pl.pallas_call(
        functools.partial(_combine_kernel, base=base),
        grid_spec=grid_spec,
        out_shape=jax.ShapeDtypeStruct((m, D_MODEL), F32),
        compiler_params=_params(1),
        name="moe_combine",
    )(pos_a, pos_b, h, gate_a, gate_b, g, y)


def _token_cumsum(sel):
    n_e, n_tok = sel.shape
    blocks = n_tok // LANES
    s = sel.astype(F32).reshape(n_e, blocks, LANES)
    idx = jnp.arange(LANES)
    within = jnp.einsum("ebl,lm->ebm", s, (idx[:, None] <= idx[None, :]).astype(F32),
                        precision=lax.Precision.HIGHEST)
    totals = within[:, :, -1]
    bidx = jnp.arange(blocks)
    offsets = jnp.einsum("eb,bc->ec", totals, (bidx[:, None] < bidx[None, :]).astype(F32),
                         precision=lax.Precision.HIGHEST)
    return (within + offsets[:, :, None]).astype(jnp.int32).reshape(n_e, n_tok)


def _route_tables(sel, comb, n_tiles):
    tm = MOE_TILE
    sel = sel.T
    comb = comb.T
    cum = _token_cumsum(sel)
    counts = cum[:, -1]
    padded = ((counts + tm - 1) // tm) * tm
    ends = jnp.cumsum(padded)
    starts = ends - padded
    slot = starts[:, None] + cum - sel
    order = jnp.cumsum(sel, axis=0)
    first = (sel == 1) & (order == 1)
    second = (sel == 1) & (order == 2)
    pos_a = jnp.sum(jnp.where(first, slot, 0), axis=0).astype(jnp.int32)
    pos_b = jnp.sum(jnp.where(second, slot, 0), axis=0).astype(jnp.int32)
    gate_a = jnp.sum(jnp.where(first, comb, 0.0), axis=0)[:, None]
    gate_b = jnp.sum(jnp.where(second, comb, 0.0), axis=0)[:, None]
    tile_start = jnp.arange(n_tiles, dtype=jnp.int32) * tm
    tile_expert = jnp.minimum(jnp.sum(tile_start[:, None] >= ends[None, :], axis=1), N_EXPERTS - 1)
    rows_in_tile = (starts + counts)[tile_expert] - tile_start
    quarter = tm // MOE_QUARTERS
    tile_quarters = jnp.clip((rows_in_tile + quarter - 1) // quarter, 0, MOE_QUARTERS).astype(jnp.int32)
    nvalid = (ends[-1:] // tm).astype(jnp.int32)
    last_tile = jnp.maximum(ends - tm, 0).astype(jnp.int32)
    has_rows = (counts > 0).astype(jnp.int32)
    return (pos_a, pos_b, gate_a, gate_b, tile_expert.astype(jnp.int32), tile_quarters, nvalid,
            last_tile, has_rows)


def kernel(x_prompt, x_sample, state_pool, cache_k, cache_v, meta_tokens, g_pool, w_pool, ls_pool, g_ffn0, w_ff1, w_ff3, w_ff2, g_kv, w_kv, b_kv, g_attn, w_q, b_q, sinks, w_o, b_o, g_ffn1, w_router, w_e1, w_e3, w_e2, g_final):
    n_seq, seq, _ = x_prompt.shape
    n_dec, n_t, _ = x_sample.shape
    past_len = PAST_LEN
    window = cache_k.shape[1]
    assert window == WINDOW and seq % POOL_TILE == 0 and N_META <= POOL_TILE

    vec = lambda a: a.reshape(1, -1).astype(F32)
    g_pool, ls_pool, g_ffn0, g_kv, b_kv, g_attn, b_q, b_o, g_ffn1, g_final = map(
        vec, (g_pool, ls_pool, g_ffn0, g_kv, b_kv, g_attn, b_q, b_o, g_ffn1, g_final))
    w_pool, w_ff1, w_ff3, w_ff2 = (w.astype(BF16) for w in (w_pool, w_ff1, w_ff3, w_ff2))
    w_kv, w_q, w_o = (w.astype(F32) for w in (w_kv, w_q, w_o))
    w_router = jnp.pad(w_router, ((0, 0), (0, LANES - N_EXPERTS))).astype(BF16)
    meta = meta_tokens.astype(F32)

    def layer0_tail(h1, tm_ffn, tm_proj):
        h2 = _ffn(h1, g_ffn0, w_ff1, w_ff3, w_ff2, min(tm_ffn, h1.shape[0]))
        kv, q = _qkv(h2, g_kv, w_kv, b_kv, g_attn, w_q, b_q, tm_proj)
        return h2, kv, q

    x_meta = jnp.pad(meta, ((0, POOL_TILE - N_META), (0, 0)))[None]
    h1_m, _ = _pool_seq(x_meta, jnp.zeros_like(meta), g_pool, w_pool, ls_pool, 0)

    h1_s, pool_s = _pool_step(x_sample.transpose(1, 0, 2), state_pool.transpose(1, 0, 2),
                              g_pool, w_pool, ls_pool, past_len)
    n_s = n_t * n_dec
    h1_ms = jnp.concatenate([h1_m[0], h1_s.reshape(n_s, D_MODEL)])
    h2_ms, kv_ms, q_ms = layer0_tail(h1_ms, h1_ms.shape[0], POOL_TILE)
    h2_s, kv_s, q_s = h2_ms[POOL_TILE:], kv_ms[POOL_TILE:], q_ms[POOL_TILE:]
    past_kv = jnp.pad(kv_ms[:N_META], ((WINDOW - N_META, 0), (0, 0)))

    h1_p, pool_p = _pool_seq(x_prompt, meta, g_pool, w_pool, ls_pool, N_META)
    h2_p, kv_p, q_p = layer0_tail(h1_p.reshape(n_seq * seq, D_MODEL), FFN_TILE, QKV_TILE)
    o_p = _attn_seq(q_p, kv_p, past_kv, sinks.astype(F32), n_seq, N_META)
    h3_p, u_p, comb_p, sel_p = _oproj(o_p, h2_p, w_o, b_o, g_ffn1, w_router, PROJ_TILE)
    kv_tail = kv_p.reshape(n_seq, seq, 2 * HKV)[:, seq - WINDOW:]
    k_p = kv_tail[..., :HKV].reshape(n_seq, WINDOW, N_KV_HEADS, HEAD_DIM)
    v_p = kv_tail[..., HKV:].reshape(n_seq, WINDOW, N_KV_HEADS, HEAD_DIM)

    q5 =q_s.reshape(n_t, n_dec, N_KV_HEADS, GROUP, HEAD_DIM).transpose(1, 2, 0, 3, 4)
    eye = jnp.eye(N_KV_HEADS, dtype=BF16)
    q_bd = (q5[:, :, :, :, None, :] * eye[None, :, None, None, :, None]).reshape(
        n_dec, N_KV_HEADS * n_t * GROUP, HKV)
    new_kv = jnp.pad(kv_s.reshape(n_t, n_dec, 2 * HKV).transpose(1, 0, 2),
                     ((0, 0), (0, BF16_SUBLANES - n_t), (0, 0)))
    sink_rows = jnp.broadcast_to(sinks.astype(F32).reshape(N_KV_HEADS, 1, GROUP),
                                 (N_KV_HEADS, n_t, GROUP)).reshape(-1, 1)
    o_s4, k_s, v_s = _attn_step(q_bd, new_kv, cache_k.reshape(n_dec, WINDOW, HKV),
                                cache_v.reshape(n_dec, WINDOW, HKV), sink_rows, past_len, n_t)
    o_s = o_s4.reshape(n_dec, N_KV_HEADS, n_t, GROUP, HEAD_DIM).transpose(2, 0, 1, 3, 4).reshape(n_s, D_MODEL)
    h3_s, u_s, comb_s, sel_s = _oproj(o_s, h2_s, w_o, b_o, g_ffn1, w_router, PROJ_TILE)

    n_tok = n_seq * seq + n_s
    min_tiles = 2 * n_tok // MOE_TILE
    n_tiles = min_tiles + N_EXPERTS
    pos_a, pos_b, gate_a, gate_b, tile_expert, tile_quarters, nvalid, last_tile, has_rows = _route_tables(
        jnp.concatenate([sel_p, sel_s]), jnp.concatenate([comb_p, comb_s]), n_tiles)
    xs = _dispatch(pos_a, pos_b, last_tile, has_rows, nvalid, u_p, u_s, n_tiles, min_tiles)
    y = _moe(tile_expert, tile_quarters, nvalid, xs, w_e1.astype(F32), w_e3.astype(F32), w_e2.astype(F32))
    n_p = n_seq * seq
    y_p = _combine(pos_a, pos_b, h3_p, gate_a[:n_p], gate_b[:n_p], g_final, y, 0)
    y_s = _combine(pos_a, pos_b, h3_s, gate_a[n_p:], gate_b[n_p:], g_final, y, n_p)

    return (y_p.reshape(n_seq, seq, D_MODEL),
            y_s.reshape(n_t, n_dec, D_MODEL).transpose(1, 0, 2),
            pool_p,
            pool_s.transpose(1, 0, 2),
            k_p, v_p,
            k_s.reshape(n_dec, WINDOW, N_KV_HEADS, HEAD_DIM),
            v_s.reshape(n_dec, WINDOW, N_KV_HEADS, HEAD_DIM))
```

```python
import functools

import jax
import jax.numpy as jnp
from jax import lax
from jax.experimental import pallas as pl
from jax.experimental.pallas import tpu as pltpu

F32 = jnp.float32
BF16 = jnp.bfloat16

D_MODEL = 2048
N_META = 16
POOL_WINDOWS = (2, 4, 8, 16)
POOL_GROUP_DIM = D_MODEL // len(POOL_WINDOWS)
POOL_STATE = max(POOL_WINDOWS) - 1
HEAD_DIM = 64
N_HEADS = D_MODEL // HEAD_DIM
N_KV_HEADS = 4
GROUP = N_HEADS // N_KV_HEADS
HKV = N_KV_HEADS * HEAD_DIM
WINDOW = 128
N_EXPERTS = 8
EPS = 1e-5
PAST_LEN = 8192
MASKED = -1e30

LANES = 128
V7X_VMEM_BYTES = 64 * 2 ** 20
VMEM_LIMIT = V7X_VMEM_BYTES - 8 * 2 ** 20

POOL_TILE = 128
ATTN_TILE = WINDOW
MOE_TILE = 1024
MOE_QUARTERS = 4
MOE_FF_TILE = 512
FFN_FF_TILE = 512
FFN_TILE = 1024
PROJ_TILE = 256
QKV_TILE = 512
ROW_DMA_TILE = 256
BF16_SUBLANES = 16


def _dot(a, b):
    return jnp.dot(a, b, preferred_element_type=F32)


def _dot_nt(a, b):
    return lax.dot_general(a, b, (((1,), (1,)), ((), ())), preferred_element_type=F32)


def _rmsnorm(x, g):
    return x * lax.rsqrt(jnp.mean(x * x, axis=-1, keepdims=True) + EPS) * g


def _params(n_axes):
    return pltpu.CompilerParams(dimension_semantics=("arbitrary",) * n_axes,
                                vmem_limit_bytes=VMEM_LIMIT)


def _pool_seq_kernel(x_ref, meta_ref, g_ref, w_ref, ls_ref, h_ref, st_ref, hi_ref, lo_ref, u_ref, *, pos0):
    t = pl.program_id(1)
    tt = POOL_TILE
    g = g_ref[...]

    def split(rows):
        hi = rows.astype(BF16)
        return hi, (rows - hi.astype(F32)).astype(BF16)

    @pl.when(t == 0)
    def _():
        zeros = jnp.zeros((tt - N_META, D_MODEL), BF16)
        hi_ref[0:tt - N_META, :] = zeros
        lo_ref[0:tt - N_META, :] = zeros
        hi_ref[tt - N_META:tt, :], lo_ref[tt - N_META:tt, :] = split(_rmsnorm(meta_ref[...], g))

    @pl.when(t > 0)
    def _():
        hi_ref[0:tt, :] = hi_ref[tt:2 * tt, :]
        lo_ref[0:tt, :] = lo_ref[tt:2 * tt, :]

    x = x_ref[0]
    u = _rmsnorm(x, g)
    u_ref[...] = u
    hi_ref[tt:2 * tt, :], lo_ref[tt:2 * tt, :] = split(u)
    hi = hi_ref[...]
    lo = lo_ref[...]
    r = lax.broadcasted_iota(jnp.int32, (tt, 2 * tt), 0)
    c = lax.broadcasted_iota(jnp.int32, (tt, 2 * tt), 1)
    pos = pos0 + t * tt + lax.broadcasted_iota(jnp.int32, (tt, 1), 0)
    for gi, w in enumerate(POOL_WINDOWS):
        sl = slice(gi * POOL_GROUP_DIM, (gi + 1) * POOL_GROUP_DIM)
        band = jnp.where((c > r + tt - w) & (c <= r + tt), 1.0, 0.0).astype(BF16)
        win_sum = _dot(band, hi[:, sl]) + _dot(band, lo[:, sl])
        cnt = jnp.minimum(w, pos + 1).astype(F32)
        mix = win_sum / cnt - u[:, sl]
        o = _dot(mix.astype(BF16), w_ref[gi]) * ls_ref[:, sl]
        h_ref[0, :, sl] = x[:, sl] + o

    @pl.when(t == pl.num_programs(1) - 1)
    def _():
        st_ref[0] = u_ref[tt - POOL_STATE:tt, :]


def _pool_seq(x, meta, g, w, ls, pos0):
    b, t, _ = x.shape
    tt = POOL_TILE
    return pl.pallas_call(
        functools.partial(_pool_seq_kernel, pos0=pos0),
        grid=(b, t // tt),
        in_specs=[
            pl.BlockSpec((1, tt, D_MODEL), lambda i, j: (i, j, 0)),
            pl.BlockSpec((N_META, D_MODEL), lambda i, j: (0, 0)),
            pl.BlockSpec((1, D_MODEL), lambda i, j: (0, 0)),
            pl.BlockSpec((len(POOL_WINDOWS), POOL_GROUP_DIM, POOL_GROUP_DIM), lambda i, j: (0, 0, 0)),
            pl.BlockSpec((1, D_MODEL), lambda i, j: (0, 0)),
        ],
        out_specs=[
            pl.BlockSpec((1, tt, D_MODEL), lambda i, j: (i, j, 0)),
            pl.BlockSpec((1, POOL_STATE, D_MODEL), lambda i, j: (i, 0, 0)),
        ],
        out_shape=[
            jax.ShapeDtypeStruct((b, t, D_MODEL), F32),
            jax.ShapeDtypeStruct((b, POOL_STATE, D_MODEL), F32),
        ],
        scratch_shapes=[pltpu.VMEM((2 * tt, D_MODEL), BF16), pltpu.VMEM((2 * tt, D_MODEL), BF16),
                        pltpu.VMEM((tt, D_MODEL), F32)],
        compiler_params=_params(2),
        name="pool_seq",
    )(x, meta, g, w, ls)


def _pool_step_kernel(x_ref, st_ref, g_ref, w_ref, ls_ref, h_ref, nst_ref, e_ref, *, pos0):
    n_t, bb, _ = x_ref.shape
    g = g_ref[...]
    for j in range(POOL_STATE):
        e_ref[j] = st_ref[j]
    for t in range(n_t):
        e_ref[POOL_STATE + t] = _rmsnorm(x_ref[t], g)
    for j in range(POOL_STATE):
        nst_ref[j] = e_ref[j + n_t]
    for gi, w in enumerate(POOL_WINDOWS):
        sl = slice(gi * POOL_GROUP_DIM, (gi + 1) * POOL_GROUP_DIM)
        mixes = []
        for t in range(n_t):
            win_sum = e_ref[POOL_STATE + t, :, sl]
            for i in range(1, w):
                win_sum = win_sum + e_ref[POOL_STATE + t - i, :, sl]
            cnt = float(min(w, pos0 + t + 1))
            mixes.append(win_sum / cnt - e_ref[POOL_STATE + t, :, sl])
        mix = jnp.concatenate(mixes, axis=0)
        o = _dot(mix.astype(BF16), w_ref[gi]) * ls_ref[:, sl]
        for t in range(n_t):
            h_ref[t, :, sl] = x_ref[t, :, sl] + o[t * bb:(t + 1) * bb]


def _pool_step(x_t, st_t, g, w, ls, pos0, bb=32):
    n_t, b, _ = x_t.shape
    return pl.pallas_call(
        functools.partial(_pool_step_kernel, pos0=pos0),
        grid=(b // bb,),
        in_specs=[
            pl.BlockSpec((n_t, bb, D_MODEL), lambda i: (0, i, 0)),
            pl.BlockSpec((POOL_STATE, bb, D_MODEL), lambda i: (0, i, 0)),
            pl.BlockSpec((1, D_MODEL), lambda i: (0, 0)),
            pl.BlockSpec((len(POOL_WINDOWS), POOL_GROUP_DIM, POOL_GROUP_DIM), lambda i: (0, 0, 0)),
            pl.BlockSpec((1, D_MODEL), lambda i: (0, 0)),
        ],
        out_specs=[
            pl.BlockSpec((n_t, bb, D_MODEL), lambda i: (0, i, 0)),
            pl.BlockSpec((POOL_STATE, bb, D_MODEL), lambda i: (0, i, 0)),
        ],
        out_shape=[
            jax.ShapeDtypeStruct((n_t, b, D_MODEL), F32),
            jax.ShapeDtypeStruct((POOL_STATE, b, D_MODEL), F32),
        ],
        scratch_shapes=[pltpu.VMEM((POOL_STATE + n_t, bb, D_MODEL), F32)],
        compiler_params=_params(1),
        name="pool_step",
    )(x_t, st_t, g, w, ls)


def _swiglu_skewed(k, n_chunks, parts, w1, w3, w2, first, last=lambda: None):
    def up():
        w_gate, w_lin = w1(), w3()
        for u_ref, h_ref, _ in parts:
            u = u_ref[...]
            h_ref[...] = (jax.nn.silu(_dot(u, w_gate)) * _dot(u, w_lin)).astype(h_ref.dtype)

    def down():
        w_out = w2()
        for _, h_ref, acc_ref in parts:
            acc_ref[...] += _dot(h_ref[...], w_out)

    @pl.when(k == 0)
    def _():
        first()
        up()

    @pl.when((k > 0) & (k < n_chunks))
    def _():
        down()
        up()

    @pl.when(k == n_chunks)
    def _():
        last()
        down()


def _ffn_kernel(x_ref, g_ref, w1_ref, w3_ref, w2_ref, o_ref, u_ref, h_ref):
    def first():
        x = x_ref[...]
        u_ref[...] = _rmsnorm(x, g_ref[...]).astype(BF16)
        o_ref[...] = x

    _swiglu_skewed(pl.program_id(1), pl.num_programs(1) - 1, [(u_ref, h_ref, o_ref)],
                   lambda: w1_ref[...], lambda: w3_ref[...], lambda: w2_ref[...], first)


def _ffn(x, g, w1, w3, w2, tm):
    m = x.shape[0]
    f = w1.shape[1]
    tf = FFN_FF_TILE
    kf = f // tf
    return pl.pallas_call(
        _ffn_kernel,
        grid=(m // tm, kf + 1),
        in_specs=[
            pl.BlockSpec((tm, D_MODEL), lambda i, k: (i, 0)),
            pl.BlockSpec((1, D_MODEL), lambda i, k: (0, 0)),
            pl.BlockSpec((D_MODEL, tf), lambda i, k: (0, jnp.minimum(k, kf - 1))),
            pl.BlockSpec((D_MODEL, tf), lambda i, k: (0, jnp.minimum(k, kf - 1))),
            pl.BlockSpec((tf, D_MODEL), lambda i, k: (jnp.maximum(k - 1, 0), 0)),
        ],
        out_specs=pl.BlockSpec((tm, D_MODEL), lambda i, k: (i, 0)),
        out_shape=jax.ShapeDtypeStruct((m, D_MODEL), F32),
        scratch_shapes=[pltpu.VMEM((tm, D_MODEL), BF16), pltpu.VMEM((tm, tf), BF16)],
        compiler_params=_params(2),
        name="ffn0",
    )(x, g, w1, w3, w2)


def _qkv_kernel(x_ref, gkv_ref, wkv_ref, bkv_ref, gq_ref, wq_ref, bq_ref, kv_ref, q_ref):
    x = x_ref[...]
    xn = x * lax.rsqrt(jnp.mean(x * x, axis=-1, keepdims=True) + EPS)
    kv_ref[...] = _dot(xn * gkv_ref[...], wkv_ref[...]) + bkv_ref[...]
    q = _dot(xn * gq_ref[...], wq_ref[...]) + bq_ref[...]
    q_ref[...] = (q * HEAD_DIM ** -0.5).astype(BF16)


def _qkv(x, gkv, wkv, bkv, gq, wq, bq, tm):
    m = x.shape[0]
    row = lambda i: (i, 0)
    fixed = lambda i: (0, 0)
    return pl.pallas_call(
        _qkv_kernel,
        grid=(m // tm,),
        in_specs=[
            pl.BlockSpec((tm, D_MODEL), row),
            pl.BlockSpec((1, D_MODEL), fixed),
            pl.BlockSpec((D_MODEL, 2 * HKV), fixed, pipeline_mode=pl.Buffered(1)),
            pl.BlockSpec((1, 2 * HKV), fixed),
            pl.BlockSpec((1, D_MODEL), fixed),
            pl.BlockSpec((D_MODEL, D_MODEL), fixed, pipeline_mode=pl.Buffered(1)),
            pl.BlockSpec((1, D_MODEL), fixed),
        ],
        out_specs=[pl.BlockSpec((tm, 2 * HKV), row), pl.BlockSpec((tm, D_MODEL), row)],
        out_shape=[jax.ShapeDtypeStruct((m, 2 * HKV), F32), jax.ShapeDtypeStruct((m, D_MODEL), BF16)],
        compiler_params=_params(1),
        name="qkv",
    )(x, gkv, wkv, bkv, gq, wq, bq)


def _attn_seq_kernel(sinks_ref, q_ref, cur_ref, prev_ref, past_ref, o_ref, *, pos0):
    j = pl.program_id(1)
    tq = ATTN_TILE
    prev = jnp.where(j == 0, past_ref[...], prev_ref[...])
    band = jnp.concatenate([prev, cur_ref[...]], axis=0)
    c = lax.broadcasted_iota(jnp.int32, (2 * tq, tq), 0)
    r = lax.broadcasted_iota(jnp.int32, (2 * tq, tq), 1)
    key_pos = pos0 + (j - 1) * tq + c
    valid = (c > r) & (c <= r + tq) & (key_pos >= 0)
    low = lax.broadcasted_iota(jnp.int32, (2 * tq, LANES), 1) < HEAD_DIM
    zeros = jnp.zeros((2 * tq, LANES), F32)
    k_bd, v_bd_t = [], []
    v_t = band[:, HKV:].T
    zv = jnp.zeros((HEAD_DIM, 2 * tq), F32)
    for kvh in range(N_KV_HEADS):
        tile = band[:, (kvh // 2) * LANES:(kvh // 2 + 1) * LANES]
        swapped = pltpu.roll(tile, HEAD_DIM, axis=1)
        in_low, in_high = (tile, swapped) if kvh % 2 == 0 else (swapped, tile)
        k_bd.append(jnp.concatenate([jnp.where(low, in_low, zeros), jnp.where(low, zeros, in_high)],
                                    axis=0).astype(BF16))
        vt = v_t[kvh * HEAD_DIM:(kvh + 1) * HEAD_DIM]
        v_bd_t.append(jnp.concatenate([jnp.concatenate([vt, zv], axis=1),
                                       jnp.concatenate([zv, vt], axis=1)], axis=0).astype(BF16))
    for pair in range(N_HEADS // 2):
        kvh = (2 * pair) // GROUP
        ps = slice(pair * LANES, (pair + 1) * LANES)
        s_t = _dot_nt(k_bd[kvh], q_ref[:, ps])
        probs, inv = [], []
        for half in range(2):
            s = jnp.where(valid, s_t[half * 2 * tq:(half + 1) * 2 * tq], MASKED)
            sink = sinks_ref[2 * pair + half]
            m = jnp.maximum(jnp.max(s, axis=0, keepdims=True), sink)
            p = jnp.exp(s - m)
            denom = jnp.sum(p, axis=0, keepdims=True) + jnp.exp(sink - m)
            probs.append(p.astype(BF16))
            inv.append(jnp.broadcast_to(1.0 / denom, (HEAD_DIM, tq)))
        o_t = _dot(v_bd_t[kvh], jnp.concatenate(probs, axis=0)) * jnp.concatenate(inv, axis=0)
        o_ref[:, ps] = o_t.T.astype(BF16)


def _attn_seq(q, kv, past_kv, sinks, n_seq, pos0):
    m = q.shape[0]
    tq = ATTN_TILE
    nb = m // n_seq // tq
    return pl.pallas_call(
        functools.partial(_attn_seq_kernel, pos0=pos0),
        grid=(n_seq, nb),
        in_specs=[
            pl.BlockSpec(memory_space=pltpu.SMEM),
            pl.BlockSpec((tq, D_MODEL), lambda b, j: (b * nb + j, 0)),
            pl.BlockSpec((tq, 2 * HKV), lambda b, j: (b * nb + j, 0)),
            pl.BlockSpec((tq, 2 * HKV), lambda b, j: (b * nb + jnp.maximum(j - 1, 0), 0)),
            pl.BlockSpec((tq, 2 * HKV), lambda b, j: (0, 0)),
        ],
        out_specs=pl.BlockSpec((tq, D_MODEL), lambda b, j: (b * nb + j, 0)),
        out_shape=jax.ShapeDtypeStruct((m, D_MODEL), BF16),
        compiler_params=_params(2),
        name="attn_seq",
    )(sinks, q, kv, kv, past_kv)


def _attn_step_kernel(q_ref, new_ref, ck_ref, cv_ref, sink_ref, o_ref, nk_ref, nv_ref, *, pos0, n_t):
    bb = q_ref.shape[0]
    rows = N_KV_HEADS * n_t * GROUP
    pad = WINDOW - new_ref.shape[1]
    rho = lax.broadcasted_iota(jnp.int32, (rows, 2 * WINDOW), 0)
    c = lax.broadcasted_iota(jnp.int32, (rows, 2 * WINDOW), 1)
    t = (rho // GROUP) % n_t
    valid = (c > t) & (c <= t + WINDOW) & (pos0 - WINDOW + c >= 0)
    sink = sink_ref[...]
    for i in range(bb):
        new = jnp.concatenate([new_ref[i], jnp.zeros((pad, 2 * HKV), F32)], axis=0)
        kcat = jnp.concatenate([ck_ref[i], new[:, :HKV]], axis=0).astype(BF16)
        vcat = jnp.concatenate([cv_ref[i], new[:, HKV:]], axis=0).astype(BF16)
        s = jnp.where(valid, _dot_nt(q_ref[i], kcat), MASKED)
        m = jnp.maximum(jnp.max(s, axis=-1, keepdims=True), sink)
        p = jnp.exp(s - m)
        denom = jnp.sum(p, axis=-1, keepdims=True) + jnp.exp(sink - m)
        o = (_dot(p.astype(BF16), vcat) / denom).astype(BF16)
        for kvh in range(N_KV_HEADS):
            rs = n_t * GROUP
            o_ref[i, kvh] = o[kvh * rs:(kvh + 1) * rs, kvh * HEAD_DIM:(kvh + 1) * HEAD_DIM]
        nk_ref[i, 0:WINDOW - n_t, :] = ck_ref[i, n_t:WINDOW, :]
        nk_ref[i, WINDOW - n_t:WINDOW, :] = new_ref[i, 0:n_t, 0:HKV]
        nv_ref[i, 0:WINDOW - n_t, :] = cv_ref[i, n_t:WINDOW, :]
        nv_ref[i, WINDOW - n_t:WINDOW, :] = new_ref[i, 0:n_t, HKV:2 * HKV]


def _attn_step(q_bd, new_kv, cache_k, cache_v, sink_rows, pos0, n_t, bb=16):
    b = q_bd.shape[0]
    rows = q_bd.shape[1]
    n_new = new_kv.shape[1]
    blk = lambda *s: pl.BlockSpec((bb,) + s, lambda i: (i,) + (0,) * len(s))
    return pl.pallas_call(
        functools.partial(_attn_step_kernel, pos0=pos0, n_t=n_t),
        grid=(b // bb,),
        in_specs=[
            blk(rows, HKV),
            blk(n_new, 2 * HKV),
            blk(WINDOW, HKV),
            blk(WINDOW, HKV),
            pl.BlockSpec((rows, 1), lambda i: (0, 0)),
        ],
        out_specs=[blk(N_KV_HEADS, n_t * GROUP, HEAD_DIM), blk(WINDOW, HKV), blk(WINDOW, HKV)],
        out_shape=[
            jax.ShapeDtypeStruct((b, N_KV_HEADS, n_t * GROUP, HEAD_DIM), BF16),
            jax.ShapeDtypeStruct((b, WINDOW, HKV), F32),
            jax.ShapeDtypeStruct((b, WINDOW, HKV), F32),
        ],
        compiler_params=_params(1),
        name="attn_step",
    )(q_bd, new_kv, cache_k, cache_v, sink_rows)


def _oproj_kernel(o_ref, h_ref, wo_ref, bo_ref, g_ref, wr_ref, h3_ref, u_ref, comb_ref, sel_ref):
    h3 = h_ref[...] + (_dot(o_ref[...].astype(F32), wo_ref[...]) + bo_ref[...])
    h3_ref[...] = h3
    u = _rmsnorm(h3, g_ref[...])
    u_ref[...] = u
    logits = _dot(u.astype(BF16), wr_ref[...])
    lane = lax.broadcasted_iota(jnp.int32, logits.shape, 1)
    lg = jnp.where(lane < N_EXPERTS, logits, -jnp.inf)
    m1 = jnp.max(lg, axis=-1, keepdims=True)
    i1 = jnp.min(jnp.where(lg == m1, lane, LANES), axis=-1, keepdims=True)
    top1 = lane == i1
    lg2 = jnp.where(top1, -jnp.inf, lg)
    m2 = jnp.max(lg2, axis=-1, keepdims=True)
    i2 = jnp.min(jnp.where(lg2 == m2, lane, LANES), axis=-1, keepdims=True)
    top2 = lane == i2
    e2 = jnp.exp(m2 - m1)
    denom = 1.0 + e2
    comb = jnp.where(top1, 1.0 / denom, 0.0) + jnp.where(top2, e2 / denom, 0.0)
    comb_ref[...] = comb[:, :N_EXPERTS]
    sel_ref[...] = jnp.where(top1 | top2, 1, 0).astype(jnp.int32)[:, :N_EXPERTS]


def _oproj(o, h, wo, bo, g, wr, tm):
    m = o.shape[0]
    row = lambda i: (i, 0)
    fixed = lambda i: (0, 0)
    return pl.pallas_call(
        _oproj_kernel,
        grid=(m // tm,),
        in_specs=[
            pl.BlockSpec((tm, D_MODEL), row),
            pl.BlockSpec((tm, D_MODEL), row),
            pl.BlockSpec((D_MODEL, D_MODEL), fixed, pipeline_mode=pl.Buffered(1)),
            pl.BlockSpec((1, D_MODEL), fixed),
            pl.BlockSpec((1, D_MODEL), fixed),
            pl.BlockSpec((D_MODEL, LANES), fixed),
        ],
        out_specs=[
            pl.BlockSpec((tm, D_MODEL), row),
            pl.BlockSpec((tm, D_MODEL), row),
            pl.BlockSpec((tm, N_EXPERTS), row),
            pl.BlockSpec((tm, N_EXPERTS), row),
        ],
        out_shape=[
            jax.ShapeDtypeStruct((m, D_MODEL), F32),
            jax.ShapeDtypeStruct((m, D_MODEL), F32),
            jax.ShapeDtypeStruct((m, N_EXPERTS), F32),
            jax.ShapeDtypeStruct((m, N_EXPERTS), jnp.int32),
        ],
        compiler_params=_params(1),
        name="oproj_router",
    )(o, h, wo, bo, g, wr)


def _row_copy(src, r, dst, p, sem):
    return pltpu.make_async_copy(src.at[pl.ds(r, 1)], dst.at[pl.ds(p, 1)], sem)


def _zero_fill_copies(zeros_ref, xs_ref, last_tile_ref, has_rows_ref, nvalid_ref, sem, n_tiles, min_tiles):
    tm = MOE_TILE
    pairs = []
    for e in range(N_EXPERTS):
        start = pl.multiple_of(last_tile_ref[e], tm)
        cp = pltpu.make_async_copy(zeros_ref, xs_ref.at[pl.ds(start, tm)], sem)
        pairs.append((has_rows_ref[e] != 0, cp))
    for tile in range(min_tiles, n_tiles):
        cp = pltpu.make_async_copy(zeros_ref, xs_ref.at[pl.ds(tile * tm, tm)], sem)
        pairs.append((tile >= nvalid_ref[0], cp))
    return pairs


def _dispatch_kernel(pos_a_ref, pos_b_ref, last_tile_ref, has_rows_ref, nvalid_ref,
                     up_ref, us_ref, xs_ref, zeros_ref, sem, zsem, *, n_blocks_p, n_tiles, min_tiles):
    i = pl.program_id(0)
    rt = ROW_DMA_TILE

    @pl.when(i == 0)
    def _():
        zeros_ref[...] = jnp.zeros(zeros_ref.shape, F32)
        pairs = _zero_fill_copies(zeros_ref, xs_ref, last_tile_ref, has_rows_ref, nvalid_ref,
                                  zsem, n_tiles, min_tiles)
        for cond, cp in pairs:
            @pl.when(cond)
            def _():
                cp.start()
        for cond, cp in pairs:
            @pl.when(cond)
            def _():
                cp.wait()

    def scatter(src_ref):
        for r in range(rt):
            tok = i * rt + r
            _row_copy(src_ref, r, xs_ref, pos_a_ref[tok], sem).start(priority=0)
            _row_copy(src_ref, r, xs_ref, pos_b_ref[tok], sem).start(priority=1)
        for _ in range(2):
            pltpu.make_async_copy(src_ref, xs_ref.at[pl.ds(0, rt)], sem).wait()

    @pl.when(i < n_blocks_p)
    def _():
        scatter(up_ref)

    @pl.when(i >= n_blocks_p)
    def _():
        scatter(us_ref)


def _dispatch(pos_a, pos_b, last_tile, has_rows, nvalid, u_p, u_s, n_tiles, min_tiles):
    rt = ROW_DMA_TILE
    nbp = u_p.shape[0] // rt
    nbs = u_s.shape[0] // rt
    grid_spec = pltpu.PrefetchScalarGridSpec(
        num_scalar_prefetch=5,
        grid=(nbp + nbs,),
        in_specs=[
            pl.BlockSpec((rt, D_MODEL), lambda i, *_: (jnp.minimum(i, nbp - 1), 0)),
            pl.BlockSpec((rt, D_MODEL), lambda i, *_: (jnp.maximum(i - nbp, 0), 0)),
        ],
        out_specs=pl.BlockSpec(memory_space=pl.ANY),
        scratch_shapes=[
            pltpu.VMEM((MOE_TILE, D_MODEL), F32),
            pltpu.SemaphoreType.DMA,
            pltpu.SemaphoreType.DMA,
        ],
    )
    return pl.pallas_call(
        functools.partial(_dispatch_kernel, n_blocks_p=nbp, n_tiles=n_tiles, min_tiles=min_tiles),
        grid_spec=grid_spec,
        out_shape=jax.ShapeDtypeStruct((n_tiles * MOE_TILE, D_MODEL), F32),
        compiler_params=_params(1),
        name="moe_dispatch",
    )(pos_a, pos_b, last_tile, has_rows, nvalid, u_p, u_s)


def _moe_kernel(te_ref, quarters_ref, nv_ref, xs_ref, w1_ref, w3_ref, w2_ref, y_ref, x_ref, h_ref, sem):
    i = pl.program_id(0)
    k = pl.program_id(1)
    tm = MOE_TILE
    valid = i < nv_ref[0]
    quarters = quarters_ref[i]
    n_chunks = pl.num_programs(1) - 1
    weights = (lambda: w1_ref[0], lambda: w3_ref[0], lambda: w2_ref[0])

    def fetch(tile):
        return pltpu.make_async_copy(xs_ref.at[pl.ds(pl.multiple_of(tile * tm, tm), tm)], x_ref, sem)

    def zero_tile():
        y_ref[...] = jnp.zeros(y_ref.shape, F32)

    def first():
        zero_tile()

        @pl.when(i == 0)
        def _():
            fetch(0).start()

        fetch(i).wait()

    def last():
        @pl.when(i + 1 < nv_ref[0])
        def _():
            fetch(i + 1).start()

    for n in range(1, MOE_QUARTERS + 1):
        rows = n * (tm // MOE_QUARTERS)
        part = (x_ref.at[pl.ds(0, rows)], h_ref.at[pl.ds(0, rows)], y_ref.at[pl.ds(0, rows)])

        @pl.when(valid & (quarters == n))
        def _():
            _swiglu_skewed(k, n_chunks, [part], *weights, first, last)

    @pl.when(jnp.logical_not(valid) & (k == 0))
    def _():
        zero_tile()


def _moe(tile_expert, tile_quarters, nvalid, xs, w1, w3, w2):
    tm, tf = MOE_TILE, MOE_FF_TILE
    n_tiles = xs.shape[0] // tm
    kf = w1.shape[2] // tf

    def out_row(i, k, te, full, nv):
        return (i, 0)

    def up(i, k, te, full, nv):
        return (te[jnp.minimum(i, nv[0] - 1)], 0, jnp.where(i < nv[0], jnp.minimum(k, kf - 1), kf - 1))

    def down(i, k, te, full, nv):
        return (te[jnp.minimum(i, nv[0] - 1)], jnp.where(i < nv[0], jnp.maximum(k - 1, 0), kf - 1), 0)

    grid_spec = pltpu.PrefetchScalarGridSpec(
        num_scalar_prefetch=3,
        grid=(n_tiles, kf + 1),
        in_specs=[
            pl.BlockSpec(memory_space=pl.ANY),
            pl.BlockSpec((1, D_MODEL, tf), up),
            pl.BlockSpec((1, D_MODEL, tf), up),
            pl.BlockSpec((1, tf, D_MODEL), down),
        ],
        out_specs=pl.BlockSpec((tm, D_MODEL), out_row),
        scratch_shapes=[
            pltpu.VMEM((tm, D_MODEL), F32),
            pltpu.VMEM((tm, tf), F32),
            pltpu.SemaphoreType.DMA,
        ],
    )
    return pl.pallas_call(
        _moe_kernel,
        grid_spec=grid_spec,
        out_shape=jax.ShapeDtypeStruct(xs.shape, F32),
        compiler_params=_params(2),
        name="moe_ffn",
    )(tile_expert, tile_quarters, nvalid, xs, w1, w3, w2)


def _combine_kernel(pos_a_ref, pos_b_ref, h_ref, ga_ref, gb_ref, g_ref, y_ref, o_ref, ya_ref, yb_ref, sem, *, base):
    i = pl.program_id(0)
    rt = ROW_DMA_TILE

    def fetch(step, slot):
        for r in range(rt):
            tok = base + step * rt + r
            _row_copy(y_ref, pos_a_ref[tok], ya_ref.at[slot], r, sem.at[slot]).start(priority=0)
            _row_copy(y_ref, pos_b_ref[tok], yb_ref.at[slot], r, sem.at[slot]).start(priority=1)

    @pl.when(i == 0)
    def _():
        fetch(0, 0)

    def step(slot, has_next):
        pltpu.make_async_copy(y_ref.at[pl.ds(0, rt)], ya_ref.at[slot], sem.at[slot]).wait()
        pltpu.make_async_copy(y_ref.at[pl.ds(0, rt)], yb_ref.at[slot], sem.at[slot]).wait()
        if has_next:
            fetch(i + 1, 1 - slot)
        moe = ga_ref[...] * ya_ref[slot] + gb_ref[...] * yb_ref[slot]
        o_ref[...] = _rmsnorm(h_ref[...] + moe, g_ref[...])

    more = i + 1 < pl.num_programs(0)
    for slot in range(2):
        parity = i % 2 == slot

        @pl.when(parity & more)
        def _():
            step(slot, True)

        @pl.when(parity & jnp.logical_not(more))
        def _():
            step(slot, False)


def _combine(pos_a, pos_b, h, gate_a, gate_b, g, y, base):
    rt = ROW_DMA_TILE
    m = h.shape[0]
    row = lambda i, *_: (i, 0)
    grid_spec = pltpu.PrefetchScalarGridSpec(
        num_scalar_prefetch=2,
        grid=(m // rt,),
        in_specs=[
            pl.BlockSpec((rt, D_MODEL), row),
            pl.BlockSpec((rt, 1), row),
            pl.BlockSpec((rt, 1), row),
            pl.BlockSpec((1, D_MODEL), lambda i, *_: (0, 0)),
            pl.BlockSpec(memory_space=pl.ANY),
        ],
        out_specs=pl.BlockSpec((rt, D_MODEL), row),
        scratch_shapes=[
            pltpu.VMEM((2, rt, D_MODEL), F32),
            pltpu.VMEM((2, rt, D_MODEL), F32),
            pltpu.SemaphoreType.DMA((2,)),
        ],
    )
    return pl.pallas_call(
        functools.partial(_combine_kernel, base=base),
        grid_spec=grid_spec,
        out_shape=jax.ShapeDtypeStruct((m, D_MODEL), F32),
        compiler_params=_params(1),
        name="moe_combine",
    )(pos_a, pos_b, h, gate_a, gate_b, g, y)


def _token_cumsum(sel):
    n_e, n_tok = sel.shape
    blocks = n_tok // LANES
    s = sel.astype(F32).reshape(n_e, blocks, LANES)
    idx = jnp.arange(LANES)
    within = jnp.einsum("ebl,lm->ebm", s, (idx[:, None] <= idx[None, :]).astype(F32),
                        precision=lax.Precision.HIGHEST)
    totals = within[:, :, -1]
    bidx = jnp.arange(blocks)
    offsets = jnp.einsum("eb,bc->ec", totals, (bidx[:, None] < bidx[None, :]).astype(F32),
                         precision=lax.Precision.HIGHEST)
    return (within + offsets[:, :, None]).astype(jnp.int32).reshape(n_e, n_tok)


def _route_tables(sel, comb, n_tiles):
    tm = MOE_TILE
    sel = sel.T
    comb = comb.T
    cum = _token_cumsum(sel)
    counts = cum[:, -1]
    padded = ((counts + tm - 1) // tm) * tm
    ends = jnp.cumsum(padded)
    starts = ends - padded
    slot = starts[:, None] + cum - sel
    order = jnp.cumsum(sel, axis=0)
    first = (sel == 1) & (order == 1)
    second = (sel == 1) & (order == 2)
    pos_a = jnp.sum(jnp.where(first, slot, 0), axis=0).astype(jnp.int32)
    pos_b = jnp.sum(jnp.where(second, slot, 0), axis=0).astype(jnp.int32)
    gate_a = jnp.sum(jnp.where(first, comb, 0.0), axis=0)[:, None]
    gate_b = jnp.sum(jnp.where(second, comb, 0.0), axis=0)[:, None]
    tile_start = jnp.arange(n_tiles, dtype=jnp.int32) * tm
    tile_expert = jnp.minimum(jnp.sum(tile_start[:, None] >= ends[None, :], axis=1), N_EXPERTS - 1)
    rows_in_tile = (starts + counts)[tile_expert] - tile_start
    quarter = tm // MOE_QUARTERS
    tile_quarters = jnp.clip((rows_in_tile + quarter - 1) // quarter, 0, MOE_QUARTERS).astype(jnp.int32)
    nvalid = (ends[-1:] // tm).astype(jnp.int32)
    last_tile = jnp.maximum(ends - tm, 0).astype(jnp.int32)
    has_rows = (counts > 0).astype(jnp.int32)
    return (pos_a, pos_b, gate_a, gate_b, tile_expert.astype(jnp.int32), tile_quarters, nvalid,
            last_tile, has_rows)


def kernel(x_prompt, x_sample, state_pool, cache_k, cache_v, meta_tokens, g_pool, w_pool, ls_pool, g_ffn0, w_ff1, w_ff3, w_ff2, g_kv, w_kv, b_kv, g_attn, w_q, b_q, sinks, w_o, b_o, g_ffn1, w_router, w_e1, w_e3, w_e2, g_final):
    n_seq, seq, _ = x_prompt.shape
    n_dec, n_t, _ = x_sample.shape
    past_len = PAST_LEN
    window = cache_k.shape[1]
    assert window == WINDOW and seq % POOL_TILE == 0 and N_META <= POOL_TILE

    vec = lambda a: a.reshape(1, -1).astype(F32)
    g_pool, ls_pool, g_ffn0, g_kv, b_kv, g_attn, b_q, b_o, g_ffn1, g_final = map(
        vec, (g_pool, ls_pool, g_ffn0, g_kv, b_kv, g_attn, b_q, b_o, g_ffn1, g_final))
    w_pool, w_ff1, w_ff3, w_ff2 = (w.astype(BF16) for w in (w_pool, w_ff1, w_ff3, w_ff2))
    w_kv, w_q, w_o = (w.astype(F32) for w in (w_kv, w_q, w_o))
    w_router = jnp.pad(w_router, ((0, 0), (0, LANES - N_EXPERTS))).astype(BF16)
    meta = meta_tokens.astype(F32)

    def layer0_tail(h1, tm_ffn, tm_proj):
        h2 = _ffn(h1, g_ffn0, w_ff1, w_ff3, w_ff2, min(tm_ffn, h1.shape[0]))
        kv, q = _qkv(h2, g_kv, w_kv, b_kv, g_attn, w_q, b_q, tm_proj)
        return h2, kv, q

    x_meta = jnp.pad(meta, ((0, POOL_TILE - N_META), (0, 0)))[None]
    h1_m, _ = _pool_seq(x_meta, jnp.zeros_like(meta), g_pool, w_pool, ls_pool, 0)

    h1_s, pool_s = _pool_step(x_sample.transpose(1, 0, 2), state_pool.transpose(1, 0, 2),
                              g_pool, w_pool, ls_pool, past_len)
    n_s = n_t * n_dec
    h1_ms = jnp.concatenate([h1_m[0], h1_s.reshape(n_s, D_MODEL)])
    h2_ms, kv_ms, q_ms = layer0_tail(h1_ms, h1_ms.shape[0], POOL_TILE)
    h2_s, kv_s, q_s = h2_ms[POOL_TILE:], kv_ms[POOL_TILE:], q_ms[POOL_TILE:]
    past_kv = jnp.pad(kv_ms[:N_META], ((WINDOW - N_META, 0), (0, 0)))

    h1_p, pool_p = _pool_seq(x_prompt, meta, g_pool, w_pool, ls_pool, N_META)
    h2_p, kv_p, q_p = layer0_tail(h1_p.reshape(n_seq * seq, D_MODEL), FFN_TILE, QKV_TILE)
    o_p = _attn_seq(q_p, kv_p, past_kv, sinks.astype(F32), n_seq, N_META)
    h3_p, u_p, comb_p, sel_p = _oproj(o_p, h2_p, w_o, b_o, g_ffn1, w_router, PROJ_TILE)
    kv_tail = kv_p.reshape(n_seq, seq, 2 * HKV)[:, seq - WINDOW:]
    k_p = kv_tail[..., :HKV].reshape(n_seq, WINDOW, N_KV_HEADS, HEAD_DIM)
    v_p = kv_tail[..., HKV:].reshape(n_seq, WINDOW, N_KV_HEADS, HEAD_DIM)

    q5 =q_s.reshape(n_t, n_dec, N_KV_HEADS, GROUP, HEAD_DIM).transpose(1, 2, 0, 3, 4)
    eye = jnp.eye(N_KV_HEADS, dtype=BF16)
    q_bd = (q5[:, :, :, :, None, :] * eye[None, :, None, None, :, None]).reshape(
        n_dec, N_KV_HEADS * n_t * GROUP, HKV)
    new_kv = jnp.pad(kv_s.reshape(n_t, n_dec, 2 * HKV).transpose(1, 0, 2),
                     ((0, 0), (0, BF16_SUBLANES - n_t), (0, 0)))
    sink_rows = jnp.broadcast_to(sinks.astype(F32).reshape(N_KV_HEADS, 1, GROUP),
                                 (N_KV_HEADS, n_t, GROUP)).reshape(-1, 1)
    o_s4, k_s, v_s = _attn_step(q_bd, new_kv, cache_k.reshape(n_dec, WINDOW, HKV),
                                cache_v.reshape(n_dec, WINDOW, HKV), sink_rows, past_len, n_t)
    o_s = o_s4.reshape(n_dec, N_KV_HEADS, n_t, GROUP, HEAD_DIM).transpose(2, 0, 1, 3, 4).reshape(n_s, D_MODEL)
    h3_s, u_s, comb_s, sel_s = _oproj(o_s, h2_s, w_o, b_o, g_ffn1, w_router, PROJ_TILE)

    n_tok = n_seq * seq + n_s
    min_tiles = 2 * n_tok // MOE_TILE
    n_tiles = min_tiles + N_EXPERTS
    pos_a, pos_b, gate_a, gate_b, tile_expert, tile_quarters, nvalid, last_tile, has_rows = _route_tables(
        jnp.concatenate([sel_p, sel_s]), jnp.concatenate([comb_p, comb_s]), n_tiles)
    xs = _dispatch(pos_a, pos_b, last_tile, has_rows, nvalid, u_p, u_s, n_tiles, min_tiles)
    y = _moe(tile_expert, tile_quarters, nvalid, xs, w_e1.astype(F32), w_e3.astype(F32), w_e2.astype(F32))
    n_p = n_seq * seq
    y_p = _combine(pos_a, pos_b, h3_p, gate_a[:n_p], gate_b[:n_p], g_final, y, 0)
    y_s = _combine(pos_a, pos_b, h3_s, gate_a[n_p:], gate_b[n_p:], g_final, y, n_p)

    return (y_p.reshape(n_seq, seq, D_MODEL),
            y_s.reshape(n_t, n_dec, D_MODEL).transpose(1, 0, 2),
            pool_p,
            pool_s.transpose(1, 0, 2),
            k_p, v_p,
            k_s.reshape(n_dec, WINDOW, N_KV_HEADS, HEAD_DIM),
            v_s.reshape(n_dec, WINDOW, N_KV_HEADS, HEAD_DIM))
```
